```python
import jax, jax.numpy as jnp
from jax import lax
import numpy as np

D_MODEL = 2048
BATCH = 1
SEQ = 8192
DEPTH = 2

GRID_W = 64
CTX_LEN = 256
EPS = 1e-6
ROPE_THETA = 10000.0
Q_BLOCK = 128
HEAD_DIM = 128
NA_HEADS = 8
NA_KH = 8
NA_KW = 16
MLA_HEADS = 8
MLA_Q_RANK = 512
MLA_KV_RANK = 512
MLA_NOPE = 128
MLA_ROPE = 64
MLA_V = 128
EVEN_IN_WIDTHS = (NA_HEADS * HEAD_DIM, NA_HEADS * HEAD_DIM, NA_HEADS * HEAD_DIM, MLA_Q_RANK, MLA_KV_RANK, MLA_ROPE)
EVEN_OUT_WIDTH = NA_HEADS * HEAD_DIM + MLA_HEADS * MLA_V
GQA_HEADS = 16
GQA_KV = 4
GQA_GROUP = GQA_HEADS // GQA_KV
N_EXPERTS = 64
N_GROUPS = 8
TOPK_GROUPS = 4
MOE_TOP_K = 8
D_EXPERT = 512
D_SHARED = 512
ROUTED_SCALE = 2.5
EXPERT_BLOCK = 128
N_EVEN = (DEPTH + 1) // 2
N_ODD = DEPTH // 2

kernel_name = 'hybrid_na_mla_gqa_moe_dit'


def rmsnorm(x, g):
    xf = x.astype(jnp.float32)
    y = xf * lax.rsqrt(jnp.mean(xf * xf, axis=-1, keepdims=True) + EPS)
    return y.astype(x.dtype) * g


def modulate(x, shift, scale):
    return x * (1 + scale) + shift


def to_heads(t, n):
    B, T, _ = t.shape
    return t.reshape(B, T, n, -1).transpose(0, 2, 1, 3)


def from_heads(o):
    B, H, T, d = o.shape
    return o.transpose(0, 2, 1, 3).reshape(B, T, H * d)


def rope_tables(S, d):
    t = jnp.arange(S)
    row = (t // GRID_W).astype(jnp.float32)
    col = (t % GRID_W).astype(jnp.float32)
    n = d // 4
    inv = ROPE_THETA ** (-jnp.arange(n, dtype=jnp.float32) / n)
    ang = jnp.concatenate([row[:, None] * inv, col[:, None] * inv], axis=-1)
    ang = jnp.concatenate([jnp.zeros((CTX_LEN, d // 2), jnp.float32), ang], axis=0)
    return jnp.cos(ang), jnp.sin(ang)


def apply_rope(x, cos, sin):
    cos = cos.astype(x.dtype)
    sin = sin.astype(x.dtype)
    x1, x2 = x[..., 0::2], x[..., 1::2]
    return jnp.stack([x1 * cos - x2 * sin, x1 * sin + x2 * cos], axis=-1).reshape(x.shape)


def attend_blocks(q, k, v, scale):
    B, Hk, G, Sq, dk = q.shape
    nb = Sq // Q_BLOCK
    qb = q.reshape(B, Hk, G, nb, Q_BLOCK, dk).transpose(3, 0, 1, 2, 4, 5)

    def one(qi):
        s = jnp.einsum('bkgqd,bksd->bkgqs', qi, k).astype(jnp.float32) * scale
        p = jax.nn.softmax(s, axis=-1).astype(v.dtype)
        return jnp.einsum('bkgqs,bksd->bkgqd', p, v)

    o = lax.map(one, qb)
    return o.transpose(1, 2, 3, 0, 4, 5).reshape(B, Hk, G, Sq, v.shape[-1])


def neighbourhood_attention(q, k, v, rpb, k_ctx, v_ctx):
    B, H, S, dh = q.shape
    rows = S // GRID_W
    kh = min(NA_KH, rows)
    scale = dh ** -0.5
    qg = q.reshape(B, H, rows, GRID_W, dh)
    kg = k.reshape(B, H, rows, GRID_W, dh)
    vg = v.reshape(B, H, rows, GRID_W, dh)
    cols = jnp.arange(GRID_W)
    c0 = jnp.clip(cols - NA_KW // 2, 0, GRID_W - NA_KW)
    col_idx = c0[:, None] + jnp.arange(NA_KW)
    col_rel = col_idx - cols[:, None] + NA_KW - 1

    def row_block(r):
        r0 = jnp.clip(r - kh // 2, 0, rows - kh)
        q_r = lax.dynamic_index_in_dim(qg, r, axis=2, keepdims=False)
        k_win = lax.dynamic_slice_in_dim(kg, r0, kh, axis=2)[:, :, :, col_idx]
        v_win = lax.dynamic_slice_in_dim(vg, r0, kh, axis=2)[:, :, :, col_idx]
        row_rel = r0 + jnp.arange(kh) - r + NA_KH - 1
        bias = rpb[:, row_rel][:, :, col_rel].transpose(0, 2, 1, 3)
        s_loc = jnp.einsum('bhwd,bhiwjd->bhwij', q_r, k_win).astype(jnp.float32) * scale + bias.astype(jnp.float32)
        s_ctx = jnp.einsum('bhwd,bhld->bhwl', q_r, k_ctx).astype(jnp.float32) * scale
        s = jnp.concatenate([s_loc.reshape(B, H, GRID_W, kh * NA_KW), s_ctx], axis=-1)
        p = jax.nn.softmax(s, axis=-1).astype(v.dtype)
        p_loc = p[..., :kh * NA_KW].reshape(B, H, GRID_W, kh, NA_KW)
        p_ctx = p[..., kh * NA_KW:]
        return jnp.einsum('bhwij,bhiwjd->bhwd', p_loc, v_win) + jnp.einsum('bhwl,bhld->bhwd', p_ctx, v_ctx)

    o = lax.map(row_block, jnp.arange(rows))
    return o.transpose(1, 2, 0, 3, 4).reshape(B, H, S, dh)


def mixer_even(h, need_ctx, w_in, rpb, q_norm, w_uq, kv_norm, w_ukv, w_out):
    B, T, _ = h.shape
    L = CTX_LEN
    S = T - L
    splits = np.cumsum(EVEN_IN_WIDTHS)[:-1].tolist()
    qa, ka, va, cq, ckv, kr = jnp.split(h @ w_in, splits, axis=-1)
    qa, ka, va = to_heads(qa, NA_HEADS), to_heads(ka, NA_HEADS), to_heads(va, NA_HEADS)
    o_na = neighbourhood_attention(qa[:, :, L:], ka[:, :, L:], va[:, :, L:], rpb, ka[:, :, :L], va[:, :, :L])
    cos, sin = rope_tables(S, MLA_ROPE)
    q = to_heads(rmsnorm(cq, q_norm) @ w_uq, MLA_HEADS)
    kv = to_heads(rmsnorm(ckv, kv_norm) @ w_ukv, MLA_HEADS)
    q = jnp.concatenate([q[..., :MLA_NOPE], apply_rope(q[..., MLA_NOPE:], cos, sin)], axis=-1)
    k_rope = apply_rope(kr, cos, sin)[:, None]
    k = jnp.concatenate([kv[..., :MLA_NOPE], jnp.broadcast_to(k_rope, (B, MLA_HEADS, T, MLA_ROPE))], axis=-1)
    v = kv[..., MLA_NOPE:]
    mla_scale = (MLA_NOPE + MLA_ROPE) ** -0.5
    o_mla = attend_blocks(q[:, :, None, L:], k, v, mla_scale)[:, :, 0]
    if need_ctx:
        o_na_c = attend_blocks(qa[:, :, None, :L], ka[:, :, :L], va[:, :, :L], HEAD_DIM ** -0.5)[:, :, 0]
        o_mla_c = attend_blocks(q[:, :, None, :L], k[:, :, :L], v[:, :, :L], mla_scale)[:, :, 0]
        o_na = jnp.concatenate([o_na_c, o_na], axis=2)
        o_mla = jnp.concatenate([o_mla_c, o_mla], axis=2)
    o = jnp.concatenate([from_heads(o_na), from_heads(o_mla)], axis=-1)
    return o @ w_out


def mixer_odd(h, need_ctx, w_qkv, q_norm, k_norm, w_o):
    B, T, _ = h.shape
    L = CTX_LEN
    S = T - L
    q, k, v = jnp.split(h @ w_qkv, [GQA_HEADS * HEAD_DIM, (GQA_HEADS + GQA_KV) * HEAD_DIM], axis=-1)
    cos, sin = rope_tables(S, HEAD_DIM)
    q = apply_rope(rmsnorm(to_heads(q, GQA_HEADS), q_norm), cos, sin)
    k = apply_rope(rmsnorm(to_heads(k, GQA_KV), k_norm), cos, sin)
    v = to_heads(v, GQA_KV)
    q = q.reshape(B, GQA_KV, GQA_GROUP, T, HEAD_DIM)
    scale = HEAD_DIM ** -0.5
    o = attend_blocks(q[:, :, :, L:], k, v, scale)
    if need_ctx:
        o_c = attend_blocks(q[:, :, :, :L], k[:, :, :L], v[:, :, :L], scale)
        o = jnp.concatenate([o_c, o], axis=3)
    o = o.reshape(B, GQA_HEADS, -1, HEAD_DIM)
    return from_heads(o) @ w_o


def swiglu(x, wg, wu, wd):
    return (jax.nn.silu(x @ wg) * (x @ wu)) @ wd


def routed_experts(xf, eidx, ew, w_gate, w_up, w_down):
    N, D = xf.shape
    A = N * MOE_TOP_K
    e_flat = eidx.reshape(-1)
    order = jnp.argsort(e_flat)
    e_sorted = e_flat[order]
    tok_sorted = (order // MOE_TOP_K).astype(jnp.int32)
    w_sorted = ew.reshape(-1)[order]
    counts = jnp.bincount(e_flat, length=N_EXPERTS)
    padded = (counts + EXPERT_BLOCK - 1) // EXPERT_BLOCK * EXPERT_BLOCK
    pad_end = jnp.cumsum(padded)
    pad_start = pad_end - padded
    start = jnp.cumsum(counts) - counts
    dest = pad_start[e_sorted] + jnp.arange(A) - start[e_sorted]
    n_blocks = -(-A // EXPERT_BLOCK) + N_EXPERTS
    P = n_blocks * EXPERT_BLOCK
    tok_buf = jnp.full((P,), N, jnp.int32).at[dest].set(tok_sorted)
    w_buf = jnp.zeros((P,), xf.dtype).at[dest].set(w_sorted)
    block_expert = jnp.minimum(jnp.searchsorted(pad_end, jnp.arange(n_blocks) * EXPERT_BLOCK, side='right'), N_EXPERTS - 1)
    xpad = jnp.concatenate([xf, jnp.zeros((1, D), xf.dtype)], axis=0)

    def run_block(args):
        tok, e = args
        return swiglu(xpad[tok], w_gate[e], w_up[e], w_down[e])

    y = lax.map(run_block, (tok_buf.reshape(n_blocks, EXPERT_BLOCK), block_expert))
    y = y.reshape(P, D) * w_buf[:, None]
    return jax.ops.segment_sum(y, tok_buf, num_segments=N + 1)[:N]


def moe(h, router_w, router_b, w_gate, w_up, w_down, sh_gate, sh_up, sh_down):
    B, T, D = h.shape
    xf = h.reshape(B * T, D)
    N = xf.shape[0]
    scores = jax.nn.sigmoid((xf @ router_w).astype(jnp.float32))
    sel = scores + router_b.astype(jnp.float32)
    group_score = lax.top_k(sel.reshape(N, N_GROUPS, N_EXPERTS // N_GROUPS), 2)[0].sum(-1)
    _, gidx = lax.top_k(group_score, TOPK_GROUPS)
    gmask = jax.nn.one_hot(gidx, N_GROUPS, dtype=jnp.float32).sum(1) > 0
    emask = jnp.repeat(gmask, N_EXPERTS // N_GROUPS, axis=1)
    _, eidx = lax.top_k(jnp.where(emask, sel, -jnp.inf), MOE_TOP_K)
    ew = jnp.take_along_axis(scores, eidx, axis=-1)
    ew = (ew / jnp.sum(ew, axis=-1, keepdims=True) * ROUTED_SCALE).astype(xf.dtype)
    out = routed_experts(xf, eidx, ew, w_gate, w_up, w_down) + swiglu(xf, sh_gate, sh_up, sh_down)
    return out.reshape(B, T, D)


def setup_inputs(seed: int = 0) -> dict:
    key = jax.random.key(seed)
    ks = iter(jax.random.split(key, 40))
    D = D_MODEL

    def nrm(shape, scale):
        return jax.random.normal(next(ks), shape, jnp.float32) * scale

    def gain(shape):
        return 1.0 + nrm(shape, 0.05)

    return {
        'x': nrm((BATCH, SEQ, D), 1.0),
        'c': nrm((BATCH, D), 1.0),
        'ctx': nrm((BATCH, CTX_LEN, D), 1.0),
        'c_ctx': nrm((D,), 1.0),
        'ada_w': nrm((DEPTH, D, 6 * D), 0.5 * D ** -0.5),
        'ada_b': nrm((DEPTH, 6 * D), 0.02),
        'mix_norm': gain((DEPTH, D)),
        'ffn_norm': gain((DEPTH, D)),
        'ev_w_in': nrm((N_EVEN, D, sum(EVEN_IN_WIDTHS)), D ** -0.5),
        'ev_rpb': nrm((N_EVEN, NA_HEADS, 2 * NA_KH - 1, 2 * NA_KW - 1), 0.1),
        'ev_q_norm': gain((N_EVEN, MLA_Q_RANK)),
        'ev_w_uq': nrm((N_EVEN, MLA_Q_RANK, MLA_HEADS * (MLA_NOPE + MLA_ROPE)), MLA_Q_RANK ** -0.5),
        'ev_kv_norm': gain((N_EVEN, MLA_KV_RANK)),
        'ev_w_ukv': nrm((N_EVEN, MLA_KV_RANK, MLA_HEADS * (MLA_NOPE + MLA_V)), MLA_KV_RANK ** -0.5),
        'ev_w_out': nrm((N_EVEN, EVEN_OUT_WIDTH, D), EVEN_OUT_WIDTH ** -0.5),
        'od_w_qkv': nrm((N_ODD, D, (GQA_HEADS + 2 * GQA_KV) * HEAD_DIM), D ** -0.5),
        'od_q_norm': gain((N_ODD, HEAD_DIM)),
        'od_k_norm': gain((N_ODD, HEAD_DIM)),
        'od_w_o': nrm((N_ODD, GQA_HEADS * HEAD_DIM, D), (GQA_HEADS * HEAD_DIM) ** -0.5),
        'router_w': nrm((DEPTH, D, N_EXPERTS), D ** -0.5),
        'router_b': nrm((DEPTH, N_EXPERTS), 0.01),
        'exp_w_gate': nrm((DEPTH, N_EXPERTS, D, D_EXPERT), D ** -0.5),
        'exp_w_up': nrm((DEPTH, N_EXPERTS, D, D_EXPERT), D ** -0.5),
        'exp_w_down': nrm((DEPTH, N_EXPERTS, D_EXPERT, D), D_EXPERT ** -0.5),
        'sh_w_gate': nrm((DEPTH, D, D_SHARED), D ** -0.5),
        'sh_w_up': nrm((DEPTH, D, D_SHARED), D ** -0.5),
        'sh_w_down': nrm((DEPTH, D_SHARED, D), D_SHARED ** -0.5),
        'final_norm': gain((D,)),
    }


def reference(x, c, ctx, c_ctx, ada_w, ada_b, mix_norm, ffn_norm, ev_w_in, ev_rpb, ev_q_norm, ev_w_uq, ev_kv_norm, ev_w_ukv, ev_w_out, od_w_qkv, od_q_norm, od_k_norm, od_w_o, router_w, router_b, exp_w_gate, exp_w_up, exp_w_down, sh_w_gate, sh_w_up, sh_w_down, final_norm):
    S = x.shape[1]
    L = CTX_LEN
    x_lat, x_ctx = x, ctx
    for i in range(DEPTH):
        last = i == DEPTH - 1
        j = i // 2
        m_l = jnp.split((jax.nn.silu(c) @ ada_w[i] + ada_b[i])[:, None, :], 6, axis=-1)
        m_c = jnp.split(jax.nn.silu(c_ctx) @ ada_w[i] + ada_b[i], 6, axis=-1)
        h = jnp.concatenate([modulate(rmsnorm(x_ctx, mix_norm[i]), m_c[0], m_c[1]),
                             modulate(rmsnorm(x_lat, mix_norm[i]), m_l[0], m_l[1])], axis=1)
        if i % 2 == 0:
            o = mixer_even(h, not last, ev_w_in[j], ev_rpb[j], ev_q_norm[j], ev_w_uq[j], ev_kv_norm[j], ev_w_ukv[j], ev_w_out[j])
        else:
            o = mixer_odd(h, not last, od_w_qkv[j], od_q_norm[j], od_k_norm[j], od_w_o[j])
        x_lat = x_lat + m_l[2] * o[:, -S:]
        h_lat = modulate(rmsnorm(x_lat, ffn_norm[i]), m_l[3], m_l[4])
        moe_p = (router_w[i], router_b[i], exp_w_gate[i], exp_w_up[i], exp_w_down[i], sh_w_gate[i], sh_w_up[i], sh_w_down[i])
        if last:
            x_lat = x_lat + m_l[5] * moe(h_lat, *moe_p)
        else:
            x_ctx = x_ctx + m_c[2] * o[:, :L]
            h_ctx = modulate(rmsnorm(x_ctx, ffn_norm[i]), m_c[3], m_c[4])
            f = moe(jnp.concatenate([h_ctx, h_lat], axis=1), *moe_p)
            x_ctx = x_ctx + m_c[5] * f[:, :L]
            x_lat = x_lat + m_l[5] * f[:, L:]
    return rmsnorm(x_lat, final_norm)
```

```python
import functools

import numpy as np
import jax
import jax.numpy as jnp
from jax import lax
from jax.experimental import pallas as pl
from jax.experimental.pallas import tpu as pltpu

F32 = jnp.float32
BF16 = jnp.bfloat16

D_MODEL = 2048
SEQ = 8192
DEPTH = 2
GRID_W = 64
GRID_H = SEQ // GRID_W
CTX_LEN = 256
N_TOK = SEQ + CTX_LEN
EPS = 1e-6
ROPE_THETA = 10000.0
HEAD_DIM = 128
NA_HEADS = 8
NA_KH = 8
NA_KW = 16
MLA_HEADS = 8
MLA_Q_RANK = 512
MLA_KV_RANK = 512
MLA_NOPE = 128
MLA_ROPE = 64
MLA_V = 128
MLA_QK_PAD = 256
GQA_HEADS = 16
GQA_KV = 4
GQA_GROUP = GQA_HEADS // GQA_KV
N_EXPERTS = 64
N_GROUPS = 8
GROUP_SIZE = N_EXPERTS // N_GROUPS
TOPK_GROUPS = 4
MOE_TOP_K = 8
D_EXPERT = 512
ROUTED_SCALE = 2.5

NEG = -1e30

VMEM_LIMIT_CAP = 56 * 1024 * 1024

ROW_TILE = 256
MM_TM = 768
NA_RB = 8
NA_U = NA_RB + NA_KH - 1
EXPERT_TILE = 256
COMBINE_TILE = 128


def _cparams(sem, vmem_bytes):
    return pltpu.CompilerParams(
        dimension_semantics=sem,
        vmem_limit_bytes=int(min(max(vmem_bytes, 16 * 1024 * 1024), VMEM_LIMIT_CAP)),
    )


def _dot(a, b):
    return jnp.dot(a, b, preferred_element_type=F32)


def _dot_nt(a, b):
    return lax.dot_general(a, b, (((1,), (1,)), ((), ())), preferred_element_type=F32)


def _mods_kernel(cc_ref, w_ref, b_ref, o_ref, acc_ref, *, nk):
    k = pl.program_id(2)

    @pl.when(k == 0)
    def _():
        acc_ref[...] = jnp.zeros_like(acc_ref)

    a = cc_ref[...]
    a = a * jax.nn.sigmoid(a)
    w = w_ref[...]
    tk, tn = w.shape
    w3 = w.reshape(tk // 8, 8, tn)
    for v in range(2):
        col = a[:, v:v + 1].reshape(tk // 8, 8, 1)
        acc_ref[v] += jnp.sum(w3 * col, axis=0)

    @pl.when(k == nk - 1)
    def _():
        o_ref[...] = jnp.sum(acc_ref[...], axis=1) + b_ref[...]


def _adaln_mods(c, c_ctx, ada_w, ada_b):
    depth, d, n = ada_w.shape
    tk, tn = 512, 1536
    nk = d // tk
    cc = jnp.stack([c[0], c_ctx], axis=1)
    return pl.pallas_call(
        functools.partial(_mods_kernel, nk=nk),
        grid=(depth, n // tn, nk),
        in_specs=[
            pl.BlockSpec((tk, 2), lambda i, j, k: (k, 0)),
            pl.BlockSpec((None, tk, tn), lambda i, j, k: (i, k, j)),
            pl.BlockSpec((None, 1, tn), lambda i, j, k: (i, 0, j)),
        ],
        out_specs=pl.BlockSpec((None, 2, tn), lambda i, j, k: (i, 0, j)),
        out_shape=jax.ShapeDtypeStruct((depth, 2, n), F32),
        scratch_shapes=[pltpu.VMEM((2, 8, tn), F32)],
        compiler_params=_cparams(("arbitrary", "arbitrary", "arbitrary"), 4 * tk * tn * 4),
        name="adaln_mods",
    )(cc, ada_w, ada_b.reshape(depth, 1, n))


def _router_select(hf, rwt_ref, rb_ref):
    logits = lax.dot_general(rwt_ref[...], hf, (((1,), (1,)), ((), ())),
                             precision=lax.Precision.HIGHEST, preferred_element_type=F32)
    scores = jax.nn.sigmoid(logits)
    sel = scores + rb_ref[...]
    tm = sel.shape[1]
    ninf = jnp.float32(-jnp.inf)
    sub = lax.broadcasted_iota(jnp.int32, (GROUP_SIZE, tm), 0)
    gi = lax.broadcasted_iota(jnp.int32, (N_GROUPS, tm), 0)
    gs = jnp.zeros((N_GROUPS, tm), F32)
    for g in range(N_GROUPS):
        blk = sel[g * GROUP_SIZE:(g + 1) * GROUP_SIZE, :]
        m1 = jnp.max(blk, axis=0, keepdims=True)
        first = jnp.min(jnp.where(blk == m1, sub, GROUP_SIZE), axis=0, keepdims=True)
        m2 = jnp.max(jnp.where(sub == first, ninf, blk), axis=0, keepdims=True)
        gs = jnp.where(gi == g, m1 + m2, gs)
    gsel = jnp.zeros((N_GROUPS, tm), F32)
    cur = gs
    for _ in range(TOPK_GROUPS):
        mx = jnp.max(cur, axis=0, keepdims=True)
        idx = jnp.min(jnp.where(cur == mx, gi, N_GROUPS), axis=0, keepdims=True)
        pick = gi == idx
        gsel = jnp.where(pick, 1.0, gsel)
        cur = jnp.where(pick, ninf, cur)
    ei = lax.broadcasted_iota(jnp.int32, (N_EXPERTS, tm), 0)
    eg = ei // GROUP_SIZE
    emask = jnp.zeros((N_EXPERTS, tm), F32)
    for g in range(N_GROUPS):
        emask = jnp.where(eg == g, gsel[g:g + 1, :], emask)
    cur = jnp.where(emask > 0.0, sel, ninf)
    idxs, ws = [], []
    for _ in range(MOE_TOP_K):
        mx = jnp.max(cur, axis=0, keepdims=True)
        idx = jnp.min(jnp.where(cur == mx, ei, N_EXPERTS), axis=0, keepdims=True)
        pick = ei == idx
        ws.append(jnp.sum(jnp.where(pick, scores, 0.0), axis=0, keepdims=True))
        idxs.append(idx)
        cur = jnp.where(pick, ninf, cur)
    total = ws[0]
    for w in ws[1:]:
        total = total + w
    ws = [w / total * ROUTED_SCALE for w in ws]
    return idxs, ws


def _norm_kernel(*refs, has_resid, modulated, with_router, out_dtype):
    it = iter(refs)
    x_ref = next(it)
    if has_resid:
        o_ref = next(it)
        gate_ref = next(it)
    g_ref = next(it)
    if modulated:
        shift_ref = next(it)
        scale_ref = next(it)
    if with_router:
        rwt_ref = next(it)
        rb_ref = next(it)
    if has_resid:
        xn_ref = next(it)
    h_ref = next(it)
    if with_router:
        h32_ref = next(it)
        eidx_ref = next(it)
        ew_ref = next(it)

    x = x_ref[...]
    if has_resid:
        x = x + gate_ref[...] * o_ref[...].astype(F32)
        xn_ref[...] = x
    y = x * lax.rsqrt(jnp.mean(x * x, axis=-1, keepdims=True) + EPS)
    y = y * g_ref[...]
    if modulated:
        y = y * (1.0 + scale_ref[...]) + shift_ref[...]
    h_ref[...] = y.astype(out_dtype)
    if with_router:
        h32_ref[...] = y
        idxs, ws = _router_select(y, rwt_ref, rb_ref)
        for k in range(MOE_TOP_K):
            eidx_ref[k:k + 1, :] = idxs[k]
            ew_ref[k:k + 1, :] = ws[k]


def _norm_call(x, n_rows, gain, *, resid=None, mod=None, router=None, out_dtype=BF16):
    d = x.shape[1]
    tm = ROW_TILE
    nblk = n_rows // tm
    lat_blocks = SEQ // tm

    def cls_map(i):
        return (jnp.where(i >= lat_blocks, 1, 0), 0, 0)

    row = pl.BlockSpec((tm, d), lambda i: (i, 0))
    vec = pl.BlockSpec((None, 1, d), cls_map)
    args, in_specs = [x], [row]
    if resid is not None:
        o, gate = resid
        args += [o, gate.reshape(2, 1, d)]
        in_specs += [row, vec]
    args.append(gain.reshape(1, d))
    in_specs.append(pl.BlockSpec((1, d), lambda i: (0, 0)))
    if mod is not None:
        shift, scale = mod
        args += [shift.reshape(2, 1, d), scale.reshape(2, 1, d)]
        in_specs += [vec, vec]
    if router is not None:
        rw, rb = router
        args += [rw.T, rb.reshape(N_EXPERTS, 1)]
        in_specs += [pl.BlockSpec((N_EXPERTS, d), lambda i: (0, 0)),
                     pl.BlockSpec((N_EXPERTS, 1), lambda i: (0, 0))]
    out_shape, out_specs = [], []
    if resid is not None:
        out_shape.append(jax.ShapeDtypeStruct((n_rows, d), F32))
        out_specs.append(row)
    out_shape.append(jax.ShapeDtypeStruct((n_rows, d), out_dtype))
    out_specs.append(row)
    if router is not None:
        out_shape += [jax.ShapeDtypeStruct((n_rows, d), F32),
                      jax.ShapeDtypeStruct((MOE_TOP_K, n_rows), jnp.int32),
                      jax.ShapeDtypeStruct((MOE_TOP_K, n_rows), F32)]
        out_specs += [row, pl.BlockSpec((MOE_TOP_K, tm), lambda i: (0, i)),
                      pl.BlockSpec((MOE_TOP_K, tm), lambda i: (0, i))]
    return pl.pallas_call(
        functools.partial(_norm_kernel, has_resid=resid is not None, modulated=mod is not None,
                          with_router=router is not None, out_dtype=out_dtype),
        grid=(nblk,),
        in_specs=in_specs,
        out_specs=out_specs,
        out_shape=out_shape,
        compiler_params=_cparams(("arbitrary",), 16 * tm * d * 4),
        name="norm_router" if router is not None else "norm",
    )(*args)


def _mm_kernel(a_ref, w_ref, o_ref, wb_ref):
    @pl.when(pl.program_id(1) == 0)
    def _():
        wb_ref[...] = w_ref[...].astype(BF16)

    o_ref[...] = _dot(a_ref[...], wb_ref[...]).astype(o_ref.dtype)


def _matmul(a, w, m_rows, out_dtype, tm, tn):
    k, n = w.shape
    assert m_rows % tm == 0 and n % tn == 0 and a.shape[1] == k
    osz = jnp.dtype(out_dtype).itemsize
    vmem = 2 * k * tn * 4 + k * tn * 2 + 2 * tm * k * 2 + 2 * tm * tn * osz + tm * tn * 4
    return pl.pallas_call(
        _mm_kernel,
        grid=(n // tn, m_rows // tm),
        in_specs=[pl.BlockSpec((tm, k), lambda j, i: (i, 0)),
                  pl.BlockSpec((k, tn), lambda j, i: (0, j))],
        out_specs=pl.BlockSpec((tm, tn), lambda j, i: (i, j)),
        out_shape=jax.ShapeDtypeStruct((m_rows, n), out_dtype),
        scratch_shapes=[pltpu.VMEM((k, tn), BF16)],
        compiler_params=_cparams(("arbitrary", "arbitrary"), vmem + (4 << 20)),
        name="matmul",
    )(a, w)


def _rope(x, cos_e, sin_s):
    w = x.shape[-1]
    lane = lax.broadcasted_iota(jnp.int32, x.shape, 1)
    nxt = pltpu.roll(x, w - 1, 1)
    prv = pltpu.roll(x, 1, 1)
    swapped = jnp.where(lane % 2 == 0, nxt, prv)
    return x * cos_e + swapped * sin_s


def _rms(x, g):
    return x * lax.rsqrt(jnp.mean(x * x, axis=-1, keepdims=True) + EPS) * g


def _mla_prep1_kernel(lat_ref, qg_ref, kvg_ref, cos_ref, sin_ref, nq_ref, nkv_ref, kr_ref):
    lat = lat_ref[...]
    nq_ref[...] = _rms(lat[:, :MLA_Q_RANK], qg_ref[...]).astype(BF16)
    nkv_ref[...] = _rms(lat[:, MLA_Q_RANK:MLA_Q_RANK + MLA_KV_RANK], kvg_ref[...]).astype(BF16)
    kr = lat[:, MLA_Q_RANK + MLA_KV_RANK:]
    kr_ref[...] = _rope(kr, cos_ref[...], sin_ref[...]).astype(BF16)


def _mla_prep1(lat, q_norm, kv_norm, cos_t, sin_t):
    n, wdt = lat.shape
    tm = ROW_TILE
    row = lambda w: pl.BlockSpec((tm, w), lambda i: (i, 0))
    one = lambda w: pl.BlockSpec((1, w), lambda i: (0, 0))
    return pl.pallas_call(
        _mla_prep1_kernel,
        grid=(n // tm,),
        in_specs=[row(wdt), one(MLA_Q_RANK), one(MLA_KV_RANK), row(128), row(128)],
        out_specs=[row(MLA_Q_RANK), row(MLA_KV_RANK), row(128)],
        out_shape=[jax.ShapeDtypeStruct((n, MLA_Q_RANK), BF16),
                   jax.ShapeDtypeStruct((n, MLA_KV_RANK), BF16),
                   jax.ShapeDtypeStruct((n, 128), BF16)],
        compiler_params=_cparams(("arbitrary",), 8 << 20),
        name="mla_prep1",
    )(lat, q_norm.reshape(1, -1), kv_norm.reshape(1, -1), cos_t, sin_t)


def _mla_prep2_kernel(q_ref, kn_ref, kr_ref, cos_ref, sin_ref, qo_ref, ko_ref, *, scale):
    cos_e = cos_ref[...]
    sin_s = sin_ref[...]
    kr = kr_ref[...]
    for h in range(MLA_HEADS):
        lo = h * MLA_QK_PAD
        qo_ref[:, lo:lo + MLA_NOPE] = (q_ref[:, lo:lo + MLA_NOPE].astype(F32) * scale).astype(BF16)
        qr = q_ref[:, lo + MLA_NOPE:lo + MLA_QK_PAD].astype(F32)
        qo_ref[:, lo + MLA_NOPE:lo + MLA_QK_PAD] = (_rope(qr, cos_e, sin_s) * scale).astype(BF16)
        ko_ref[:, lo:lo + MLA_NOPE] = kn_ref[:, h * MLA_NOPE:(h + 1) * MLA_NOPE]
        ko_ref[:, lo + MLA_NOPE:lo + MLA_QK_PAD] = kr


def _mla_prep2(q_raw, kv_all, kr, cos_t, sin_t):
    n = q_raw.shape[0]
    tm = ROW_TILE
    wq = MLA_HEADS * MLA_QK_PAD
    row = lambda w: pl.BlockSpec((tm, w), lambda i: (i, 0))
    return pl.pallas_call(
        functools.partial(_mla_prep2_kernel, scale=(MLA_NOPE + MLA_ROPE) ** -0.5),
        grid=(n // tm,),
        in_specs=[row(wq), row(MLA_HEADS * MLA_NOPE), row(128), row(128), row(128)],
        out_specs=[row(wq), row(wq)],
        out_shape=[jax.ShapeDtypeStruct((n, wq), BF16), jax.ShapeDtypeStruct((n, wq), BF16)],
        compiler_params=_cparams(("arbitrary",), 16 << 20),
        name="mla_prep2",
    )(q_raw, kv_all, kr, cos_t, sin_t)


def _gqa_prep_kernel(qkv_ref, qg_ref, kg_ref, cos_ref, sin_ref, qo_ref, ko_ref, *, scale):
    cos_e = cos_ref[...]
    sin_s = sin_ref[...]
    for h in range(GQA_HEADS):
        xh = qkv_ref[:, h * HEAD_DIM:(h + 1) * HEAD_DIM].astype(F32)
        qo_ref[:, h * HEAD_DIM:(h + 1) * HEAD_DIM] = (
            _rope(_rms(xh, qg_ref[...]), cos_e, sin_s) * scale).astype(BF16)
    for h in range(GQA_KV):
        lo = (GQA_HEADS + h) * HEAD_DIM
        xh = qkv_ref[:, lo:lo + HEAD_DIM].astype(F32)
        ko_ref[:, h * HEAD_DIM:(h + 1) * HEAD_DIM] = _rope(_rms(xh, kg_ref[...]), cos_e, sin_s).astype(BF16)


def _gqa_prep(qkv, q_norm, k_norm, cos_t, sin_t):
    n = qkv.shape[0]
    tm = ROW_TILE
    wqk = (GQA_HEADS + GQA_KV) * HEAD_DIM
    row = lambda w: pl.BlockSpec((tm, w), lambda i: (i, 0))
    one = pl.BlockSpec((1, HEAD_DIM), lambda i: (0, 0))
    return pl.pallas_call(
        functools.partial(_gqa_prep_kernel, scale=HEAD_DIM ** -0.5),
        grid=(n // tm,),
        in_specs=[row(wqk), one, one, row(HEAD_DIM), row(HEAD_DIM)],
        out_specs=[row(GQA_HEADS * HEAD_DIM), row(GQA_KV * HEAD_DIM)],
        out_shape=[jax.ShapeDtypeStruct((n, GQA_HEADS * HEAD_DIM), BF16),
                   jax.ShapeDtypeStruct((n, GQA_KV * HEAD_DIM), BF16)],
        compiler_params=_cparams(("arbitrary",), 16 << 20),
        name="gqa_prep",
    )(qkv, q_norm.reshape(1, -1), k_norm.reshape(1, -1), cos_t, sin_t)


def _flash_kernel(q_ref, k_ref, v_ref, o_ref, *, group, dk, dv, tk, n_lat, q_scale):
    tq = q_ref.shape[0]
    if group == 1:
        q = q_ref[...]
    else:
        q = jnp.concatenate([q_ref[:, g * dk:(g + 1) * dk] for g in range(group)], axis=0)
    if q_scale is not None:
        q = (q.astype(F32) * q_scale).astype(BF16)

    s = _dot_nt(q, k_ref[SEQ:SEQ + CTX_LEN, :])
    m = jnp.max(s, axis=-1, keepdims=True)
    p = jnp.exp(s - m)
    l = jnp.sum(p, axis=-1, keepdims=True)
    acc = _dot(p.astype(BF16), v_ref[SEQ:SEQ + CTX_LEN, :])

    def body(j, carry):
        m, l, acc = carry
        off = pl.multiple_of(j * tk, tk)
        s = _dot_nt(q, k_ref[pl.ds(off, tk), :])
        m_new = jnp.maximum(m, jnp.max(s, axis=-1, keepdims=True))
        alpha = jnp.exp(m - m_new)
        p = jnp.exp(s - m_new)
        l = alpha * l + jnp.sum(p, axis=-1, keepdims=True)
        acc = alpha * acc + _dot(p.astype(BF16), v_ref[pl.ds(off, tk), :])
        return m_new, l, acc

    if n_lat:
        m, l, acc = lax.fori_loop(0, n_lat, body, (m, l, acc))
    o = acc / l
    for g in range(group):
        o_ref[:, g * dv:(g + 1) * dv] = o[g * tq:(g + 1) * tq].astype(o_ref.dtype)


def _flash(q_arr, k_arr, v_arr, *, n_kv_heads, group, dk, dv, q_col0, k_col0, v_col0, ctx_queries, tq, tk,
           q_scale=None):
    n_tok = k_arr.shape[0]
    if ctx_queries:
        n_q, q_blk0, n_lat = CTX_LEN, SEQ // tq, 0
    else:
        n_q, q_blk0, n_lat = SEQ, 0, SEQ // tk
    m = group * tq
    vmem = (2 * n_tok * (dk + dv) * 2 + 4 * tq * group * (dk + dv) * 2
            + 3 * m * max(tk, CTX_LEN) * 4 + 4 * m * dv * 4)
    return pl.pallas_call(
        functools.partial(_flash_kernel, group=group, dk=dk, dv=dv, tk=tk, n_lat=n_lat, q_scale=q_scale),
        grid=(n_kv_heads, n_q // tq),
        in_specs=[pl.BlockSpec((tq, group * dk), lambda h, i: (q_blk0 + i, q_col0 + h)),
                  pl.BlockSpec((n_tok, dk), lambda h, i: (0, k_col0 + h)),
                  pl.BlockSpec((n_tok, dv), lambda h, i: (0, v_col0 + h))],
        out_specs=pl.BlockSpec((tq, group * dv), lambda h, i: (i, h)),
        out_shape=jax.ShapeDtypeStruct((n_q, n_kv_heads * group * dv), BF16),
        compiler_params=_cparams(("arbitrary", "arbitrary"), vmem + (8 << 20)),
        name="flash_ctx" if ctx_queries else "flash",
    )(q_arr, k_arr, v_arr)


def _na_start_row(b):
    return jnp.clip(NA_RB * b - NA_KH // 2, 0, GRID_H - NA_U)


def _na_kernel(q_ref, k_ref, v_ref, bias_ref, o_ref, *, scale):
    b = pl.program_id(1)
    start = pl.multiple_of(_na_start_row(b) * GRID_W, GRID_W)
    nwin = NA_U * GRID_W
    q = (q_ref[...].astype(F32) * scale).astype(BF16)
    s_loc = _dot_nt(q, k_ref[pl.ds(start, nwin), :]) + bias_ref[...]
    s_ctx = _dot_nt(q, k_ref[SEQ:SEQ + CTX_LEN, :])
    m = jnp.maximum(jnp.max(s_loc, axis=-1, keepdims=True), jnp.max(s_ctx, axis=-1, keepdims=True))
    p_loc = jnp.exp(s_loc - m)
    p_ctx = jnp.exp(s_ctx - m)
    l = jnp.sum(p_loc, axis=-1, keepdims=True) + jnp.sum(p_ctx, axis=-1, keepdims=True)
    o = _dot(p_loc.astype(BF16), v_ref[pl.ds(start, nwin), :]) + _dot(p_ctx.astype(BF16), v_ref[SEQ:SEQ + CTX_LEN, :])
    o_ref[...] = (o / l).astype(o_ref.dtype)


def _na_bias_tables(rpb):
    h = rpb.shape[0]
    w = np.arange(GRID_W)
    c0 = np.clip(w - NA_KW // 2, 0, GRID_W - NA_KW)
    colmask = (w[None, :] >= c0[:, None]) & (w[None, :] < c0[:, None] + NA_KW)
    colidx = np.clip(w[None, :] - w[:, None] + NA_KW - 1, 0, 2 * NA_KW - 2)
    slabs = jnp.where(colmask[None, None], rpb[:, :, colidx], NEG)
    slabs = jnp.concatenate([slabs, jnp.full((h, 1, GRID_W, GRID_W), NEG, rpb.dtype)], axis=1)
    masked = 2 * NA_KH - 1
    nb = GRID_H // NA_RB
    tabs = []
    for b in (0, 1, nb - 1):
        start = int(np.clip(NA_RB * b - NA_KH // 2, 0, GRID_H - NA_U))
        a = np.full((NA_RB, NA_U), masked, np.int32)
        for j in range(NA_RB):
            r = NA_RB * b + j
            r0 = int(np.clip(r - NA_KH // 2, 0, GRID_H - NA_KH))
            for u in range(NA_U):
                kr = start + u
                if r0 <= kr < r0 + NA_KH:
                    a[j, u] = kr - r + NA_KH - 1
        t = slabs[:, a]
        tabs.append(t.transpose(0, 1, 3, 2, 4).reshape(h, NA_RB * GRID_W, NA_U * GRID_W))
    return jnp.stack(tabs, axis=1)


def _na_attention(qkv, bias_tabs):
    n_tok = qkv.shape[0]
    nb = GRID_H // NA_RB
    tq = NA_RB * GRID_W
    nwin = NA_U * GRID_W

    def cls(b):
        return jnp.where(b == 0, 0, jnp.where(b == nb - 1, 2, 1))

    vmem = 4 * n_tok * HEAD_DIM * 2 + 2 * tq * nwin * 4 + 4 * tq * (nwin + CTX_LEN) * 4
    return pl.pallas_call(
        functools.partial(_na_kernel, scale=HEAD_DIM ** -0.5),
        grid=(NA_HEADS, nb),
        in_specs=[pl.BlockSpec((tq, HEAD_DIM), lambda h, b: (b, h)),
                  pl.BlockSpec((n_tok, HEAD_DIM), lambda h, b: (0, NA_HEADS + h)),
                  pl.BlockSpec((n_tok, HEAD_DIM), lambda h, b: (0, 2 * NA_HEADS + h)),
                  pl.BlockSpec((None, None, tq, nwin), lambda h, b: (h, cls(b), 0, 0))],
        out_specs=pl.BlockSpec((tq, HEAD_DIM), lambda h, b: (b, h)),
        out_shape=jax.ShapeDtypeStruct((SEQ, NA_HEADS * HEAD_DIM), BF16),
        compiler_params=_cparams(("arbitrary", "arbitrary"), vmem + (8 << 20)),
        name="na_attention",
    )(qkv, qkv, qkv, bias_tabs)


def _dispatch_kernel(slot_ref, h_hbm, xg_in, xg_out, sem, *, tb):
    del xg_in
    i = pl.program_id(0)

    def body(t, carry):
        for k in range(MOE_TOP_K):
            pltpu.make_async_copy(h_hbm.at[pl.ds(i * tb + t, 1)],
                                  xg_out.at[pl.ds(slot_ref[k, t], 1)], sem).start()
        return carry

    lax.fori_loop(0, tb, body, 0)
    for k in range(MOE_TOP_K):
        pltpu.make_async_copy(h_hbm.at[pl.ds(0, tb)], xg_out.at[pl.ds(0, tb)], sem).wait()


def _dispatch(h32, slot_t, n_rows, n_slots):
    d = h32.shape[1]
    tb = ROW_TILE
    xg0 = jnp.zeros((n_slots, d), F32)
    return pl.pallas_call(
        functools.partial(_dispatch_kernel, tb=tb),
        grid=(n_rows // tb,),
        in_specs=[pl.BlockSpec((MOE_TOP_K, tb), lambda i: (0, i), memory_space=pltpu.SMEM),
                  pl.BlockSpec(memory_space=pl.ANY),
                  pl.BlockSpec(memory_space=pl.ANY)],
        out_specs=pl.BlockSpec(memory_space=pl.ANY),
        out_shape=jax.ShapeDtypeStruct((n_slots, d), F32),
        scratch_shapes=[pltpu.SemaphoreType.DMA(())],
        input_output_aliases={2: 0},
        compiler_params=_cparams(("arbitrary",), 16 << 20),
        name="moe_dispatch",
    )(slot_t, h32, xg0)


def _ffn_kernel(be_ref, nu_ref, x_ref, wg_ref, wu_ref, wd_ref, o_ref, wgb, wub, wdb):
    b = pl.program_id(0)
    prev = be_ref[jnp.maximum(b - 1, 0)]
    new_expert = jnp.logical_or(b == 0, be_ref[b] != prev)

    @pl.when(jnp.logical_and(new_expert, b < nu_ref[0]))
    def _():
        wgb[...] = wg_ref[...].astype(BF16)
        wub[...] = wu_ref[...].astype(BF16)
        wdb[...] = wd_ref[...].astype(BF16)

    @pl.when(b < nu_ref[0])
    def _():
        x = x_ref[...].astype(BF16)
        g = _dot(x, wgb[...])
        u = _dot(x, wub[...])
        a = (g * jax.nn.sigmoid(g) * u).astype(BF16)
        o_ref[...] = _dot(a, wdb[...]).astype(o_ref.dtype)

    @pl.when(b >= nu_ref[0])
    def _():
        o_ref[...] = jnp.zeros_like(o_ref)


def _expert_ffn(x_rows, block_expert, n_used, wg, wu, wd, tb, out_dtype):
    p, d = x_rows.shape
    de = wg.shape[-1]
    nb = p // tb
    xsz = x_rows.dtype.itemsize
    osz = jnp.dtype(out_dtype).itemsize
    vmem = 2 * 3 * d * de * 4 + 3 * d * de * 2 + 2 * tb * d * (xsz + osz) + 4 * tb * de * 4 + tb * d * 4

    def blk(b, be, nu):
        return jnp.minimum(b, nu[0] - 1)

    grid_spec = pltpu.PrefetchScalarGridSpec(
        num_scalar_prefetch=2,
        grid=(nb,),
        in_specs=[pl.BlockSpec((tb, d), lambda b, be, nu: (blk(b, be, nu), 0)),
                  pl.BlockSpec((None, d, de), lambda b, be, nu: (be[blk(b, be, nu)], 0, 0)),
                  pl.BlockSpec((None, d, de), lambda b, be, nu: (be[blk(b, be, nu)], 0, 0)),
                  pl.BlockSpec((None, de, d), lambda b, be, nu: (be[blk(b, be, nu)], 0, 0))],
        out_specs=pl.BlockSpec((tb, d), lambda b, be, nu: (b, 0)),
        scratch_shapes=[pltpu.VMEM((d, de), BF16), pltpu.VMEM((d, de), BF16), pltpu.VMEM((de, d), BF16)],
    )
    return pl.pallas_call(
        _ffn_kernel,
        grid_spec=grid_spec,
        out_shape=jax.ShapeDtypeStruct((p, d), out_dtype),
        compiler_params=_cparams(("arbitrary",), vmem + (4 << 20)),
        name="expert_ffn",
    )(block_expert, n_used, x_rows, wg, wu, wd)


def _combine_kernel(slot_cur, slot_nxt, y_hbm, ew_ref, ysh_ref, x_ref, gate_ref, o_ref, buf, sem, *, tc, nblk):
    i = pl.program_id(0)

    def gather(slot_ref, bslot):
        def body(t, carry):
            for k in range(MOE_TOP_K):
                pltpu.make_async_copy(y_hbm.at[pl.ds(slot_ref[k, t], 1)],
                                      buf.at[bslot, k, pl.ds(t, 1)], sem.at[bslot]).start()
            return carry

        lax.fori_loop(0, tc, body, 0)

    @pl.when(i == 0)
    def _():
        gather(slot_cur, 0)

    cur = i % 2

    @pl.when(i + 1 < nblk)
    def _():
        @pl.when(cur == 0)
        def _():
            gather(slot_nxt, 1)

        @pl.when(cur == 1)
        def _():
            gather(slot_nxt, 0)

    def finish(bslot):
        for k in range(MOE_TOP_K):
            pltpu.make_async_copy(y_hbm.at[pl.ds(0, tc)], buf.at[bslot, k], sem.at[bslot]).wait()
        ew = ew_ref[...]
        acc = ysh_ref[...].astype(F32)
        for k in range(MOE_TOP_K):
            acc = acc + ew[:, k:k + 1] * buf[bslot, k]
        o_ref[...] = x_ref[...] + gate_ref[...] * acc

    @pl.when(cur == 0)
    def _():
        finish(0)

    @pl.when(cur == 1)
    def _():
        finish(1)


def _combine(y, slot_t, ew, ysh, x, gate, n_rows):
    d = y.shape[1]
    tc = COMBINE_TILE
    nblk = n_rows // tc
    lat_blocks = SEQ // tc
    row = pl.BlockSpec((tc, d), lambda i: (i, 0))
    vmem = 2 * MOE_TOP_K * tc * d * 4 + 8 * tc * d * 4
    return pl.pallas_call(
        functools.partial(_combine_kernel, tc=tc, nblk=nblk),
        grid=(nblk,),
        in_specs=[pl.BlockSpec((MOE_TOP_K, tc), lambda i: (0, i), memory_space=pltpu.SMEM),
                  pl.BlockSpec((MOE_TOP_K, tc), lambda i: (0, jnp.minimum(i + 1, nblk - 1)),
                               memory_space=pltpu.SMEM),
                  pl.BlockSpec(memory_space=pl.ANY),
                  pl.BlockSpec((tc, MOE_TOP_K), lambda i: (i, 0)),
                  row, row,
                  pl.BlockSpec((None, 1, d), lambda i: (jnp.where(i >= lat_blocks, 1, 0), 0, 0))],
        out_specs=row,
        out_shape=jax.ShapeDtypeStruct((n_rows, d), F32),
        scratch_shapes=[pltpu.VMEM((2, MOE_TOP_K, tc, d), F32), pltpu.SemaphoreType.DMA((2,))],
        compiler_params=_cparams(("arbitrary",), vmem + (4 << 20)),
        name="moe_combine",
    )(slot_t, slot_t, y, ew, ysh, x, gate.reshape(2, 1, d))


def _routing_tables(eidx_t, n_rows):
    tb = EXPERT_TILE
    n_blocks = -(-(n_rows * MOE_TOP_K) // tb) + N_EXPERTS
    e_iota = jnp.arange(N_EXPERTS, dtype=jnp.int32)[:, None]
    onehot = jnp.zeros((N_EXPERTS, n_rows), jnp.int32)
    for k in range(MOE_TOP_K):
        onehot = onehot + (eidx_t[k][None, :] == e_iota).astype(jnp.int32)
    counts = jnp.sum(onehot, axis=1)
    rank = jnp.cumsum(onehot, axis=1) - onehot
    padded = (counts + tb - 1) // tb * tb
    pad_end = jnp.cumsum(padded)
    pad_start = pad_end - padded
    base = pad_start[:, None] + rank
    slot = jnp.stack([jnp.sum(jnp.where(eidx_t[k][None, :] == e_iota, base, 0), axis=0)
                      for k in range(MOE_TOP_K)], axis=0).astype(jnp.int32)
    blk_start = jnp.arange(n_blocks, dtype=jnp.int32) * tb
    block_expert = jnp.minimum(jnp.searchsorted(pad_end, blk_start, side='right'), N_EXPERTS - 1).astype(jnp.int32)
    n_used = (pad_end[-1] // tb).astype(jnp.int32).reshape(1)
    return slot, block_expert, n_used, n_blocks * tb


def _moe(x_new, h, h32, eidx_t, ew_t, gate, n_rows, wg, wu, wd, sg, su, sd):
    slot, block_expert, n_used, n_slots = _routing_tables(eidx_t, n_rows)
    xg = _dispatch(h32, slot, n_rows, n_slots)
    y = _expert_ffn(xg, block_expert, n_used, wg, wu, wd, EXPERT_TILE, F32)
    one = jnp.ones((1,), jnp.int32)
    tm = MM_TM if n_rows % MM_TM == 0 else 512
    ysh = _expert_ffn(h, jnp.zeros((n_rows // tm,), jnp.int32), one * (n_rows // tm),
                      sg[None], su[None], sd[None], tm, BF16)
    return _combine(y, slot, ew_t.T, ysh, x_new, gate, n_rows)


def _rope_tables(d, width):
    t = jnp.arange(SEQ)
    row = (t // GRID_W).astype(F32)
    col = (t % GRID_W).astype(F32)
    n = d // 4
    inv = ROPE_THETA ** (-jnp.arange(n, dtype=F32) / n)
    ang = jnp.concatenate([row[:, None] * inv, col[:, None] * inv], axis=-1)
    ang = jnp.concatenate([ang, jnp.zeros((CTX_LEN, d // 2), F32)], axis=0)
    cos = jnp.repeat(jnp.cos(ang), 2, axis=1)
    sin = jnp.repeat(jnp.sin(ang), 2, axis=1) * jnp.tile(jnp.array([-1.0, 1.0], F32), d // 2)
    if width > d:
        cos = jnp.concatenate([cos, jnp.ones((N_TOK, width - d), F32)], axis=1)
        sin = jnp.concatenate([sin, jnp.zeros((N_TOK, width - d), F32)], axis=1)
    return cos, sin


def _mixer_even(h, w_in, rpb, q_norm, w_uq, kv_norm, w_ukv, w_out):
    n_na = 3 * NA_HEADS * HEAD_DIM
    qkv = _matmul(h, w_in[:, :n_na], N_TOK, BF16, MM_TM, 1024)
    w_tail = jnp.pad(w_in[:, n_na:], ((0, 0), (0, 64)))
    lat = _matmul(h, w_tail, N_TOK, F32, MM_TM, 384)
    cos_t, sin_t = _rope_tables(MLA_ROPE, 128)
    nq, nkv, kr = _mla_prep1(lat, q_norm, kv_norm, cos_t, sin_t)
    wq = w_uq.reshape(MLA_Q_RANK, MLA_HEADS, MLA_NOPE + MLA_ROPE)
    wq = jnp.pad(wq, ((0, 0), (0, 0), (0, MLA_QK_PAD - MLA_NOPE - MLA_ROPE))).reshape(MLA_Q_RANK, -1)
    wkv = w_ukv.reshape(MLA_KV_RANK, MLA_HEADS, MLA_NOPE + MLA_V)
    wkv = jnp.concatenate([wkv[:, :, :MLA_NOPE].reshape(MLA_KV_RANK, -1),
                           wkv[:, :, MLA_NOPE:].reshape(MLA_KV_RANK, -1)], axis=1)
    q_raw = _matmul(nq, wq, N_TOK, F32, MM_TM, 1024)
    kv_all = _matmul(nkv, wkv, N_TOK, BF16, MM_TM, 1024)
    q_mla, k_cat = _mla_prep2(q_raw, kv_all, kr, cos_t, sin_t)

    o_na = _na_attention(qkv, _na_bias_tables(rpb))
    mla = dict(n_kv_heads=MLA_HEADS, group=1, dk=MLA_QK_PAD, dv=MLA_V, q_col0=0, k_col0=0, v_col0=MLA_HEADS)
    o_mla = _flash(q_mla, k_cat, kv_all, ctx_queries=False, tq=512, tk=512, **mla)
    o_mla_c = _flash(q_mla, k_cat, kv_all, ctx_queries=True, tq=256, tk=512, **mla)
    na = dict(n_kv_heads=NA_HEADS, group=1, dk=HEAD_DIM, dv=HEAD_DIM, q_col0=0, k_col0=NA_HEADS, v_col0=2 * NA_HEADS)
    o_na_c = _flash(qkv, qkv, qkv, ctx_queries=True, tq=256, tk=512, q_scale=HEAD_DIM ** -0.5, **na)
    o = jnp.concatenate([jnp.concatenate([o_na, o_mla], axis=1),
                         jnp.concatenate([o_na_c, o_mla_c], axis=1)], axis=0)
    return _matmul(o, w_out, N_TOK, F32, MM_TM, 1024)


def _mixer_odd_last(h, w_qkv, q_norm, k_norm, w_o):
    qkv = _matmul(h, w_qkv, N_TOK, BF16, MM_TM, 1024)
    cos_t, sin_t = _rope_tables(HEAD_DIM, HEAD_DIM)
    q_g, k_g = _gqa_prep(qkv, q_norm, k_norm, cos_t, sin_t)
    o = _flash(q_g, k_g, qkv, n_kv_heads=GQA_KV, group=GQA_GROUP, dk=HEAD_DIM, dv=HEAD_DIM,
               q_col0=0, k_col0=0, v_col0=GQA_HEADS + GQA_KV, ctx_queries=False, tq=256, tk=512)
    return _matmul(o, w_o, SEQ, F32, 512, 1024)


def kernel(x, c, ctx, c_ctx, ada_w, ada_b, mix_norm, ffn_norm, ev_w_in, ev_rpb, ev_q_norm, ev_w_uq, ev_kv_norm, ev_w_ukv, ev_w_out, od_w_qkv, od_q_norm, od_k_norm, od_w_o, router_w, router_b, exp_w_gate, exp_w_up, exp_w_down, sh_w_gate, sh_w_up, sh_w_down, final_norm):
    assert DEPTH == 2 and x.shape == (1, SEQ, D_MODEL)
    mods = _adaln_mods(c, c_ctx, ada_w, ada_b)
    xt = jnp.concatenate([x[0], ctx[0]], axis=0)

    def parts(i):
        return [mods[i, :, k * D_MODEL:(k + 1) * D_MODEL] for k in range(6)]

    sh1, sc1, g1, sh2, sc2, g2 = parts(0)
    (h,) = _norm_call(xt, N_TOK, mix_norm[0], mod=(sh1, sc1))
    o = _mixer_even(h, ev_w_in[0], ev_rpb[0], ev_q_norm[0], ev_w_uq[0], ev_kv_norm[0], ev_w_ukv[0], ev_w_out[0])
    xt, h, h32, eidx_t, ew_t = _norm_call(xt, N_TOK, ffn_norm[0], resid=(o, g1), mod=(sh2, sc2),
                                          router=(router_w[0], router_b[0]))
    xt = _moe(xt, h, h32, eidx_t, ew_t, g2, N_TOK, exp_w_gate[0], exp_w_up[0], exp_w_down[0],
              sh_w_gate[0], sh_w_up[0], sh_w_down[0])

    sh1, sc1, g1, sh2, sc2, g2 = parts(1)
    (h,) = _norm_call(xt, N_TOK, mix_norm[1], mod=(sh1, sc1))
    o = _mixer_odd_last(h, od_w_qkv[0], od_q_norm[0], od_k_norm[0], od_w_o[0])
    xl, h, h32, eidx_t, ew_t = _norm_call(xt, SEQ, ffn_norm[1], resid=(o, g1), mod=(sh2, sc2),
                                          router=(router_w[1], router_b[1]))
    xl = _moe(xl, h, h32, eidx_t, ew_t, g2, SEQ, exp_w_gate[1], exp_w_up[1], exp_w_down[1],
              sh_w_gate[1], sh_w_up[1], sh_w_down[1])
    (out,) = _norm_call(xl, SEQ, final_norm, out_dtype=F32)
    return out[None]
```

```python
import functools

import numpy as np
import jax
import jax.numpy as jnp
from jax import lax
from jax.experimental import pallas as pl
from jax.experimental.pallas import tpu as pltpu

F32 = jnp.float32
BF16 = jnp.bfloat16

D_MODEL = 2048
SEQ = 8192
DEPTH = 2
GRID_W = 64
GRID_H = SEQ // GRID_W
CTX_LEN = 256
N_TOK = SEQ + CTX_LEN
EPS = 1e-6
ROPE_THETA = 10000.0
HEAD_DIM = 128
NA_HEADS = 8
NA_KH = 8
NA_KW = 16
MLA_HEADS = 8
MLA_Q_RANK = 512
MLA_KV_RANK = 512
MLA_NOPE = 128
MLA_ROPE = 64
MLA_V = 128
MLA_QK_PAD = 256
GQA_HEADS = 16
GQA_KV = 4
GQA_GROUP = GQA_HEADS // GQA_KV
N_EXPERTS = 64
N_GROUPS = 8
GROUP_SIZE = N_EXPERTS // N_GROUPS
TOPK_GROUPS = 4
MOE_TOP_K = 8
D_EXPERT = 512
ROUTED_SCALE = 2.5

NEG = -1e30

VMEM_LIMIT_CAP = 56 * 1024 * 1024

ROW_TILE = 256
MM_TM = 768
NA_RB = 8
NA_U = NA_RB + NA_KH - 1
EXPERT_TILE = 256
COMBINE_TILE = 128


def _cparams(sem, vmem_bytes):
    return pltpu.CompilerParams(
        dimension_semantics=sem,
        vmem_limit_bytes=int(min(max(vmem_bytes, 16 * 1024 * 1024), VMEM_LIMIT_CAP)),
    )


def _dot(a, b):
    return jnp.dot(a, b, preferred_element_type=F32)


def _dot_nt(a, b):
    return lax.dot_general(a, b, (((1,), (1,)), ((), ())), preferred_element_type=F32)


def _mods_kernel(cc_ref, w_ref, b_ref, o_ref, acc_ref, *, nk):
    k = pl.program_id(2)

    @pl.when(k == 0)
    def _():
        acc_ref[...] = jnp.zeros_like(acc_ref)

    a = cc_ref[...]
    a = a * jax.nn.sigmoid(a)
    w = w_ref[...]
    tk, tn = w.shape
    w3 = w.reshape(tk // 8, 8, tn)
    for v in range(2):
        col = a[:, v:v + 1].reshape(tk // 8, 8, 1)
        acc_ref[v] += jnp.sum(w3 * col, axis=0)

    @pl.when(k == nk - 1)
    def _():
        o_ref[...] = jnp.sum(acc_ref[...], axis=1) + b_ref[...]


def _adaln_mods(c, c_ctx, ada_w, ada_b):
    depth, d, n = ada_w.shape
    tk, tn = 512, 1536
    nk = d // tk
    cc = jnp.stack([c[0], c_ctx], axis=1)
    return pl.pallas_call(
        functools.partial(_mods_kernel, nk=nk),
        grid=(depth, n // tn, nk),
        in_specs=[
            pl.BlockSpec((tk, 2), lambda i, j, k: (k, 0)),
            pl.BlockSpec((None, tk, tn), lambda i, j, k: (i, k, j)),
            pl.BlockSpec((None, 1, tn), lambda i, j, k: (i, 0, j)),
        ],
        out_specs=pl.BlockSpec((None, 2, tn), lambda i, j, k: (i, 0, j)),
        out_shape=jax.ShapeDtypeStruct((depth, 2, n), F32),
        scratch_shapes=[pltpu.VMEM((2, 8, tn), F32)],
        compiler_params=_cparams(("arbitrary", "arbitrary", "arbitrary"), 4 * tk * tn * 4),
        name="adaln_mods",
    )(cc, ada_w, ada_b.reshape(depth, 1, n))


def _router_select(hf, rwt_ref, rb_ref, base_ref):
    logits = lax.dot_general(rwt_ref[...], hf, (((1,), (1,)), ((), ())),
                             precision=lax.Precision.HIGHEST, preferred_element_type=F32)
    scores = jax.nn.sigmoid(logits)
    sel = scores + rb_ref[...]
    tm = sel.shape[1]
    ninf = jnp.float32(-jnp.inf)
    sub = lax.broadcasted_iota(jnp.int32, (GROUP_SIZE, tm), 0)
    gi = lax.broadcasted_iota(jnp.int32, (N_GROUPS, tm), 0)
    gs = jnp.zeros((N_GROUPS, tm), F32)
    for g in range(N_GROUPS):
        blk = sel[g * GROUP_SIZE:(g + 1) * GROUP_SIZE, :]
        m1 = jnp.max(blk, axis=0, keepdims=True)
        first = jnp.min(jnp.where(blk == m1, sub, GROUP_SIZE), axis=0, keepdims=True)
        m2 = jnp.max(jnp.where(sub == first, ninf, blk), axis=0, keepdims=True)
        gs = jnp.where(gi == g, m1 + m2, gs)
    gsel = jnp.zeros((N_GROUPS, tm), F32)
    cur = gs
    for _ in range(TOPK_GROUPS):
        mx = jnp.max(cur, axis=0, keepdims=True)
        idx = jnp.min(jnp.where(cur == mx, gi, N_GROUPS), axis=0, keepdims=True)
        pick = gi == idx
        gsel = jnp.where(pick, 1.0, gsel)
        cur = jnp.where(pick, ninf, cur)
    ei = lax.broadcasted_iota(jnp.int32, (N_EXPERTS, tm), 0)
    eg = ei // GROUP_SIZE
    emask = jnp.zeros((N_EXPERTS, tm), F32)
    for g in range(N_GROUPS):
        emask = jnp.where(eg == g, gsel[g:g + 1, :], emask)
    cur = jnp.where(emask > 0.0, sel, ninf)
    idxs, ws, picks = [], [], []
    for _ in range(MOE_TOP_K):
        mx = jnp.max(cur, axis=0, keepdims=True)
        idx = jnp.min(jnp.where(cur == mx, ei, N_EXPERTS), axis=0, keepdims=True)
        pick = ei == idx
        ws.append(jnp.sum(jnp.where(pick, scores, 0.0), axis=0, keepdims=True))
        idxs.append(idx)
        picks.append(pick)
        cur = jnp.where(pick, ninf, cur)
    total = ws[0]
    for w in ws[1:]:
        total = total + w
    ws = [w / total * ROUTED_SCALE for w in ws]
    onehot = jnp.zeros((N_EXPERTS, tm), F32)
    for pick in picks:
        onehot = jnp.where(pick, 1.0, onehot)
    tri = (lax.broadcasted_iota(jnp.int32, (tm, tm), 0) < lax.broadcasted_iota(jnp.int32, (tm, tm), 1))
    before = _dot(onehot.astype(BF16), jnp.where(tri, 1.0, 0.0).astype(BF16)) + base_ref[...]
    ranks = [jnp.sum(jnp.where(pick, before, 0.0), axis=0, keepdims=True).astype(jnp.int32) for pick in picks]
    base_ref[...] += jnp.sum(onehot, axis=1, keepdims=True)
    return idxs, ws, ranks


def _norm_kernel(*refs, has_resid, modulated, with_router, out_dtype):
    it = iter(refs)
    x_ref = next(it)
    if has_resid:
        o_ref = next(it)
        gate_ref = next(it)
    g_ref = next(it)
    if modulated:
        shift_ref = next(it)
        scale_ref = next(it)
    if with_router:
        rwt_ref = next(it)
        rb_ref = next(it)
    if has_resid:
        xn_ref = next(it)
    h_ref = next(it)
    if with_router:
        h32_ref = next(it)
        eidx_ref = next(it)
        ew_ref = next(it)
        rank_ref = next(it)
        cnt_ref = next(it)
        base_ref = next(it)

        @pl.when(pl.program_id(0) == 0)
        def _():
            base_ref[...] = jnp.zeros_like(base_ref)

    x = x_ref[...]
    if has_resid:
        x = x + gate_ref[...] * o_ref[...].astype(F32)
        xn_ref[...] = x
    y = x * lax.rsqrt(jnp.mean(x * x, axis=-1, keepdims=True) + EPS)
    y = y * g_ref[...]
    if modulated:
        y = y * (1.0 + scale_ref[...]) + shift_ref[...]
    h_ref[...] = y.astype(out_dtype)
    if with_router:
        h32_ref[...] = y
        idxs, ws, ranks = _router_select(y, rwt_ref, rb_ref, base_ref)
        for k in range(MOE_TOP_K):
            eidx_ref[k:k + 1, :] = idxs[k]
            ew_ref[k:k + 1, :] = ws[k]
            rank_ref[k:k + 1, :] = ranks[k]
        cnt_ref[...] = base_ref[...].astype(jnp.int32)


def _norm_call(x, n_rows, gain, *, resid=None, mod=None, router=None, out_dtype=BF16):
    d = x.shape[1]
    tm = ROW_TILE
    nblk = n_rows // tm
    lat_blocks = SEQ // tm

    def cls_map(i):
        return (jnp.where(i >= lat_blocks, 1, 0), 0, 0)

    row = pl.BlockSpec((tm, d), lambda i: (i, 0))
    vec = pl.BlockSpec((None, 1, d), cls_map)
    args, in_specs = [x], [row]
    if resid is not None:
        o, gate = resid
        args += [o, gate.reshape(2, 1, d)]
        in_specs += [row, vec]
    args.append(gain.reshape(1, d))
    in_specs.append(pl.BlockSpec((1, d), lambda i: (0, 0)))
    if mod is not None:
        shift, scale = mod
        args += [shift.reshape(2, 1, d), scale.reshape(2, 1, d)]
        in_specs += [vec, vec]
    if router is not None:
        rw, rb = router
        args += [rw.T, rb.reshape(N_EXPERTS, 1)]
        in_specs += [pl.BlockSpec((N_EXPERTS, d), lambda i: (0, 0)),
                     pl.BlockSpec((N_EXPERTS, 1), lambda i: (0, 0))]
    out_shape, out_specs = [], []
    if resid is not None:
        out_shape.append(jax.ShapeDtypeStruct((n_rows, d), F32))
        out_specs.append(row)
    out_shape.append(jax.ShapeDtypeStruct((n_rows, d), out_dtype))
    out_specs.append(row)
    if router is not None:
        out_shape += [jax.ShapeDtypeStruct((n_rows, d), F32),
                      jax.ShapeDtypeStruct((MOE_TOP_K, n_rows), jnp.int32),
                      jax.ShapeDtypeStruct((MOE_TOP_K, n_rows), F32),
                      jax.ShapeDtypeStruct((MOE_TOP_K, n_rows), jnp.int32),
                      jax.ShapeDtypeStruct((N_EXPERTS, 1), jnp.int32)]
        topk = pl.BlockSpec((MOE_TOP_K, tm), lambda i: (0, i))
        out_specs += [row, topk, topk, topk, pl.BlockSpec((N_EXPERTS, 1), lambda i: (0, 0))]
    return pl.pallas_call(
        functools.partial(_norm_kernel, has_resid=resid is not None, modulated=mod is not None,
                          with_router=router is not None, out_dtype=out_dtype),
        grid=(nblk,),
        in_specs=in_specs,
        out_specs=out_specs,
        out_shape=out_shape,
        scratch_shapes=[pltpu.VMEM((N_EXPERTS, 1), F32)] if router is not None else [],
        compiler_params=_cparams(("arbitrary",), 16 * tm * d * 4),
        name="norm_router" if router is not None else "norm",
    )(*args)


def _mm_kernel(a_ref, w_ref, o_ref, wb_ref):
    @pl.when(pl.program_id(1) == 0)
    def _():
        wb_ref[...] = w_ref[...].astype(BF16)

    o_ref[...] = _dot(a_ref[...], wb_ref[...]).astype(o_ref.dtype)


def _matmul(a, w, m_rows, out_dtype, tm, tn):
    k, n = w.shape
    assert m_rows % tm == 0 and n % tn == 0 and a.shape[1] == k
    osz = jnp.dtype(out_dtype).itemsize
    vmem = 2 * k * tn * 4 + k * tn * 2 + 2 * tm * k * 2 + 2 * tm * tn * osz + tm * tn * 4
    return pl.pallas_call(
        _mm_kernel,
        grid=(n // tn, m_rows // tm),
        in_specs=[pl.BlockSpec((tm, k), lambda j, i: (i, 0)),
                  pl.BlockSpec((k, tn), lambda j, i: (0, j))],
        out_specs=pl.BlockSpec((tm, tn), lambda j, i: (i, j)),
        out_shape=jax.ShapeDtypeStruct((m_rows, n), out_dtype),
        scratch_shapes=[pltpu.VMEM((k, tn), BF16)],
        compiler_params=_cparams(("arbitrary", "arbitrary"), vmem + (4 << 20)),
        name="matmul",
    )(a, w)


def _rope(x, cos_e, sin_s):
    w = x.shape[-1]
    lane = lax.broadcasted_iota(jnp.int32, x.shape, 1)
    nxt = pltpu.roll(x, w - 1, 1)
    prv = pltpu.roll(x, 1, 1)
    swapped = jnp.where(lane % 2 == 0, nxt, prv)
    return x * cos_e + swapped * sin_s


def _rms(x, g):
    return x * lax.rsqrt(jnp.mean(x * x, axis=-1, keepdims=True) + EPS) * g


def _mla_prep1_kernel(lat_ref, qg_ref, kvg_ref, cos_ref, sin_ref, nq_ref, nkv_ref, kr_ref):
    lat = lat_ref[...]
    nq_ref[...] = _rms(lat[:, :MLA_Q_RANK], qg_ref[...]).astype(BF16)
    nkv_ref[...] = _rms(lat[:, MLA_Q_RANK:MLA_Q_RANK + MLA_KV_RANK], kvg_ref[...]).astype(BF16)
    kr = lat[:, MLA_Q_RANK + MLA_KV_RANK:]
    kr_ref[...] = _rope(kr, cos_ref[...], sin_ref[...]).astype(BF16)


def _mla_prep1(lat, q_norm, kv_norm, cos_t, sin_t):
    n, wdt = lat.shape
    tm = ROW_TILE
    row = lambda w: pl.BlockSpec((tm, w), lambda i: (i, 0))
    one = lambda w: pl.BlockSpec((1, w), lambda i: (0, 0))
    return pl.pallas_call(
        _mla_prep1_kernel,
        grid=(n // tm,),
        in_specs=[row(wdt), one(MLA_Q_RANK), one(MLA_KV_RANK), row(128), row(128)],
        out_specs=[row(MLA_Q_RANK), row(MLA_KV_RANK), row(128)],
        out_shape=[jax.ShapeDtypeStruct((n, MLA_Q_RANK), BF16),
                   jax.ShapeDtypeStruct((n, MLA_KV_RANK), BF16),
                   jax.ShapeDtypeStruct((n, 128), BF16)],
        compiler_params=_cparams(("arbitrary",), 8 << 20),
        name="mla_prep1",
    )(lat, q_norm.reshape(1, -1), kv_norm.reshape(1, -1), cos_t, sin_t)


def _mla_prep2_kernel(q_ref, kn_ref, kr_ref, cos_ref, sin_ref, qo_ref, ko_ref, *, scale):
    cos_e = cos_ref[...]
    sin_s = sin_ref[...]
    kr = kr_ref[...]
    for h in range(MLA_HEADS):
        lo = h * MLA_QK_PAD
        qo_ref[:, lo:lo + MLA_NOPE] = (q_ref[:, lo:lo + MLA_NOPE].astype(F32) * scale).astype(BF16)
        qr = q_ref[:, lo + MLA_NOPE:lo + MLA_QK_PAD].astype(F32)
        qo_ref[:, lo + MLA_NOPE:lo + MLA_QK_PAD] = (_rope(qr, cos_e, sin_s) * scale).astype(BF16)
        ko_ref[:, lo:lo + MLA_NOPE] = kn_ref[:, h * MLA_NOPE:(h + 1) * MLA_NOPE]
        ko_ref[:, lo + MLA_NOPE:lo + MLA_QK_PAD] = kr


def _mla_prep2(q_raw, kv_all, kr, cos_t, sin_t):
    n = q_raw.shape[0]
    tm = ROW_TILE
    wq = MLA_HEADS * MLA_QK_PAD
    row = lambda w: pl.BlockSpec((tm, w), lambda i: (i, 0))
    return pl.pallas_call(
        functools.partial(_mla_prep2_kernel, scale=(MLA_NOPE + MLA_ROPE) ** -0.5),
        grid=(n // tm,),
        in_specs=[row(wq), row(MLA_HEADS * MLA_NOPE), row(128), row(128), row(128)],
        out_specs=[row(wq), row(wq)],
        out_shape=[jax.ShapeDtypeStruct((n, wq), BF16), jax.ShapeDtypeStruct((n, wq), BF16)],
        compiler_params=_cparams(("arbitrary",), 16 << 20),
        name="mla_prep2",
    )(q_raw, kv_all, kr, cos_t, sin_t)


def _gqa_prep_kernel(qkv_ref, qg_ref, kg_ref, cos_ref, sin_ref, qo_ref, ko_ref, *, scale):
    cos_e = cos_ref[...]
    sin_s = sin_ref[...]
    for h in range(GQA_HEADS):
        xh = qkv_ref[:, h * HEAD_DIM:(h + 1) * HEAD_DIM].astype(F32)
        qo_ref[:, h * HEAD_DIM:(h + 1) * HEAD_DIM] = (
            _rope(_rms(xh, qg_ref[...]), cos_e, sin_s) * scale).astype(BF16)
    for h in range(GQA_KV):
        lo = (GQA_HEADS + h) * HEAD_DIM
        xh = qkv_ref[:, lo:lo + HEAD_DIM].astype(F32)
        ko_ref[:, h * HEAD_DIM:(h + 1) * HEAD_DIM] = _rope(_rms(xh, kg_ref[...]), cos_e, sin_s).astype(BF16)


def _gqa_prep(qkv, q_norm, k_norm, cos_t, sin_t):
    n = qkv.shape[0]
    tm = ROW_TILE
    wqk = (GQA_HEADS + GQA_KV) * HEAD_DIM
    row = lambda w: pl.BlockSpec((tm, w), lambda i: (i, 0))
    one = pl.BlockSpec((1, HEAD_DIM), lambda i: (0, 0))
    return pl.pallas_call(
        functools.partial(_gqa_prep_kernel, scale=HEAD_DIM ** -0.5),
        grid=(n // tm,),
        in_specs=[row(wqk), one, one, row(HEAD_DIM), row(HEAD_DIM)],
        out_specs=[row(GQA_HEADS * HEAD_DIM), row(GQA_KV * HEAD_DIM)],
        out_shape=[jax.ShapeDtypeStruct((n, GQA_HEADS * HEAD_DIM), BF16),
                   jax.ShapeDtypeStruct((n, GQA_KV * HEAD_DIM), BF16)],
        compiler_params=_cparams(("arbitrary",), 16 << 20),
        name="gqa_prep",
    )(qkv, q_norm.reshape(1, -1), k_norm.reshape(1, -1), cos_t, sin_t)


def _flash_kernel(q_ref, k_ref, v_ref, o_ref, *, group, dk, dv, tk, n_lat, q_scale):
    tq = q_ref.shape[0]
    if group == 1:
        q = q_ref[...]
    else:
        q = jnp.concatenate([q_ref[:, g * dk:(g + 1) * dk] for g in range(group)], axis=0)
    if q_scale is not None:
        q = (q.astype(F32) * q_scale).astype(BF16)

    s = _dot_nt(q, k_ref[SEQ:SEQ + CTX_LEN, :])
    m = jnp.max(s, axis=-1, keepdims=True)
    p = jnp.exp(s - m)
    l = jnp.sum(p, axis=-1, keepdims=True)
    acc = _dot(p.astype(BF16), v_ref[SEQ:SEQ + CTX_LEN, :])

    def body(j, carry):
        m, l, acc = carry
        off = pl.multiple_of(j * tk, tk)
        s = _dot_nt(q, k_ref[pl.ds(off, tk), :])
        m_new = jnp.maximum(m, jnp.max(s, axis=-1, keepdims=True))
        alpha = jnp.exp(m - m_new)
        p = jnp.exp(s - m_new)
        l = alpha * l + jnp.sum(p, axis=-1, keepdims=True)
        acc = alpha * acc + _dot(p.astype(BF16), v_ref[pl.ds(off, tk), :])
        return m_new, l, acc

    if n_lat:
        m, l, acc = lax.fori_loop(0, n_lat, body, (m, l, acc))
    o = acc / l
    for g in range(group):
        o_ref[:, g * dv:(g + 1) * dv] = o[g * tq:(g + 1) * tq].astype(o_ref.dtype)


def _flash(q_arr, k_arr, v_arr, *, n_kv_heads, group, dk, dv, q_col0, k_col0, v_col0, ctx_queries, tq, tk,
           q_scale=None):
    n_tok = k_arr.shape[0]
    if ctx_queries:
        n_q, q_blk0, n_lat = CTX_LEN, SEQ // tq, 0
    else:
        n_q, q_blk0, n_lat = SEQ, 0, SEQ // tk
    m = group * tq
    vmem = (2 * n_tok * (dk + dv) * 2 + 4 * tq * group * (dk + dv) * 2
            + 3 * m * max(tk, CTX_LEN) * 4 + 4 * m * dv * 4)
    return pl.pallas_call(
        functools.partial(_flash_kernel, group=group, dk=dk, dv=dv, tk=tk, n_lat=n_lat, q_scale=q_scale),
        grid=(n_kv_heads, n_q // tq),
        in_specs=[pl.BlockSpec((tq, group * dk), lambda h, i: (q_blk0 + i, q_col0 + h)),
                  pl.BlockSpec((n_tok, dk), lambda h, i: (0, k_col0 + h)),
                  pl.BlockSpec((n_tok, dv), lambda h, i: (0, v_col0 + h))],
        out_specs=pl.BlockSpec((tq, group * dv), lambda h, i: (i, h)),
        out_shape=jax.ShapeDtypeStruct((n_q, n_kv_heads * group * dv), BF16),
        compiler_params=_cparams(("arbitrary", "arbitrary"), vmem + (8 << 20)),
        name="flash_ctx" if ctx_queries else "flash",
    )(q_arr, k_arr, v_arr)


def _na_start_row(b):
    return jnp.clip(NA_RB * b - NA_KH // 2, 0, GRID_H - NA_U)


def _na_kernel(q_ref, k_ref, v_ref, bias_ref, o_ref, *, scale):
    b = pl.program_id(1)
    start = pl.multiple_of(_na_start_row(b) * GRID_W, GRID_W)
    nwin = NA_U * GRID_W
    q = (q_ref[...].astype(F32) * scale).astype(BF16)
    s_loc = _dot_nt(q, k_ref[pl.ds(start, nwin), :]) + bias_ref[...]
    s_ctx = _dot_nt(q, k_ref[SEQ:SEQ + CTX_LEN, :])
    m = jnp.maximum(jnp.max(s_loc, axis=-1, keepdims=True), jnp.max(s_ctx, axis=-1, keepdims=True))
    p_loc = jnp.exp(s_loc - m)
    p_ctx = jnp.exp(s_ctx - m)
    l = jnp.sum(p_loc, axis=-1, keepdims=True) + jnp.sum(p_ctx, axis=-1, keepdims=True)
    o = _dot(p_loc.astype(BF16), v_ref[pl.ds(start, nwin), :]) + _dot(p_ctx.astype(BF16), v_ref[SEQ:SEQ + CTX_LEN, :])
    o_ref[...] = (o / l).astype(o_ref.dtype)


def _na_bias_tables(rpb):
    h = rpb.shape[0]
    w = np.arange(GRID_W)
    c0 = np.clip(w - NA_KW // 2, 0, GRID_W - NA_KW)
    colmask = (w[None, :] >= c0[:, None]) & (w[None, :] < c0[:, None] + NA_KW)
    colidx = np.clip(w[None, :] - w[:, None] + NA_KW - 1, 0, 2 * NA_KW - 2)
    slabs = jnp.where(colmask[None, None], rpb[:, :, colidx], NEG)
    slabs = jnp.concatenate([slabs, jnp.full((h, 1, GRID_W, GRID_W), NEG, rpb.dtype)], axis=1)
    masked = 2 * NA_KH - 1
    nb = GRID_H // NA_RB
    tabs = []
    for b in (0, 1, nb - 1):
        start = int(np.clip(NA_RB * b - NA_KH // 2, 0, GRID_H - NA_U))
        a = np.full((NA_RB, NA_U), masked, np.int32)
        for j in range(NA_RB):
            r = NA_RB * b + j
            r0 = int(np.clip(r - NA_KH // 2, 0, GRID_H - NA_KH))
            for u in range(NA_U):
                kr = start + u
                if r0 <= kr < r0 + NA_KH:
                    a[j, u] = kr - r + NA_KH - 1
        t = slabs[:, a]
        tabs.append(t.transpose(0, 1, 3, 2, 4).reshape(h, NA_RB * GRID_W, NA_U * GRID_W))
    return jnp.stack(tabs, axis=1)


def _na_attention(qkv, bias_tabs):
    n_tok = qkv.shape[0]
    nb = GRID_H // NA_RB
    tq = NA_RB * GRID_W
    nwin = NA_U * GRID_W

    def cls(b):
        return jnp.where(b == 0, 0, jnp.where(b == nb - 1, 2, 1))

    vmem = 4 * n_tok * HEAD_DIM * 2 + 2 * tq * nwin * 4 + 4 * tq * (nwin + CTX_LEN) * 4
    return pl.pallas_call(
        functools.partial(_na_kernel, scale=HEAD_DIM ** -0.5),
        grid=(NA_HEADS, nb),
        in_specs=[pl.BlockSpec((tq, HEAD_DIM), lambda h, b: (b, h)),
                  pl.BlockSpec((n_tok, HEAD_DIM), lambda h, b: (0, NA_HEADS + h)),
                  pl.BlockSpec((n_tok, HEAD_DIM), lambda h, b: (0, 2 * NA_HEADS + h)),
                  pl.BlockSpec((None, None, tq, nwin), lambda h, b: (h, cls(b), 0, 0))],
        out_specs=pl.BlockSpec((tq, HEAD_DIM), lambda h, b: (b, h)),
        out_shape=jax.ShapeDtypeStruct((SEQ, NA_HEADS * HEAD_DIM), BF16),
        compiler_params=_cparams(("arbitrary", "arbitrary"), vmem + (8 << 20)),
        name="na_attention",
    )(qkv, qkv, qkv, bias_tabs)


def _dispatch_kernel(ps_ref, tail_ref, eidx_ref, rank_ref, h_ref, xg_hbm, zbuf, sem, zsem, *, tb):
    te = EXPERT_TILE

    @pl.when(pl.program_id(0) == 0)
    def _():
        zbuf[...] = jnp.zeros_like(zbuf)

        def zero_tail(e, wait):
            @pl.when(tail_ref[e] >= 0)
            def _():
                cp = pltpu.make_async_copy(zbuf, xg_hbm.at[pl.ds(pl.multiple_of(tail_ref[e], te), te)], zsem)
                if wait:
                    cp.wait()
                else:
                    cp.start()

        lax.fori_loop(0, N_EXPERTS, lambda e, c: (zero_tail(e, False), c)[1], 0)
        lax.fori_loop(0, N_EXPERTS, lambda e, c: (zero_tail(e, True), c)[1], 0)

    def body(t, carry):
        for k in range(MOE_TOP_K):
            slot = ps_ref[eidx_ref[k, t]] + rank_ref[k, t]
            pltpu.make_async_copy(h_ref.at[pl.ds(t, 1)], xg_hbm.at[pl.ds(slot, 1)], sem).start()
        return carry

    lax.fori_loop(0, tb, body, 0)
    for k in range(MOE_TOP_K):
        pltpu.make_async_copy(h_ref, xg_hbm.at[pl.ds(0, tb)], sem).wait()


def _dispatch(h32, eidx_t, rank_t, pad_start, tail_start, n_rows, n_slots):
    d = h32.shape[1]
    tb = ROW_TILE
    topk = pl.BlockSpec((MOE_TOP_K, tb), lambda i, ps, tl: (0, i), memory_space=pltpu.SMEM)
    grid_spec = pltpu.PrefetchScalarGridSpec(
        num_scalar_prefetch=2,
        grid=(n_rows // tb,),
        in_specs=[topk, topk, pl.BlockSpec((tb, d), lambda i, ps, tl: (i, 0))],
        out_specs=pl.BlockSpec(memory_space=pl.ANY),
        scratch_shapes=[pltpu.VMEM((EXPERT_TILE, d), F32), pltpu.SemaphoreType.DMA(()),
                        pltpu.SemaphoreType.DMA(())],
    )
    return pl.pallas_call(
        functools.partial(_dispatch_kernel, tb=tb),
        grid_spec=grid_spec,
        out_shape=jax.ShapeDtypeStruct((n_slots, d), F32),
        compiler_params=_cparams(("arbitrary",), 16 << 20),
        name="moe_dispatch",
    )(pad_start, tail_start, eidx_t, rank_t, h32)


def _ffn_kernel(be_ref, nu_ref, x_ref, wg_ref, wu_ref, wd_ref, o_ref, wgb, wub, wdb):
    b = pl.program_id(0)
    prev = be_ref[jnp.maximum(b - 1, 0)]
    new_expert = jnp.logical_or(b == 0, be_ref[b] != prev)

    @pl.when(jnp.logical_and(new_expert, b < nu_ref[0]))
    def _():
        wgb[...] = wg_ref[...].astype(BF16)
        wub[...] = wu_ref[...].astype(BF16)
        wdb[...] = wd_ref[...].astype(BF16)

    @pl.when(b < nu_ref[0])
    def _():
        x = x_ref[...].astype(BF16)
        g = _dot(x, wgb[...])
        u = _dot(x, wub[...])
        a = (g * jax.nn.sigmoid(g) * u).astype(BF16)
        o_ref[...] = _dot(a, wdb[...]).astype(o_ref.dtype)

    @pl.when(b >= nu_ref[0])
    def _():
        o_ref[...] = jnp.zeros_like(o_ref)


def _expert_ffn(x_rows, block_expert, n_used, wg, wu, wd, layer, tb, out_dtype):
    p, d = x_rows.shape
    de = wg.shape[-1]
    nb = p // tb
    xsz = x_rows.dtype.itemsize
    osz = jnp.dtype(out_dtype).itemsize
    vmem = 2 * 3 * d * de * 4 + 3 * d * de * 2 + 2 * tb * d * (xsz + osz) + 4 * tb * de * 4 + tb * d * 4

    def blk(b, be, nu):
        return jnp.minimum(b, nu[0] - 1)

    grid_spec = pltpu.PrefetchScalarGridSpec(
        num_scalar_prefetch=2,
        grid=(nb,),
        in_specs=[pl.BlockSpec((tb, d), lambda b, be, nu: (blk(b, be, nu), 0)),
                  pl.BlockSpec((None, None, d, de), lambda b, be, nu: (layer, be[blk(b, be, nu)], 0, 0)),
                  pl.BlockSpec((None, None, d, de), lambda b, be, nu: (layer, be[blk(b, be, nu)], 0, 0)),
                  pl.BlockSpec((None, None, de, d), lambda b, be, nu: (layer, be[blk(b, be, nu)], 0, 0))],
        out_specs=pl.BlockSpec((tb, d), lambda b, be, nu: (b, 0)),
        scratch_shapes=[pltpu.VMEM((d, de), BF16), pltpu.VMEM((d, de), BF16), pltpu.VMEM((de, d), BF16)],
    )
    return pl.pallas_call(
        _ffn_kernel,
        grid_spec=grid_spec,
        out_shape=jax.ShapeDtypeStruct((p, d), out_dtype),
        compiler_params=_cparams(("arbitrary",), vmem + (4 << 20)),
        name="expert_ffn",
    )(block_expert, n_used, x_rows, wg, wu, wd)


def _combine_kernel(ps_ref, e_cur, r_cur, e_nxt, r_nxt, y_hbm, ew_ref, ysh_ref, x_ref, gate_ref, o_ref, buf, sem,
                    *, tc, nblk):
    i = pl.program_id(0)

    def gather(e_ref, r_ref, bslot):
        def body(t, carry):
            for k in range(MOE_TOP_K):
                slot = ps_ref[e_ref[k, t]] + r_ref[k, t]
                pltpu.make_async_copy(y_hbm.at[pl.ds(slot, 1)],
                                      buf.at[bslot, k, pl.ds(t, 1)], sem.at[bslot]).start()
            return carry

        lax.fori_loop(0, tc, body, 0)

    @pl.when(i == 0)
    def _():
        gather(e_cur, r_cur, 0)

    cur = i % 2

    @pl.when(i + 1 < nblk)
    def _():
        @pl.when(cur == 0)
        def _():
            gather(e_nxt, r_nxt, 1)

        @pl.when(cur == 1)
        def _():
            gather(e_nxt, r_nxt, 0)

    def finish(bslot):
        for k in range(MOE_TOP_K):
            pltpu.make_async_copy(y_hbm.at[pl.ds(0, tc)], buf.at[bslot, k], sem.at[bslot]).wait()
        ew = ew_ref[...]
        acc = ysh_ref[...].astype(F32)
        for k in range(MOE_TOP_K):
            acc = acc + ew[:, k:k + 1] * buf[bslot, k]
        o_ref[...] = x_ref[...] + gate_ref[...] * acc

    @pl.when(cur == 0)
    def _():
        finish(0)

    @pl.when(cur == 1)
    def _():
        finish(1)


def _combine(y, eidx_t, rank_t, pad_start, ew, ysh, x, gate, n_rows):
    d = y.shape[1]
    tc = COMBINE_TILE
    nblk = n_rows // tc
    lat_blocks = SEQ // tc
    row = pl.BlockSpec((tc, d), lambda i, ps: (i, 0))
    cur = pl.BlockSpec((MOE_TOP_K, tc), lambda i, ps: (0, i), memory_space=pltpu.SMEM)
    nxt = pl.BlockSpec((MOE_TOP_K, tc), lambda i, ps: (0, jnp.minimum(i + 1, nblk - 1)), memory_space=pltpu.SMEM)
    vmem = 2 * MOE_TOP_K * tc * d * 4 + 8 * tc * d * 4
    grid_spec = pltpu.PrefetchScalarGridSpec(
        num_scalar_prefetch=1,
        grid=(nblk,),
        in_specs=[cur, cur, nxt, nxt,
                  pl.BlockSpec(memory_space=pl.ANY),
                  pl.BlockSpec((tc, MOE_TOP_K), lambda i, ps: (i, 0)),
                  row, row,
                  pl.BlockSpec((None, 1, d), lambda i, ps: (jnp.where(i >= lat_blocks, 1, 0), 0, 0))],
        out_specs=row,
        scratch_shapes=[pltpu.VMEM((2, MOE_TOP_K, tc, d), F32), pltpu.SemaphoreType.DMA((2,))],
    )
    return pl.pallas_call(
        functools.partial(_combine_kernel, tc=tc, nblk=nblk),
        grid_spec=grid_spec,
        out_shape=jax.ShapeDtypeStruct((n_rows, d), F32),
        compiler_params=_cparams(("arbitrary",), vmem + (4 << 20)),
        name="moe_combine",
    )(pad_start, eidx_t, rank_t, eidx_t, rank_t, y, ew, ysh, x, gate.reshape(2, 1, d))


def _routing_tables(counts, n_rows):
    tb = EXPERT_TILE
    n_blocks = -(-(n_rows * MOE_TOP_K) // tb) + N_EXPERTS
    padded = (counts + tb - 1) // tb * tb
    pad_end = jnp.cumsum(padded)
    pad_start = (pad_end - padded).astype(jnp.int32)
    tail_start = jnp.where(padded > 0, pad_end - tb, -1).astype(jnp.int32)
    blk_start = jnp.arange(n_blocks, dtype=jnp.int32) * tb
    block_expert = jnp.sum(blk_start[:, None] >= pad_end[None, :], axis=1)
    block_expert = jnp.minimum(block_expert, N_EXPERTS - 1).astype(jnp.int32)
    n_used = (pad_end[-1] // tb).astype(jnp.int32).reshape(1)
    return pad_start, tail_start, block_expert, n_used, n_blocks * tb


def _moe(layer, x_new, h, h32, eidx_t, ew_t, rank_t, counts, gate, n_rows, wg, wu, wd, sg, su, sd):
    pad_start, tail_start, block_expert, n_used, n_slots = _routing_tables(counts[:, 0], n_rows)
    xg = _dispatch(h32, eidx_t, rank_t, pad_start, tail_start, n_rows, n_slots)
    y = _expert_ffn(xg, block_expert, n_used, wg, wu, wd, layer, EXPERT_TILE, F32)
    tm = MM_TM if n_rows % MM_TM == 0 else 512
    ysh = _expert_ffn(h, jnp.zeros((n_rows // tm,), jnp.int32), jnp.full((1,), n_rows // tm, jnp.int32),
                      sg[:, None], su[:, None], sd[:, None], layer, tm, BF16)
    return _combine(y, eidx_t, rank_t, pad_start, ew_t.T, ysh, x_new, gate, n_rows)


def _rope_tables(d, width):
    t = jnp.arange(SEQ)
    row = (t // GRID_W).astype(F32)
    col = (t % GRID_W).astype(F32)
    n = d // 4
    inv = ROPE_THETA ** (-jnp.arange(n, dtype=F32) / n)
    ang = jnp.concatenate([row[:, None] * inv, col[:, None] * inv], axis=-1)
    ang = jnp.concatenate([ang, jnp.zeros((CTX_LEN, d // 2), F32)], axis=0)
    cos = jnp.repeat(jnp.cos(ang), 2, axis=1)
    sin = jnp.repeat(jnp.sin(ang), 2, axis=1) * jnp.tile(jnp.array([-1.0, 1.0], F32), d // 2)
    if width > d:
        cos = jnp.concatenate([cos, jnp.ones((N_TOK, width - d), F32)], axis=1)
        sin = jnp.concatenate([sin, jnp.zeros((N_TOK, width - d), F32)], axis=1)
    return cos, sin


def _mixer_even(h, w_in, rpb, q_norm, w_uq, kv_norm, w_ukv, w_out):
    n_na = 3 * NA_HEADS * HEAD_DIM
    qkv = _matmul(h, w_in[:, :n_na], N_TOK, BF16, MM_TM, 1024)
    w_tail = jnp.pad(w_in[:, n_na:], ((0, 0), (0, 64)))
    lat = _matmul(h, w_tail, N_TOK, F32, MM_TM, 384)
    cos_t, sin_t = _rope_tables(MLA_ROPE, 128)
    nq, nkv, kr = _mla_prep1(lat, q_norm, kv_norm, cos_t, sin_t)
    wq = w_uq.reshape(MLA_Q_RANK, MLA_HEADS, MLA_NOPE + MLA_ROPE)
    wq = jnp.pad(wq, ((0, 0), (0, 0), (0, MLA_QK_PAD - MLA_NOPE - MLA_ROPE))).reshape(MLA_Q_RANK, -1)
    wkv = w_ukv.reshape(MLA_KV_RANK, MLA_HEADS, MLA_NOPE + MLA_V)
    wkv = jnp.concatenate([wkv[:, :, :MLA_NOPE].reshape(MLA_KV_RANK, -1),
                           wkv[:, :, MLA_NOPE:].reshape(MLA_KV_RANK, -1)], axis=1)
    q_raw = _matmul(nq, wq, N_TOK, F32, MM_TM, 1024)
    kv_all = _matmul(nkv, wkv, N_TOK, BF16, MM_TM, 1024)
    q_mla, k_cat = _mla_prep2(q_raw, kv_all, kr, cos_t, sin_t)

    o_na = _na_attention(qkv, _na_bias_tables(rpb))
    mla = dict(n_kv_heads=MLA_HEADS, group=1, dk=MLA_QK_PAD, dv=MLA_V, q_col0=0, k_col0=0, v_col0=MLA_HEADS)
    o_mla = _flash(q_mla, k_cat, kv_all, ctx_queries=False, tq=512, tk=512, **mla)
    o_mla_c = _flash(q_mla, k_cat, kv_all, ctx_queries=True, tq=256, tk=512, **mla)
    na = dict(n_kv_heads=NA_HEADS, group=1, dk=HEAD_DIM, dv=HEAD_DIM, q_col0=0, k_col0=NA_HEADS, v_col0=2 * NA_HEADS)
    o_na_c = _flash(qkv, qkv, qkv, ctx_queries=True, tq=256, tk=512, q_scale=HEAD_DIM ** -0.5, **na)
    o = jnp.concatenate([jnp.concatenate([o_na, o_mla], axis=1),
                         jnp.concatenate([o_na_c, o_mla_c], axis=1)], axis=0)
    return _matmul(o, w_out, N_TOK, F32, MM_TM, 1024)


def _mixer_odd_last(h, w_qkv, q_norm, k_norm, w_o):
    qkv = _matmul(h, w_qkv, N_TOK, BF16, MM_TM, 1024)
    cos_t, sin_t = _rope_tables(HEAD_DIM, HEAD_DIM)
    q_g, k_g = _gqa_prep(qkv, q_norm, k_norm, cos_t, sin_t)
    o = _flash(q_g, k_g, qkv, n_kv_heads=GQA_KV, group=GQA_GROUP, dk=HEAD_DIM, dv=HEAD_DIM,
               q_col0=0, k_col0=0, v_col0=GQA_HEADS + GQA_KV, ctx_queries=False, tq=256, tk=512)
    return _matmul(o, w_o, SEQ, F32, 512, 1024)


def kernel(x, c, ctx, c_ctx, ada_w, ada_b, mix_norm, ffn_norm, ev_w_in, ev_rpb, ev_q_norm, ev_w_uq, ev_kv_norm, ev_w_ukv, ev_w_out, od_w_qkv, od_q_norm, od_k_norm, od_w_o, router_w, router_b, exp_w_gate, exp_w_up, exp_w_down, sh_w_gate, sh_w_up, sh_w_down, final_norm):
    assert DEPTH == 2 and x.shape == (1, SEQ, D_MODEL)
    mods = _adaln_mods(c, c_ctx, ada_w, ada_b)
    xt = jnp.concatenate([x[0], ctx[0]], axis=0)

    def parts(i):
        return [mods[i, :, k * D_MODEL:(k + 1) * D_MODEL] for k in range(6)]

    sh1, sc1, g1, sh2, sc2, g2 = parts(0)
    (h,) = _norm_call(xt, N_TOK, mix_norm[0], mod=(sh1, sc1))
    o = _mixer_even(h, ev_w_in[0], ev_rpb[0], ev_q_norm[0], ev_w_uq[0], ev_kv_norm[0], ev_w_ukv[0], ev_w_out[0])
    xt, h, h32, eidx_t, ew_t, rank_t, counts = _norm_call(xt, N_TOK, ffn_norm[0], resid=(o, g1), mod=(sh2, sc2),
                                                          router=(router_w[0], router_b[0]))
    xt = _moe(0, xt, h, h32, eidx_t, ew_t, rank_t, counts, g2, N_TOK, exp_w_gate, exp_w_up, exp_w_down,
              sh_w_gate, sh_w_up, sh_w_down)

    sh1, sc1, g1, sh2, sc2, g2 = parts(1)
    (h,) = _norm_call(xt, N_TOK, mix_norm[1], mod=(sh1, sc1))
    o = _mixer_odd_last(h, od_w_qkv[0], od_q_norm[0], od_k_norm[0], od_w_o[0])
    xl, h, h32, eidx_t, ew_t, rank_t, counts = _norm_call(xt, SEQ, ffn_norm[1], resid=(o, g1), mod=(sh2, sc2),
                                                          router=(router_w[1], router_b[1]))
    xl = _moe(1, xl, h, h32, eidx_t, ew_t, rank_t, counts, g2, SEQ, exp_w_gate, exp_w_up, exp_w_down,
              sh_w_gate, sh_w_up, sh_w_down)
    (out,) = _norm_call(xl, SEQ, final_norm, out_dtype=F32)
    return out[None]
```

```python
import functools

import numpy as np
import jax
import jax.numpy as jnp
from jax import lax
from jax.experimental import pallas as pl
from jax.experimental.pallas import tpu as pltpu

F32 = jnp.float32
BF16 = jnp.bfloat16

D_MODEL = 2048
SEQ = 8192
DEPTH = 2
GRID_W = 64
GRID_H = SEQ // GRID_W
CTX_LEN = 256
N_TOK = SEQ + CTX_LEN
EPS = 1e-6
ROPE_THETA = 10000.0
HEAD_DIM = 128
NA_HEADS = 8
NA_KH = 8
NA_KW = 16
MLA_HEADS = 8
MLA_Q_RANK = 512
MLA_KV_RANK = 512
MLA_NOPE = 128
MLA_ROPE = 64
MLA_V = 128
MLA_QK_PAD = 256
GQA_HEADS = 16
GQA_KV = 4
GQA_GROUP = GQA_HEADS // GQA_KV
N_EXPERTS = 64
N_GROUPS = 8
GROUP_SIZE = N_EXPERTS // N_GROUPS
TOPK_GROUPS = 4
MOE_TOP_K = 8
D_EXPERT = 512
ROUTED_SCALE = 2.5

NEG = -1e30

VMEM_LIMIT_CAP = 56 * 1024 * 1024

ROW_TILE = 256
MM_TM = 768
NA_RB = 8
NA_U = NA_RB + NA_KH - 1
EXPERT_TILE = 256
COMBINE_TILE = 128


def _cparams(sem, vmem_bytes):
    return pltpu.CompilerParams(
        dimension_semantics=sem,
        vmem_limit_bytes=int(min(max(vmem_bytes, 16 * 1024 * 1024), VMEM_LIMIT_CAP)),
    )


def _dot(a, b):
    return jnp.dot(a, b, preferred_element_type=F32)


def _dot_nt(a, b):
    return lax.dot_general(a, b, (((1,), (1,)), ((), ())), preferred_element_type=F32)


def _mods_kernel(cc_ref, w_ref, b_ref, o_ref, acc_ref, *, nk):
    k = pl.program_id(2)

    @pl.when(k == 0)
    def _():
        acc_ref[...] = jnp.zeros_like(acc_ref)

    a = cc_ref[...]
    a = a * jax.nn.sigmoid(a)
    w = w_ref[...]
    tk, tn = w.shape
    w3 = w.reshape(tk // 8, 8, tn)
    for v in range(2):
        col = a[:, v:v + 1].reshape(tk // 8, 8, 1)
        acc_ref[v] += jnp.sum(w3 * col, axis=0)

    @pl.when(k == nk - 1)
    def _():
        o_ref[...] = jnp.sum(acc_ref[...], axis=1) + b_ref[...]


def _adaln_mods(c, c_ctx, ada_w, ada_b):
    depth, d, n = ada_w.shape
    tk, tn = 512, 1536
    nk = d // tk
    cc = jnp.stack([c[0], c_ctx], axis=1)
    return pl.pallas_call(
        functools.partial(_mods_kernel, nk=nk),
        grid=(depth, n // tn, nk),
        in_specs=[
            pl.BlockSpec((tk, 2), lambda i, j, k: (k, 0)),
            pl.BlockSpec((None, tk, tn), lambda i, j, k: (i, k, j)),
            pl.BlockSpec((None, 1, tn), lambda i, j, k: (i, 0, j)),
        ],
        out_specs=pl.BlockSpec((None, 2, tn), lambda i, j, k: (i, 0, j)),
        out_shape=jax.ShapeDtypeStruct((depth, 2, n), F32),
        scratch_shapes=[pltpu.VMEM((2, 8, tn), F32)],
        compiler_params=_cparams(("arbitrary", "arbitrary", "arbitrary"), 4 * tk * tn * 4),
        name="adaln_mods",
    )(cc, ada_w, ada_b.reshape(depth, 1, n))


def _router_select(hf, rwt_ref, rb_ref, base_ref):
    logits = lax.dot_general(rwt_ref[...], hf, (((1,), (1,)), ((), ())),
                             precision=lax.Precision.HIGHEST, preferred_element_type=F32)
    scores = jax.nn.sigmoid(logits)
    sel = scores + rb_ref[...]
    tm = sel.shape[1]
    ninf = jnp.float32(-jnp.inf)
    sub = lax.broadcasted_iota(jnp.int32, (GROUP_SIZE, tm), 0)
    gi = lax.broadcasted_iota(jnp.int32, (N_GROUPS, tm), 0)
    gs = jnp.zeros((N_GROUPS, tm), F32)
    for g in range(N_GROUPS):
        blk = sel[g * GROUP_SIZE:(g + 1) * GROUP_SIZE, :]
        m1 = jnp.max(blk, axis=0, keepdims=True)
        first = jnp.min(jnp.where(blk == m1, sub, GROUP_SIZE), axis=0, keepdims=True)
        m2 = jnp.max(jnp.where(sub == first, ninf, blk), axis=0, keepdims=True)
        gs = jnp.where(gi == g, m1 + m2, gs)
    gsel = jnp.zeros((N_GROUPS, tm), F32)
    cur = gs
    for _ in range(TOPK_GROUPS):
        mx = jnp.max(cur, axis=0, keepdims=True)
        idx = jnp.min(jnp.where(cur == mx, gi, N_GROUPS), axis=0, keepdims=True)
        pick = gi == idx
        gsel = jnp.where(pick, 1.0, gsel)
        cur = jnp.where(pick, ninf, cur)
    ei = lax.broadcasted_iota(jnp.int32, (N_EXPERTS, tm), 0)
    eg = ei // GROUP_SIZE
    emask = jnp.zeros((N_EXPERTS, tm), F32)
    for g in range(N_GROUPS):
        emask = jnp.where(eg == g, gsel[g:g + 1, :], emask)
    cur = jnp.where(emask > 0.0, sel, ninf)
    idxs, ws, picks = [], [], []
    for _ in range(MOE_TOP_K):
        mx = jnp.max(cur, axis=0, keepdims=True)
        idx = jnp.min(jnp.where(cur == mx, ei, N_EXPERTS), axis=0, keepdims=True)
        pick = ei == idx
        ws.append(jnp.sum(jnp.where(pick, scores, 0.0), axis=0, keepdims=True))
        idxs.append(idx)
        picks.append(pick)
        cur = jnp.where(pick, ninf, cur)
    total = ws[0]
    for w in ws[1:]:
        total = total + w
    ws = [w / total * ROUTED_SCALE for w in ws]
    onehot = jnp.zeros((N_EXPERTS, tm), F32)
    for pick in picks:
        onehot = jnp.where(pick, 1.0, onehot)
    tri = (lax.broadcasted_iota(jnp.int32, (tm, tm), 0) < lax.broadcasted_iota(jnp.int32, (tm, tm), 1))
    before = _dot(onehot.astype(BF16), jnp.where(tri, 1.0, 0.0).astype(BF16)) + base_ref[...]
    ranks = [jnp.sum(jnp.where(pick, before, 0.0), axis=0, keepdims=True).astype(jnp.int32) for pick in picks]
    base_ref[...] += jnp.sum(onehot, axis=1, keepdims=True)
    return idxs, ws, ranks


def _norm_kernel(*refs, has_resid, modulated, with_router, out_dtype):
    it = iter(refs)
    x_ref = next(it)
    if has_resid:
        o_ref = next(it)
        gate_ref = next(it)
    g_ref = next(it)
    if modulated:
        shift_ref = next(it)
        scale_ref = next(it)
    if with_router:
        rwt_ref = next(it)
        rb_ref = next(it)
    if has_resid:
        xn_ref = next(it)
    h_ref = next(it)
    if with_router:
        h32_ref = next(it)
        eidx_ref = next(it)
        ew_ref = next(it)
        rank_ref = next(it)
        cnt_ref = next(it)
        base_ref = next(it)

        @pl.when(pl.program_id(0) == 0)
        def _():
            base_ref[...] = jnp.zeros_like(base_ref)

    x = x_ref[...]
    if has_resid:
        x = x + gate_ref[...] * o_ref[...].astype(F32)
        xn_ref[...] = x
    y = x * lax.rsqrt(jnp.mean(x * x, axis=-1, keepdims=True) + EPS)
    y = y * g_ref[...]
    if modulated:
        y = y * (1.0 + scale_ref[...]) + shift_ref[...]
    h_ref[...] = y.astype(out_dtype)
    if with_router:
        h32_ref[...] = y
        idxs, ws, ranks = _router_select(y, rwt_ref, rb_ref, base_ref)
        for k in range(MOE_TOP_K):
            eidx_ref[k:k + 1, :] = idxs[k]
            ew_ref[k:k + 1, :] = ws[k]
            rank_ref[k:k + 1, :] = ranks[k]
        cnt_ref[...] = base_ref[...].astype(jnp.int32)


def _norm_call(x, n_rows, gain, *, resid=None, mod=None, router=None, out_dtype=BF16):
    d = x.shape[1]
    tm = ROW_TILE
    nblk = n_rows // tm
    lat_blocks = SEQ // tm

    def cls_map(i):
        return (jnp.where(i >= lat_blocks, 1, 0), 0, 0)

    row = pl.BlockSpec((tm, d), lambda i: (i, 0))
    vec = pl.BlockSpec((None, 1, d), cls_map)
    args, in_specs = [x], [row]
    if resid is not None:
        o, gate = resid
        args += [o, gate.reshape(2, 1, d)]
        in_specs += [row, vec]
    args.append(gain.reshape(1, d))
    in_specs.append(pl.BlockSpec((1, d), lambda i: (0, 0)))
    if mod is not None:
        shift, scale = mod
        args += [shift.reshape(2, 1, d), scale.reshape(2, 1, d)]
        in_specs += [vec, vec]
    if router is not None:
        rw, rb = router
        args += [rw.T, rb.reshape(N_EXPERTS, 1)]
        in_specs += [pl.BlockSpec((N_EXPERTS, d), lambda i: (0, 0)),
                     pl.BlockSpec((N_EXPERTS, 1), lambda i: (0, 0))]
    out_shape, out_specs = [], []
    if resid is not None:
        out_shape.append(jax.ShapeDtypeStruct((n_rows, d), F32))
        out_specs.append(row)
    out_shape.append(jax.ShapeDtypeStruct((n_rows, d), out_dtype))
    out_specs.append(row)
    if router is not None:
        out_shape += [jax.ShapeDtypeStruct((n_rows, d), F32),
                      jax.ShapeDtypeStruct((MOE_TOP_K, n_rows), jnp.int32),
                      jax.ShapeDtypeStruct((MOE_TOP_K, n_rows), F32),
                      jax.ShapeDtypeStruct((MOE_TOP_K, n_rows), jnp.int32),
                      jax.ShapeDtypeStruct((N_EXPERTS, 1), jnp.int32)]
        topk = pl.BlockSpec((MOE_TOP_K, tm), lambda i: (0, i))
        out_specs += [row, topk, topk, topk, pl.BlockSpec((N_EXPERTS, 1), lambda i: (0, 0))]
    return pl.pallas_call(
        functools.partial(_norm_kernel, has_resid=resid is not None, modulated=mod is not None,
                          with_router=router is not None, out_dtype=out_dtype),
        grid=(nblk,),
        in_specs=in_specs,
        out_specs=out_specs,
        out_shape=out_shape,
        scratch_shapes=[pltpu.VMEM((N_EXPERTS, 1), F32)] if router is not None else [],
        compiler_params=_cparams(("arbitrary",), 16 * tm * d * 4),
        name="norm_router" if router is not None else "norm",
    )(*args)


def _mm_kernel(a_ref, w_ref, o_ref, wb_ref):
    @pl.when(pl.program_id(1) == 0)
    def _():
        wb_ref[...] = w_ref[...].astype(BF16)

    o_ref[...] = _dot(a_ref[...], wb_ref[...]).astype(o_ref.dtype)


def _matmul(a, w, m_rows, out_dtype, tm, tn):
    k, n = w.shape
    assert m_rows % tm == 0 and n % tn == 0 and a.shape[1] == k
    osz = jnp.dtype(out_dtype).itemsize
    vmem = 2 * k * tn * 4 + k * tn * 2 + 2 * tm * k * 2 + 2 * tm * tn * osz + tm * tn * 4
    return pl.pallas_call(
        _mm_kernel,
        grid=(n // tn, m_rows // tm),
        in_specs=[pl.BlockSpec((tm, k), lambda j, i: (i, 0)),
                  pl.BlockSpec((k, tn), lambda j, i: (0, j))],
        out_specs=pl.BlockSpec((tm, tn), lambda j, i: (i, j)),
        out_shape=jax.ShapeDtypeStruct((m_rows, n), out_dtype),
        scratch_shapes=[pltpu.VMEM((k, tn), BF16)],
        compiler_params=_cparams(("arbitrary", "arbitrary"), vmem + (4 << 20)),
        name="matmul",
    )(a, w)


def _rope(x, cos_e, sin_s):
    w = x.shape[-1]
    lane = lax.broadcasted_iota(jnp.int32, x.shape, 1)
    nxt = pltpu.roll(x, w - 1, 1)
    prv = pltpu.roll(x, 1, 1)
    swapped = jnp.where(lane % 2 == 0, nxt, prv)
    return x * cos_e + swapped * sin_s


def _rms(x, g):
    return x * lax.rsqrt(jnp.mean(x * x, axis=-1, keepdims=True) + EPS) * g


def _mla_prep1_kernel(lat_ref, qg_ref, kvg_ref, cos_ref, sin_ref, nq_ref, nkv_ref, kr_ref):
    lat = lat_ref[...]
    nq_ref[...] = _rms(lat[:, :MLA_Q_RANK], qg_ref[...]).astype(BF16)
    nkv_ref[...] = _rms(lat[:, MLA_Q_RANK:MLA_Q_RANK + MLA_KV_RANK], kvg_ref[...]).astype(BF16)
    kr = lat[:, MLA_Q_RANK + MLA_KV_RANK:]
    kr_ref[...] = _rope(kr, cos_ref[...], sin_ref[...]).astype(BF16)


def _mla_prep1(lat, q_norm, kv_norm, cos_t, sin_t):
    n, wdt = lat.shape
    tm = ROW_TILE
    row = lambda w: pl.BlockSpec((tm, w), lambda i: (i, 0))
    one = lambda w: pl.BlockSpec((1, w), lambda i: (0, 0))
    return pl.pallas_call(
        _mla_prep1_kernel,
        grid=(n // tm,),
        in_specs=[row(wdt), one(MLA_Q_RANK), one(MLA_KV_RANK), row(128), row(128)],
        out_specs=[row(MLA_Q_RANK), row(MLA_KV_RANK), row(128)],
        out_shape=[jax.ShapeDtypeStruct((n, MLA_Q_RANK), BF16),
                   jax.ShapeDtypeStruct((n, MLA_KV_RANK), BF16),
                   jax.ShapeDtypeStruct((n, 128), BF16)],
        compiler_params=_cparams(("arbitrary",), 8 << 20),
        name="mla_prep1",
    )(lat, q_norm.reshape(1, -1), kv_norm.reshape(1, -1), cos_t, sin_t)


def _mla_prep2_kernel(q_ref, kn_ref, kr_ref, cos_ref, sin_ref, qo_ref, ko_ref, *, scale):
    cos_e = cos_ref[...]
    sin_s = sin_ref[...]
    kr = kr_ref[...]
    for h in range(MLA_HEADS):
        lo = h * MLA_QK_PAD
        qo_ref[:, lo:lo + MLA_NOPE] = (q_ref[:, lo:lo + MLA_NOPE].astype(F32) * scale).astype(BF16)
        qr = q_ref[:, lo + MLA_NOPE:lo + MLA_QK_PAD].astype(F32)
        qo_ref[:, lo + MLA_NOPE:lo + MLA_QK_PAD] = (_rope(qr, cos_e, sin_s) * scale).astype(BF16)
        ko_ref[:, lo:lo + MLA_NOPE] = kn_ref[:, h * MLA_NOPE:(h + 1) * MLA_NOPE]
        ko_ref[:, lo + MLA_NOPE:lo + MLA_QK_PAD] = kr


def _mla_prep2(q_raw, kv_all, kr, cos_t, sin_t):
    n = q_raw.shape[0]
    tm = ROW_TILE
    wq = MLA_HEADS * MLA_QK_PAD
    row = lambda w: pl.BlockSpec((tm, w), lambda i: (i, 0))
    return pl.pallas_call(
        functools.partial(_mla_prep2_kernel, scale=(MLA_NOPE + MLA_ROPE) ** -0.5),
        grid=(n // tm,),
        in_specs=[row(wq), row(MLA_HEADS * MLA_NOPE), row(128), row(128), row(128)],
        out_specs=[row(wq), row(wq)],
        out_shape=[jax.ShapeDtypeStruct((n, wq), BF16), jax.ShapeDtypeStruct((n, wq), BF16)],
        compiler_params=_cparams(("arbitrary",), 16 << 20),
        name="mla_prep2",
    )(q_raw, kv_all, kr, cos_t, sin_t)


def _gqa_prep_kernel(qkv_ref, qg_ref, kg_ref, cos_ref, sin_ref, qo_ref, ko_ref, *, scale):
    cos_e = cos_ref[...]
    sin_s = sin_ref[...]
    for h in range(GQA_HEADS):
        xh = qkv_ref[:, h * HEAD_DIM:(h + 1) * HEAD_DIM].astype(F32)
        qo_ref[:, h * HEAD_DIM:(h + 1) * HEAD_DIM] = (
            _rope(_rms(xh, qg_ref[...]), cos_e, sin_s) * scale).astype(BF16)
    for h in range(GQA_KV):
        lo = (GQA_HEADS + h) * HEAD_DIM
        xh = qkv_ref[:, lo:lo + HEAD_DIM].astype(F32)
        ko_ref[:, h * HEAD_DIM:(h + 1) * HEAD_DIM] = _rope(_rms(xh, kg_ref[...]), cos_e, sin_s).astype(BF16)


def _gqa_prep(qkv, q_norm, k_norm, cos_t, sin_t):
    n = qkv.shape[0]
    tm = ROW_TILE
    wqk = (GQA_HEADS + GQA_KV) * HEAD_DIM
    row = lambda w: pl.BlockSpec((tm, w), lambda i: (i, 0))
    one = pl.BlockSpec((1, HEAD_DIM), lambda i: (0, 0))
    return pl.pallas_call(
        functools.partial(_gqa_prep_kernel, scale=HEAD_DIM ** -0.5),
        grid=(n // tm,),
        in_specs=[row(wqk), one, one, row(HEAD_DIM), row(HEAD_DIM)],
        out_specs=[row(GQA_HEADS * HEAD_DIM), row(GQA_KV * HEAD_DIM)],
        out_shape=[jax.ShapeDtypeStruct((n, GQA_HEADS * HEAD_DIM), BF16),
                   jax.ShapeDtypeStruct((n, GQA_KV * HEAD_DIM), BF16)],
        compiler_params=_cparams(("arbitrary",), 16 << 20),
        name="gqa_prep",
    )(qkv, q_norm.reshape(1, -1), k_norm.reshape(1, -1), cos_t, sin_t)


def _flash_kernel(q_ref, k_ref, v_ref, o_ref, *, group, dk, dv, tk, n_lat, q_scale, row_split):
    tq = q_ref.shape[0]
    rs = tq // max(row_split, 1)
    if row_split == 0:
        streams = [None]
        qs = [jnp.concatenate([q_ref[:, g * dk:(g + 1) * dk] for g in range(group)], axis=0)]
    else:
        streams = [(g, r) for g in range(group) for r in range(row_split)]
        qs = [q_ref[r * rs:(r + 1) * rs, g * dk:(g + 1) * dk] for g, r in streams]
    if q_scale is not None:
        qs = [(q.astype(F32) * q_scale).astype(BF16) for q in qs]

    def first(q):
        s = _dot_nt(q, k_ref[SEQ:SEQ + CTX_LEN, :])
        m = jnp.max(s, axis=-1, keepdims=True)
        p = jnp.exp(s - m)
        l = jnp.sum(p, axis=-1, keepdims=True)
        return m, l, _dot(p.astype(BF16), v_ref[SEQ:SEQ + CTX_LEN, :])

    def step(q, kb, vb, m, l, acc):
        s = _dot_nt(q, kb)
        m_new = jnp.maximum(m, jnp.max(s, axis=-1, keepdims=True))
        alpha = jnp.exp(m - m_new)
        p = jnp.exp(s - m_new)
        l = alpha * l + jnp.sum(p, axis=-1, keepdims=True)
        acc = alpha * acc + _dot(p.astype(BF16), vb)
        return m_new, l, acc

    def body(j, carry):
        off = pl.multiple_of(j * tk, tk)
        kb = k_ref[pl.ds(off, tk), :]
        vb = v_ref[pl.ds(off, tk), :]
        return tuple(step(q, kb, vb, *c) for q, c in zip(qs, carry))

    carry = tuple(first(q) for q in qs)
    if n_lat:
        carry = lax.fori_loop(0, n_lat, body, carry)
    if row_split == 0:
        m, l, acc = carry[0]
        o = (acc / l).astype(o_ref.dtype)
        for g in range(group):
            o_ref[:, g * dv:(g + 1) * dv] = o[g * tq:(g + 1) * tq]
    else:
        for (g, r), (m, l, acc) in zip(streams, carry):
            o_ref[r * rs:(r + 1) * rs, g * dv:(g + 1) * dv] = (acc / l).astype(o_ref.dtype)


def _flash(q_arr, k_arr, v_arr, *, n_kv_heads, group, dk, dv, q_col0, k_col0, v_col0, ctx_queries, tq, tk,
           q_scale=None, row_split=1):
    n_tok = k_arr.shape[0]
    if ctx_queries:
        n_q, q_blk0, n_lat = CTX_LEN, SEQ // tq, 0
    else:
        n_q, q_blk0, n_lat = SEQ, 0, SEQ // tk
    m = group * tq
    vmem = (2 * n_tok * (dk + dv) * 2 + 4 * tq * group * (dk + dv) * 2
            + 3 * m * max(tk, CTX_LEN) * 4 + 4 * m * dv * 4)
    return pl.pallas_call(
        functools.partial(_flash_kernel, group=group, dk=dk, dv=dv, tk=tk, n_lat=n_lat, q_scale=q_scale,
                          row_split=row_split),
        grid=(n_kv_heads, n_q // tq),
        in_specs=[pl.BlockSpec((tq, group * dk), lambda h, i: (q_blk0 + i, q_col0 + h)),
                  pl.BlockSpec((n_tok, dk), lambda h, i: (0, k_col0 + h)),
                  pl.BlockSpec((n_tok, dv), lambda h, i: (0, v_col0 + h))],
        out_specs=pl.BlockSpec((tq, group * dv), lambda h, i: (i, h)),
        out_shape=jax.ShapeDtypeStruct((n_q, n_kv_heads * group * dv), BF16),
        compiler_params=_cparams(("arbitrary", "arbitrary"), vmem + (8 << 20)),
        name="flash_ctx" if ctx_queries else "flash",
    )(q_arr, k_arr, v_arr)


def _na_start_row(b):
    return jnp.clip(NA_RB * b - NA_KH // 2, 0, GRID_H - NA_U)


def _na_kernel(q_ref, k_ref, v_ref, bias_ref, o_ref, *, scale):
    b = pl.program_id(1)
    start = pl.multiple_of(_na_start_row(b) * GRID_W, GRID_W)
    nwin = NA_U * GRID_W
    q = (q_ref[...].astype(F32) * scale).astype(BF16)
    s_loc = _dot_nt(q, k_ref[pl.ds(start, nwin), :]) + bias_ref[...]
    s_ctx = _dot_nt(q, k_ref[SEQ:SEQ + CTX_LEN, :])
    m = jnp.maximum(jnp.max(s_loc, axis=-1, keepdims=True), jnp.max(s_ctx, axis=-1, keepdims=True))
    p_loc = jnp.exp(s_loc - m)
    p_ctx = jnp.exp(s_ctx - m)
    l = jnp.sum(p_loc, axis=-1, keepdims=True) + jnp.sum(p_ctx, axis=-1, keepdims=True)
    o = _dot(p_loc.astype(BF16), v_ref[pl.ds(start, nwin), :]) + _dot(p_ctx.astype(BF16), v_ref[SEQ:SEQ + CTX_LEN, :])
    o_ref[...] = (o / l).astype(o_ref.dtype)


def _na_bias_tables(rpb):
    h = rpb.shape[0]
    w = np.arange(GRID_W)
    c0 = np.clip(w - NA_KW // 2, 0, GRID_W - NA_KW)
    colmask = (w[None, :] >= c0[:, None]) & (w[None, :] < c0[:, None] + NA_KW)
    colidx = np.clip(w[None, :] - w[:, None] + NA_KW - 1, 0, 2 * NA_KW - 2)
    slabs = jnp.where(colmask[None, None], rpb[:, :, colidx], NEG)
    slabs = jnp.concatenate([slabs, jnp.full((h, 1, GRID_W, GRID_W), NEG, rpb.dtype)], axis=1)
    masked = 2 * NA_KH - 1
    nb = GRID_H // NA_RB
    tabs = []
    for b in (0, 1, nb - 1):
        start = int(np.clip(NA_RB * b - NA_KH // 2, 0, GRID_H - NA_U))
        a = np.full((NA_RB, NA_U), masked, np.int32)
        for j in range(NA_RB):
            r = NA_RB * b + j
            r0 = int(np.clip(r - NA_KH // 2, 0, GRID_H - NA_KH))
            for u in range(NA_U):
                kr = start + u
                if r0 <= kr < r0 + NA_KH:
                    a[j, u] = kr - r + NA_KH - 1
        t = slabs[:, a]
        tabs.append(t.transpose(0, 1, 3, 2, 4).reshape(h, NA_RB * GRID_W, NA_U * GRID_W))
    return jnp.stack(tabs, axis=1)


def _na_attention(qkv, bias_tabs):
    n_tok = qkv.shape[0]
    nb = GRID_H // NA_RB
    tq = NA_RB * GRID_W
    nwin = NA_U * GRID_W

    def cls(b):
        return jnp.where(b == 0, 0, jnp.where(b == nb - 1, 2, 1))

    vmem = 4 * n_tok * HEAD_DIM * 2 + 2 * tq * nwin * 4 + 4 * tq * (nwin + CTX_LEN) * 4
    return pl.pallas_call(
        functools.partial(_na_kernel, scale=HEAD_DIM ** -0.5),
        grid=(NA_HEADS, nb),
        in_specs=[pl.BlockSpec((tq, HEAD_DIM), lambda h, b: (b, h)),
                  pl.BlockSpec((n_tok, HEAD_DIM), lambda h, b: (0, NA_HEADS + h)),
                  pl.BlockSpec((n_tok, HEAD_DIM), lambda h, b: (0, 2 * NA_HEADS + h)),
                  pl.BlockSpec((None, None, tq, nwin), lambda h, b: (h, cls(b), 0, 0))],
        out_specs=pl.BlockSpec((tq, HEAD_DIM), lambda h, b: (b, h)),
        out_shape=jax.ShapeDtypeStruct((SEQ, NA_HEADS * HEAD_DIM), BF16),
        compiler_params=_cparams(("arbitrary", "arbitrary"), vmem + (8 << 20)),
        name="na_attention",
    )(qkv, qkv, qkv, bias_tabs)


def _dispatch_kernel(ps_ref, tail_ref, nu_ref, eidx_ref, rank_ref, h_ref, xg_hbm, zbuf, sem, zsem, *, tb, n_blocks):
    te = EXPERT_TILE

    @pl.when(pl.program_id(0) == 0)
    def _():
        zbuf[...] = jnp.zeros_like(zbuf)

        def zero_tile(row0, wait):
            cp = pltpu.make_async_copy(zbuf, xg_hbm.at[pl.ds(pl.multiple_of(row0, te), te)], zsem)
            if wait:
                cp.wait()
            else:
                cp.start()

        def zero_tail(e, wait):
            @pl.when(tail_ref[e] >= 0)
            def _():
                zero_tile(tail_ref[e], wait)

        for wait in (False, True):
            lax.fori_loop(0, N_EXPERTS, lambda e, c: (zero_tail(e, wait), c)[1], 0)
            lax.fori_loop(nu_ref[0], n_blocks, lambda b, c: (zero_tile(b * te, wait), c)[1], 0)

    def body(t, carry):
        for k in range(MOE_TOP_K):
            slot = ps_ref[eidx_ref[k, t]] + rank_ref[k, t]
            pltpu.make_async_copy(h_ref.at[pl.ds(t, 1)], xg_hbm.at[pl.ds(slot, 1)], sem).start()
        return carry

    lax.fori_loop(0, tb, body, 0)
    for k in range(MOE_TOP_K):
        pltpu.make_async_copy(h_ref, xg_hbm.at[pl.ds(0, tb)], sem).wait()


def _dispatch(h32, eidx_t, rank_t, pad_start, tail_start, n_used, n_rows, n_slots):
    d = h32.shape[1]
    tb = ROW_TILE
    topk = pl.BlockSpec((MOE_TOP_K, tb), lambda i, ps, tl, nu: (0, i), memory_space=pltpu.SMEM)
    grid_spec = pltpu.PrefetchScalarGridSpec(
        num_scalar_prefetch=3,
        grid=(n_rows // tb,),
        in_specs=[topk, topk, pl.BlockSpec((tb, d), lambda i, ps, tl, nu: (i, 0))],
        out_specs=pl.BlockSpec(memory_space=pl.ANY),
        scratch_shapes=[pltpu.VMEM((EXPERT_TILE, d), F32), pltpu.SemaphoreType.DMA(()),
                        pltpu.SemaphoreType.DMA(())],
    )
    return pl.pallas_call(
        functools.partial(_dispatch_kernel, tb=tb, n_blocks=n_slots // EXPERT_TILE),
        grid_spec=grid_spec,
        out_shape=jax.ShapeDtypeStruct((n_slots, d), F32),
        compiler_params=_cparams(("arbitrary",), 16 << 20),
        name="moe_dispatch",
    )(pad_start, tail_start, n_used, eidx_t, rank_t, h32)


def _ffn_kernel(be_ref, nu_ref, x_ref, wg_ref, wu_ref, wd_ref, o_ref, wgb, wub, wdb):
    b = pl.program_id(0)
    prev = be_ref[jnp.maximum(b - 1, 0)]
    new_expert = jnp.logical_or(b == 0, be_ref[b] != prev)

    @pl.when(jnp.logical_and(new_expert, b < nu_ref[0]))
    def _():
        wgb[...] = wg_ref[...].astype(BF16)
        wub[...] = wu_ref[...].astype(BF16)
        wdb[...] = wd_ref[...].astype(BF16)

    @pl.when(b < nu_ref[0])
    def _():
        x = x_ref[...].astype(BF16)
        g = _dot(x, wgb[...])
        u = _dot(x, wub[...])
        a = (g * jax.nn.sigmoid(g) * u).astype(BF16)
        o_ref[...] = _dot(a, wdb[...]).astype(o_ref.dtype)

    @pl.when(b >= nu_ref[0])
    def _():
        o_ref[...] = jnp.zeros_like(o_ref)


def _expert_ffn(x_rows, block_expert, n_used, wg, wu, wd, layer, tb, out_dtype):
    p, d = x_rows.shape
    de = wg.shape[-1]
    nb = p // tb
    xsz = x_rows.dtype.itemsize
    osz = jnp.dtype(out_dtype).itemsize
    vmem = 2 * 3 * d * de * 4 + 3 * d * de * 2 + 2 * tb * d * (xsz + osz) + 4 * tb * de * 4 + tb * d * 4

    def blk(b, be, nu):
        return jnp.minimum(b, nu[0] - 1)

    grid_spec = pltpu.PrefetchScalarGridSpec(
        num_scalar_prefetch=2,
        grid=(nb,),
        in_specs=[pl.BlockSpec((tb, d), lambda b, be, nu: (blk(b, be, nu), 0)),
                  pl.BlockSpec((None, None, d, de), lambda b, be, nu: (layer, be[blk(b, be, nu)], 0, 0)),
                  pl.BlockSpec((None, None, d, de), lambda b, be, nu: (layer, be[blk(b, be, nu)], 0, 0)),
                  pl.BlockSpec((None, None, de, d), lambda b, be, nu: (layer, be[blk(b, be, nu)], 0, 0))],
        out_specs=pl.BlockSpec((tb, d), lambda b, be, nu: (b, 0)),
        scratch_shapes=[pltpu.VMEM((d, de), BF16), pltpu.VMEM((d, de), BF16), pltpu.VMEM((de, d), BF16)],
    )
    return pl.pallas_call(
        _ffn_kernel,
        grid_spec=grid_spec,
        out_shape=jax.ShapeDtypeStruct((p, d), out_dtype),
        compiler_params=_cparams(("arbitrary",), vmem + (4 << 20)),
        name="expert_ffn",
    )(block_expert, n_used, x_rows, wg, wu, wd)


def _combine_kernel(ps_ref, e_cur, r_cur, e_nxt, r_nxt, y_hbm, ew_ref, ysh_ref, x_ref, gate_ref, o_ref, buf, sem,
                    *, tc, nblk):
    i = pl.program_id(0)

    def gather(e_ref, r_ref, bslot):
        def body(t, carry):
            for k in range(MOE_TOP_K):
                slot = ps_ref[e_ref[k, t]] + r_ref[k, t]
                pltpu.make_async_copy(y_hbm.at[pl.ds(slot, 1)],
                                      buf.at[bslot, k, pl.ds(t, 1)], sem.at[bslot]).start()
            return carry

        lax.fori_loop(0, tc, body, 0)

    @pl.when(i == 0)
    def _():
        gather(e_cur, r_cur, 0)

    cur = i % 2

    @pl.when(i + 1 < nblk)
    def _():
        @pl.when(cur == 0)
        def _():
            gather(e_nxt, r_nxt, 1)

        @pl.when(cur == 1)
        def _():
            gather(e_nxt, r_nxt, 0)

    def finish(bslot):
        for k in range(MOE_TOP_K):
            pltpu.make_async_copy(y_hbm.at[pl.ds(0, tc)], buf.at[bslot, k], sem.at[bslot]).wait()
        ew = ew_ref[...]
        acc = ysh_ref[...].astype(F32)
        for k in range(MOE_TOP_K):
            acc = acc + ew[:, k:k + 1] * buf[bslot, k]
        o_ref[...] = x_ref[...] + gate_ref[...] * acc

    @pl.when(cur == 0)
    def _():
        finish(0)

    @pl.when(cur == 1)
    def _():
        finish(1)


def _combine(y, eidx_t, rank_t, pad_start, ew, ysh, x, gate, n_rows):
    d = y.shape[1]
    tc = COMBINE_TILE
    nblk = n_rows // tc
    lat_blocks = SEQ // tc
    row = pl.BlockSpec((tc, d), lambda i, ps: (i, 0))
    cur = pl.BlockSpec((MOE_TOP_K, tc), lambda i, ps: (0, i), memory_space=pltpu.SMEM)
    nxt = pl.BlockSpec((MOE_TOP_K, tc), lambda i, ps: (0, jnp.minimum(i + 1, nblk - 1)), memory_space=pltpu.SMEM)
    vmem = 2 * MOE_TOP_K * tc * d * 4 + 8 * tc * d * 4
    grid_spec = pltpu.PrefetchScalarGridSpec(
        num_scalar_prefetch=1,
        grid=(nblk,),
        in_specs=[cur, cur, nxt, nxt,
                  pl.BlockSpec(memory_space=pl.ANY),
                  pl.BlockSpec((tc, MOE_TOP_K), lambda i, ps: (i, 0)),
                  row, row,
                  pl.BlockSpec((None, 1, d), lambda i, ps: (jnp.where(i >= lat_blocks, 1, 0), 0, 0))],
        out_specs=row,
        scratch_shapes=[pltpu.VMEM((2, MOE_TOP_K, tc, d), F32), pltpu.SemaphoreType.DMA((2,))],
    )
    return pl.pallas_call(
        functools.partial(_combine_kernel, tc=tc, nblk=nblk),
        grid_spec=grid_spec,
        out_shape=jax.ShapeDtypeStruct((n_rows, d), F32),
        compiler_params=_cparams(("arbitrary",), vmem + (4 << 20)),
        name="moe_combine",
    )(pad_start, eidx_t, rank_t, eidx_t, rank_t, y, ew, ysh, x, gate.reshape(2, 1, d))


def _routing_tables(counts, n_rows):
    tb = EXPERT_TILE
    n_blocks = -(-(n_rows * MOE_TOP_K) // tb) + N_EXPERTS
    padded = (counts + tb - 1) // tb * tb
    pad_end = jnp.cumsum(padded)
    pad_start = (pad_end - padded).astype(jnp.int32)
    tail_start = jnp.where(padded > 0, pad_end - tb, -1).astype(jnp.int32)
    blk_start = jnp.arange(n_blocks, dtype=jnp.int32) * tb
    block_expert = jnp.sum(blk_start[:, None] >= pad_end[None, :], axis=1)
    block_expert = jnp.minimum(block_expert, N_EXPERTS - 1).astype(jnp.int32)
    n_used = (pad_end[-1] // tb).astype(jnp.int32).reshape(1)
    return pad_start, tail_start, block_expert, n_used, n_blocks * tb


def _moe(layer, x_new, h, h32, eidx_t, ew_t, rank_t, counts, gate, n_rows, wg, wu, wd, sg, su, sd):
    pad_start, tail_start, block_expert, n_used, n_slots = _routing_tables(counts[:, 0], n_rows)
    xg = _dispatch(h32, eidx_t, rank_t, pad_start, tail_start, n_used, n_rows, n_slots)
    y = _expert_ffn(xg, block_expert, n_used, wg, wu, wd, layer, EXPERT_TILE, F32)
    tm = MM_TM if n_rows % MM_TM == 0 else 512
    ysh = _expert_ffn(h, jnp.zeros((n_rows // tm,), jnp.int32), jnp.full((1,), n_rows // tm, jnp.int32),
                      sg[:, None], su[:, None], sd[:, None], layer, tm, BF16)
    return _combine(y, eidx_t, rank_t, pad_start, ew_t.T, ysh, x_new, gate, n_rows)


def _rope_tables(d, width):
    t = jnp.arange(SEQ)
    row = (t // GRID_W).astype(F32)
    col = (t % GRID_W).astype(F32)
    n = d // 4
    inv = ROPE_THETA ** (-jnp.arange(n, dtype=F32) / n)
    ang = jnp.concatenate([row[:, None] * inv, col[:, None] * inv], axis=-1)
    ang = jnp.concatenate([ang, jnp.zeros((CTX_LEN, d // 2), F32)], axis=0)
    cos = jnp.repeat(jnp.cos(ang), 2, axis=1)
    sin = jnp.repeat(jnp.sin(ang), 2, axis=1) * jnp.tile(jnp.array([-1.0, 1.0], F32), d // 2)
    if width > d:
        cos = jnp.concatenate([cos, jnp.ones((N_TOK, width - d), F32)], axis=1)
        sin = jnp.concatenate([sin, jnp.zeros((N_TOK, width - d), F32)], axis=1)
    return cos, sin


def _mixer_even(h, w_in, rpb, q_norm, w_uq, kv_norm, w_ukv, w_out):
    n_na = 3 * NA_HEADS * HEAD_DIM
    qkv = _matmul(h, w_in[:, :n_na], N_TOK, BF16, MM_TM, 1024)
    w_tail = jnp.pad(w_in[:, n_na:], ((0, 0), (0, 64)))
    lat = _matmul(h, w_tail, N_TOK, F32, MM_TM, 384)
    cos_t, sin_t = _rope_tables(MLA_ROPE, 128)
    nq, nkv, kr = _mla_prep1(lat, q_norm, kv_norm, cos_t, sin_t)
    wq = w_uq.reshape(MLA_Q_RANK, MLA_HEADS, MLA_NOPE + MLA_ROPE)
    wq = jnp.pad(wq, ((0, 0), (0, 0), (0, MLA_QK_PAD - MLA_NOPE - MLA_ROPE))).reshape(MLA_Q_RANK, -1)
    wkv = w_ukv.reshape(MLA_KV_RANK, MLA_HEADS, MLA_NOPE + MLA_V)
    wkv = jnp.concatenate([wkv[:, :, :MLA_NOPE].reshape(MLA_KV_RANK, -1),
                           wkv[:, :, MLA_NOPE:].reshape(MLA_KV_RANK, -1)], axis=1)
    q_raw = _matmul(nq, wq, N_TOK, F32, MM_TM, 1024)
    kv_all = _matmul(nkv, wkv, N_TOK, BF16, MM_TM, 1024)
    q_mla, k_cat = _mla_prep2(q_raw, kv_all, kr, cos_t, sin_t)

    o_na = _na_attention(qkv, _na_bias_tables(rpb))
    mla = dict(n_kv_heads=MLA_HEADS, group=1, dk=MLA_QK_PAD, dv=MLA_V, q_col0=0, k_col0=0, v_col0=MLA_HEADS)
    o_mla = _flash(q_mla, k_cat, kv_all, ctx_queries=False, tq=1024, tk=1024, **mla)
    o_mla_c = _flash(q_mla, k_cat, kv_all, ctx_queries=True, tq=256, tk=512, **mla)
    na = dict(n_kv_heads=NA_HEADS, group=1, dk=HEAD_DIM, dv=HEAD_DIM, q_col0=0, k_col0=NA_HEADS, v_col0=2 * NA_HEADS)
    o_na_c = _flash(qkv, qkv, qkv, ctx_queries=True, tq=256, tk=512, q_scale=HEAD_DIM ** -0.5, **na)
    o = jnp.concatenate([jnp.concatenate([o_na, o_mla], axis=1),
                         jnp.concatenate([o_na_c, o_mla_c], axis=1)], axis=0)
    return _matmul(o, w_out, N_TOK, F32, MM_TM, 1024)


def _mixer_odd_last(h, w_qkv, q_norm, k_norm, w_o):
    qkv = _matmul(h, w_qkv, N_TOK, BF16, MM_TM, 1024)
    cos_t, sin_t = _rope_tables(HEAD_DIM, HEAD_DIM)
    q_g, k_g = _gqa_prep(qkv, q_norm, k_norm, cos_t, sin_t)
    o = _flash(q_g, k_g, qkv, n_kv_heads=GQA_KV, group=GQA_GROUP, dk=HEAD_DIM, dv=HEAD_DIM,
               q_col0=0, k_col0=0, v_col0=GQA_HEADS + GQA_KV, ctx_queries=False, tq=512, tk=1024, row_split=0)
    return _matmul(o, w_o, SEQ, F32, 512, 1024)


def kernel(x, c, ctx, c_ctx, ada_w, ada_b, mix_norm, ffn_norm, ev_w_in, ev_rpb, ev_q_norm, ev_w_uq, ev_kv_norm, ev_w_ukv, ev_w_out, od_w_qkv, od_q_norm, od_k_norm, od_w_o, router_w, router_b, exp_w_gate, exp_w_up, exp_w_down, sh_w_gate, sh_w_up, sh_w_down, final_norm):
    assert DEPTH == 2 and x.shape == (1, SEQ, D_MODEL)
    mods = _adaln_mods(c, c_ctx, ada_w, ada_b)
    xt = jnp.concatenate([x[0], ctx[0]], axis=0)

    def parts(i):
        return [mods[i, :, k * D_MODEL:(k + 1) * D_MODEL] for k in range(6)]

    sh1, sc1, g1, sh2, sc2, g2 = parts(0)
    (h,) = _norm_call(xt, N_TOK, mix_norm[0], mod=(sh1, sc1))
    o = _mixer_even(h, ev_w_in[0], ev_rpb[0], ev_q_norm[0], ev_w_uq[0], ev_kv_norm[0], ev_w_ukv[0], ev_w_out[0])
    xt, h, h32, eidx_t, ew_t, rank_t, counts = _norm_call(xt, N_TOK, ffn_norm[0], resid=(o, g1), mod=(sh2, sc2),
                                                          router=(router_w[0], router_b[0]))
    xt = _moe(0, xt, h, h32, eidx_t, ew_t, rank_t, counts, g2, N_TOK, exp_w_gate, exp_w_up, exp_w_down,
              sh_w_gate, sh_w_up, sh_w_down)

    sh1, sc1, g1, sh2, sc2, g2 = parts(1)
    (h,) = _norm_call(xt, N_TOK, mix_norm[1], mod=(sh1, sc1))
    o = _mixer_odd_last(h, od_w_qkv[0], od_q_norm[0], od_k_norm[0], od_w_o[0])
    xl, h, h32, eidx_t, ew_t, rank_t, counts = _norm_call(xt, SEQ, ffn_norm[1], resid=(o, g1), mod=(sh2, sc2),
                                                          router=(router_w[1], router_b[1]))
    xl = _moe(1, xl, h, h32, eidx_t, ew_t, rank_t, counts, g2, SEQ, exp_w_gate, exp_w_up, exp_w_down,
              sh_w_gate, sh_w_up, sh_w_down)
    (out,) = _norm_call(xl, SEQ, final_norm, out_dtype=F32)
    return out[None]
```

```python
import functools

import numpy as np
import jax
import jax.numpy as jnp
from jax import lax
from jax.experimental import pallas as pl
from jax.experimental.pallas import tpu as pltpu

F32 = jnp.float32
BF16 = jnp.bfloat16

D_MODEL = 2048
SEQ = 8192
DEPTH = 2
GRID_W = 64
GRID_H = SEQ // GRID_W
CTX_LEN = 256
N_TOK = SEQ + CTX_LEN
EPS = 1e-6
ROPE_THETA = 10000.0
HEAD_DIM = 128
NA_HEADS = 8
NA_KH = 8
NA_KW = 16
MLA_HEADS = 8
MLA_Q_RANK = 512
MLA_KV_RANK = 512
MLA_NOPE = 128
MLA_ROPE = 64
MLA_V = 128
MLA_QK_PAD = 256
GQA_HEADS = 16
GQA_KV = 4
GQA_GROUP = GQA_HEADS // GQA_KV
N_EXPERTS = 64
N_GROUPS = 8
GROUP_SIZE = N_EXPERTS // N_GROUPS
TOPK_GROUPS = 4
MOE_TOP_K = 8
D_EXPERT = 512
ROUTED_SCALE = 2.5

NEG = -1e30

VMEM_LIMIT_CAP = 56 * 1024 * 1024

ROW_TILE = 256
MM_TM = 768
NA_RB = 8
NA_U = NA_RB + NA_KH - 1
EXPERT_TILE = 256
COMBINE_TILE = 128
DMA_LOOP_UNROLL = 4


def _cparams(sem, vmem_bytes):
    return pltpu.CompilerParams(
        dimension_semantics=sem,
        vmem_limit_bytes=int(min(max(vmem_bytes, 16 * 1024 * 1024), VMEM_LIMIT_CAP)),
    )


def _dot(a, b):
    return jnp.dot(a, b, preferred_element_type=F32)


def _dot_nt(a, b):
    return lax.dot_general(a, b, (((1,), (1,)), ((), ())), preferred_element_type=F32)


def _mods_kernel(cc_ref, w_ref, b_ref, o_ref, acc_ref, *, nk):
    k = pl.program_id(2)

    @pl.when(k == 0)
    def _():
        acc_ref[...] = jnp.zeros_like(acc_ref)

    a = cc_ref[...]
    a = a * jax.nn.sigmoid(a)
    w = w_ref[...]
    tk, tn = w.shape
    w3 = w.reshape(tk // 8, 8, tn)
    for v in range(2):
        col = a[:, v:v + 1].reshape(tk // 8, 8, 1)
        acc_ref[v] += jnp.sum(w3 * col, axis=0)

    @pl.when(k == nk - 1)
    def _():
        o_ref[...] = jnp.sum(acc_ref[...], axis=1) + b_ref[...]


def _adaln_mods(c, c_ctx, ada_w, ada_b):
    depth, d, n = ada_w.shape
    tk, tn = 512, 1536
    nk = d // tk
    cc = jnp.stack([c[0], c_ctx], axis=1)
    return pl.pallas_call(
        functools.partial(_mods_kernel, nk=nk),
        grid=(depth, n // tn, nk),
        in_specs=[
            pl.BlockSpec((tk, 2), lambda i, j, k: (k, 0)),
            pl.BlockSpec((None, tk, tn), lambda i, j, k: (i, k, j)),
            pl.BlockSpec((None, 1, tn), lambda i, j, k: (i, 0, j)),
        ],
        out_specs=pl.BlockSpec((None, 2, tn), lambda i, j, k: (i, 0, j)),
        out_shape=jax.ShapeDtypeStruct((depth, 2, n), F32),
        scratch_shapes=[pltpu.VMEM((2, 8, tn), F32)],
        compiler_params=_cparams(("arbitrary", "arbitrary", "arbitrary"), 4 * tk * tn * 4),
        name="adaln_mods",
    )(cc, ada_w, ada_b.reshape(depth, 1, n))


def _router_select(hf, rwt_ref, rb_ref, base_ref):
    logits = lax.dot_general(rwt_ref[...], hf, (((1,), (1,)), ((), ())),
                             precision=lax.Precision.HIGHEST, preferred_element_type=F32)
    scores = jax.nn.sigmoid(logits)
    sel = scores + rb_ref[...]
    tm = sel.shape[1]
    ninf = jnp.float32(-jnp.inf)
    sub = lax.broadcasted_iota(jnp.int32, (GROUP_SIZE, tm), 0)
    gi = lax.broadcasted_iota(jnp.int32, (N_GROUPS, tm), 0)
    gs = jnp.zeros((N_GROUPS, tm), F32)
    for g in range(N_GROUPS):
        blk = sel[g * GROUP_SIZE:(g + 1) * GROUP_SIZE, :]
        m1 = jnp.max(blk, axis=0, keepdims=True)
        first = jnp.min(jnp.where(blk == m1, sub, GROUP_SIZE), axis=0, keepdims=True)
        m2 = jnp.max(jnp.where(sub == first, ninf, blk), axis=0, keepdims=True)
        gs = jnp.where(gi == g, m1 + m2, gs)
    gsel = jnp.zeros((N_GROUPS, tm), F32)
    cur = gs
    for _ in range(TOPK_GROUPS):
        mx = jnp.max(cur, axis=0, keepdims=True)
        idx = jnp.min(jnp.where(cur == mx, gi, N_GROUPS), axis=0, keepdims=True)
        pick = gi == idx
        gsel = jnp.where(pick, 1.0, gsel)
        cur = jnp.where(pick, ninf, cur)
    ei = lax.broadcasted_iota(jnp.int32, (N_EXPERTS, tm), 0)
    eg = ei // GROUP_SIZE
    emask = jnp.zeros((N_EXPERTS, tm), F32)
    for g in range(N_GROUPS):
        emask = jnp.where(eg == g, gsel[g:g + 1, :], emask)
    cur = jnp.where(emask > 0.0, sel, ninf)
    idxs, ws, picks = [], [], []
    for _ in range(MOE_TOP_K):
        mx = jnp.max(cur, axis=0, keepdims=True)
        idx = jnp.min(jnp.where(cur == mx, ei, N_EXPERTS), axis=0, keepdims=True)
        pick = ei == idx
        ws.append(jnp.sum(jnp.where(pick, scores, 0.0), axis=0, keepdims=True))
        idxs.append(idx)
        picks.append(pick)
        cur = jnp.where(pick, ninf, cur)
    total = ws[0]
    for w in ws[1:]:
        total = total + w
    ws = [w / total * ROUTED_SCALE for w in ws]
    onehot = jnp.zeros((N_EXPERTS, tm), F32)
    for pick in picks:
        onehot = jnp.where(pick, 1.0, onehot)
    tri = (lax.broadcasted_iota(jnp.int32, (tm, tm), 0) < lax.broadcasted_iota(jnp.int32, (tm, tm), 1))
    before = _dot(onehot.astype(BF16), jnp.where(tri, 1.0, 0.0).astype(BF16)) + base_ref[...]
    ranks = [jnp.sum(jnp.where(pick, before, 0.0), axis=0, keepdims=True).astype(jnp.int32) for pick in picks]
    base_ref[...] += jnp.sum(onehot, axis=1, keepdims=True)
    return idxs, ws, ranks


def _norm_kernel(*refs, has_resid, modulated, with_router, out_dtype):
    it = iter(refs)
    x_ref = next(it)
    if has_resid:
        o_ref = next(it)
        gate_ref = next(it)
    g_ref = next(it)
    if modulated:
        shift_ref = next(it)
        scale_ref = next(it)
    if with_router:
        rwt_ref = next(it)
        rb_ref = next(it)
    if has_resid:
        xn_ref = next(it)
    h_ref = next(it)
    if with_router:
        h32_ref = next(it)
        eidx_ref = next(it)
        ew_ref = next(it)
        rank_ref = next(it)
        cnt_ref = next(it)
        base_ref = next(it)

        @pl.when(pl.program_id(0) == 0)
        def _():
            base_ref[...] = jnp.zeros_like(base_ref)

    x = x_ref[...]
    if has_resid:
        x = x + gate_ref[...] * o_ref[...].astype(F32)
        xn_ref[...] = x
    y = x * lax.rsqrt(jnp.mean(x * x, axis=-1, keepdims=True) + EPS)
    y = y * g_ref[...]
    if modulated:
        y = y * (1.0 + scale_ref[...]) + shift_ref[...]
    h_ref[...] = y.astype(out_dtype)
    if with_router:
        h32_ref[...] = y
        idxs, ws, ranks = _router_select(y, rwt_ref, rb_ref, base_ref)
        for k in range(MOE_TOP_K):
            eidx_ref[k:k + 1, :] = idxs[k]
            ew_ref[k:k + 1, :] = ws[k]
            rank_ref[k:k + 1, :] = ranks[k]
        cnt_ref[...] = base_ref[...].astype(jnp.int32)


def _norm_call(x, n_rows, gain, *, resid=None, mod=None, router=None, out_dtype=BF16):
    d = x.shape[1]
    tm = ROW_TILE
    nblk = n_rows // tm
    lat_blocks = SEQ // tm

    def cls_map(i):
        return (jnp.where(i >= lat_blocks, 1, 0), 0, 0)

    row = pl.BlockSpec((tm, d), lambda i: (i, 0))
    vec = pl.BlockSpec((None, 1, d), cls_map)
    args, in_specs = [x], [row]
    if resid is not None:
        o, gate = resid
        args += [o, gate.reshape(2, 1, d)]
        in_specs += [row, vec]
    args.append(gain.reshape(1, d))
    in_specs.append(pl.BlockSpec((1, d), lambda i: (0, 0)))
    if mod is not None:
        shift, scale = mod
        args += [shift.reshape(2, 1, d), scale.reshape(2, 1, d)]
        in_specs += [vec, vec]
    if router is not None:
        rw, rb = router
        args += [rw.T, rb.reshape(N_EXPERTS, 1)]
        in_specs += [pl.BlockSpec((N_EXPERTS, d), lambda i: (0, 0)),
                     pl.BlockSpec((N_EXPERTS, 1), lambda i: (0, 0))]
    out_shape, out_specs = [], []
    if resid is not None:
        out_shape.append(jax.ShapeDtypeStruct((n_rows, d), F32))
        out_specs.append(row)
    out_shape.append(jax.ShapeDtypeStruct((n_rows, d), out_dtype))
    out_specs.append(row)
    if router is not None:
        out_shape += [jax.ShapeDtypeStruct((n_rows, d), F32),
                      jax.ShapeDtypeStruct((MOE_TOP_K, n_rows), jnp.int32),
                      jax.ShapeDtypeStruct((MOE_TOP_K, n_rows), F32),
                      jax.ShapeDtypeStruct((MOE_TOP_K, n_rows), jnp.int32),
                      jax.ShapeDtypeStruct((N_EXPERTS, 1), jnp.int32)]
        topk = pl.BlockSpec((MOE_TOP_K, tm), lambda i: (0, i))
        out_specs += [row, topk, topk, topk, pl.BlockSpec((N_EXPERTS, 1), lambda i: (0, 0))]
    return pl.pallas_call(
        functools.partial(_norm_kernel, has_resid=resid is not None, modulated=mod is not None,
                          with_router=router is not None, out_dtype=out_dtype),
        grid=(nblk,),
        in_specs=in_specs,
        out_specs=out_specs,
        out_shape=out_shape,
        scratch_shapes=[pltpu.VMEM((N_EXPERTS, 1), F32)] if router is not None else [],
        compiler_params=_cparams(("arbitrary",), 16 * tm * d * 4),
        name="norm_router" if router is not None else "norm",
    )(*args)


def _mm_kernel(a_ref, w_ref, o_ref, wb_ref):
    @pl.when(pl.program_id(1) == 0)
    def _():
        wb_ref[...] = w_ref[...].astype(BF16)

    o_ref[...] = _dot(a_ref[...], wb_ref[...]).astype(o_ref.dtype)


def _matmul(a, w, m_rows, out_dtype, tm, tn):
    k, n = w.shape
    assert m_rows % tm == 0 and n % tn == 0 and a.shape[1] == k
    osz = jnp.dtype(out_dtype).itemsize
    vmem = 2 * k * tn * 4 + k * tn * 2 + 2 * tm * k * 2 + 2 * tm * tn * osz + tm * tn * 4
    return pl.pallas_call(
        _mm_kernel,
        grid=(n // tn, m_rows // tm),
        in_specs=[pl.BlockSpec((tm, k), lambda j, i: (i, 0)),
                  pl.BlockSpec((k, tn), lambda j, i: (0, j))],
        out_specs=pl.BlockSpec((tm, tn), lambda j, i: (i, j)),
        out_shape=jax.ShapeDtypeStruct((m_rows, n), out_dtype),
        scratch_shapes=[pltpu.VMEM((k, tn), BF16)],
        compiler_params=_cparams(("arbitrary", "arbitrary"), vmem + (4 << 20)),
        name="matmul",
    )(a, w)


def _rope(x, cos_e, sin_s):
    w = x.shape[-1]
    lane = lax.broadcasted_iota(jnp.int32, x.shape, 1)
    nxt = pltpu.roll(x, w - 1, 1)
    prv = pltpu.roll(x, 1, 1)
    swapped = jnp.where(lane % 2 == 0, nxt, prv)
    return x * cos_e + swapped * sin_s


def _rms(x, g):
    return x * lax.rsqrt(jnp.mean(x * x, axis=-1, keepdims=True) + EPS) * g


def _mla_prep1_kernel(lat_ref, qg_ref, kvg_ref, cos_ref, sin_ref, nq_ref, nkv_ref, kr_ref):
    lat = lat_ref[...]
    nq_ref[...] = _rms(lat[:, :MLA_Q_RANK], qg_ref[...]).astype(BF16)
    nkv_ref[...] = _rms(lat[:, MLA_Q_RANK:MLA_Q_RANK + MLA_KV_RANK], kvg_ref[...]).astype(BF16)
    kr = lat[:, MLA_Q_RANK + MLA_KV_RANK:]
    kr_ref[...] = _rope(kr, cos_ref[...], sin_ref[...]).astype(BF16)


def _mla_prep1(lat, q_norm, kv_norm, cos_t, sin_t):
    n, wdt = lat.shape
    tm = ROW_TILE
    row = lambda w: pl.BlockSpec((tm, w), lambda i: (i, 0))
    one = lambda w: pl.BlockSpec((1, w), lambda i: (0, 0))
    return pl.pallas_call(
        _mla_prep1_kernel,
        grid=(n // tm,),
        in_specs=[row(wdt), one(MLA_Q_RANK), one(MLA_KV_RANK), row(128), row(128)],
        out_specs=[row(MLA_Q_RANK), row(MLA_KV_RANK), row(128)],
        out_shape=[jax.ShapeDtypeStruct((n, MLA_Q_RANK), BF16),
                   jax.ShapeDtypeStruct((n, MLA_KV_RANK), BF16),
                   jax.ShapeDtypeStruct((n, 128), BF16)],
        compiler_params=_cparams(("arbitrary",), 8 << 20),
        name="mla_prep1",
    )(lat, q_norm.reshape(1, -1), kv_norm.reshape(1, -1), cos_t, sin_t)


def _mla_prep2_kernel(q_ref, kn_ref, kr_ref, cos_ref, sin_ref, qo_ref, ko_ref, *, scale):
    cos_e = cos_ref[...]
    sin_s = sin_ref[...]
    kr = kr_ref[...]
    for h in range(MLA_HEADS):
        lo = h * MLA_QK_PAD
        qo_ref[:, lo:lo + MLA_NOPE] = (q_ref[:, lo:lo + MLA_NOPE].astype(F32) * scale).astype(BF16)
        qr = q_ref[:, lo + MLA_NOPE:lo + MLA_QK_PAD].astype(F32)
        qo_ref[:, lo + MLA_NOPE:lo + MLA_QK_PAD] = (_rope(qr, cos_e, sin_s) * scale).astype(BF16)
        ko_ref[:, lo:lo + MLA_NOPE] = kn_ref[:, h * MLA_NOPE:(h + 1) * MLA_NOPE]
        ko_ref[:, lo + MLA_NOPE:lo + MLA_QK_PAD] = kr


def _mla_prep2(q_raw, kv_all, kr, cos_t, sin_t):
    n = q_raw.shape[0]
    tm = ROW_TILE
    wq = MLA_HEADS * MLA_QK_PAD
    row = lambda w: pl.BlockSpec((tm, w), lambda i: (i, 0))
    return pl.pallas_call(
        functools.partial(_mla_prep2_kernel, scale=(MLA_NOPE + MLA_ROPE) ** -0.5),
        grid=(n // tm,),
        in_specs=[row(wq), row(MLA_HEADS * MLA_NOPE), row(128), row(128), row(128)],
        out_specs=[row(wq), row(wq)],
        out_shape=[jax.ShapeDtypeStruct((n, wq), BF16), jax.ShapeDtypeStruct((n, wq), BF16)],
        compiler_params=_cparams(("arbitrary",), 16 << 20),
        name="mla_prep2",
    )(q_raw, kv_all, kr, cos_t, sin_t)


def _gqa_prep_kernel(qkv_ref, qg_ref, kg_ref, cos_ref, sin_ref, qo_ref, ko_ref, *, scale):
    cos_e = cos_ref[...]
    sin_s = sin_ref[...]
    for h in range(GQA_HEADS):
        xh = qkv_ref[:, h * HEAD_DIM:(h + 1) * HEAD_DIM].astype(F32)
        qo_ref[:, h * HEAD_DIM:(h + 1) * HEAD_DIM] = (
            _rope(_rms(xh, qg_ref[...]), cos_e, sin_s) * scale).astype(BF16)
    for h in range(GQA_KV):
        lo = (GQA_HEADS + h) * HEAD_DIM
        xh = qkv_ref[:, lo:lo + HEAD_DIM].astype(F32)
        ko_ref[:, h * HEAD_DIM:(h + 1) * HEAD_DIM] = _rope(_rms(xh, kg_ref[...]), cos_e, sin_s).astype(BF16)


def _gqa_prep(qkv, q_norm, k_norm, cos_t, sin_t):
    n = qkv.shape[0]
    tm = ROW_TILE
    wqk = (GQA_HEADS + GQA_KV) * HEAD_DIM
    row = lambda w: pl.BlockSpec((tm, w), lambda i: (i, 0))
    one = pl.BlockSpec((1, HEAD_DIM), lambda i: (0, 0))
    return pl.pallas_call(
        functools.partial(_gqa_prep_kernel, scale=HEAD_DIM ** -0.5),
        grid=(n // tm,),
        in_specs=[row(wqk), one, one, row(HEAD_DIM), row(HEAD_DIM)],
        out_specs=[row(GQA_HEADS * HEAD_DIM), row(GQA_KV * HEAD_DIM)],
        out_shape=[jax.ShapeDtypeStruct((n, GQA_HEADS * HEAD_DIM), BF16),
                   jax.ShapeDtypeStruct((n, GQA_KV * HEAD_DIM), BF16)],
        compiler_params=_cparams(("arbitrary",), 16 << 20),
        name="gqa_prep",
    )(qkv, q_norm.reshape(1, -1), k_norm.reshape(1, -1), cos_t, sin_t)


def _flash_kernel(q_ref, k_ref, v_ref, o_ref, *, group, dk, dv, tk, n_lat, q_scale, row_split):
    tq = q_ref.shape[0]
    rs = tq // max(row_split, 1)
    if row_split == 0:
        streams = [None]
        qs = [jnp.concatenate([q_ref[:, g * dk:(g + 1) * dk] for g in range(group)], axis=0)]
    else:
        streams = [(g, r) for g in range(group) for r in range(row_split)]
        qs = [q_ref[r * rs:(r + 1) * rs, g * dk:(g + 1) * dk] for g, r in streams]
    if q_scale is not None:
        qs = [(q.astype(F32) * q_scale).astype(BF16) for q in qs]

    def first(q):
        s = _dot_nt(q, k_ref[SEQ:SEQ + CTX_LEN, :])
        m = jnp.max(s, axis=-1, keepdims=True)
        p = jnp.exp(s - m)
        l = jnp.sum(p, axis=-1, keepdims=True)
        return m, l, _dot(p.astype(BF16), v_ref[SEQ:SEQ + CTX_LEN, :])

    def step(q, kb, vb, m, l, acc):
        s = _dot_nt(q, kb)
        m_new = jnp.maximum(m, jnp.max(s, axis=-1, keepdims=True))
        alpha = jnp.exp(m - m_new)
        p = jnp.exp(s - m_new)
        l = alpha * l + jnp.sum(p, axis=-1, keepdims=True)
        acc = alpha * acc + _dot(p.astype(BF16), vb)
        return m_new, l, acc

    def body(j, carry):
        off = pl.multiple_of(j * tk, tk)
        kb = k_ref[pl.ds(off, tk), :]
        vb = v_ref[pl.ds(off, tk), :]
        return tuple(step(q, kb, vb, *c) for q, c in zip(qs, carry))

    carry = tuple(first(q) for q in qs)
    if n_lat:
        carry = lax.fori_loop(0, n_lat, body, carry)
    if row_split == 0:
        m, l, acc = carry[0]
        o = (acc / l).astype(o_ref.dtype)
        for g in range(group):
            o_ref[:, g * dv:(g + 1) * dv] = o[g * tq:(g + 1) * tq]
    else:
        for (g, r), (m, l, acc) in zip(streams, carry):
            o_ref[r * rs:(r + 1) * rs, g * dv:(g + 1) * dv] = (acc / l).astype(o_ref.dtype)


def _flash(q_arr, k_arr, v_arr, *, n_kv_heads, group, dk, dv, q_col0, k_col0, v_col0, ctx_queries, tq, tk,
           q_scale=None, row_split=1):
    n_tok = k_arr.shape[0]
    if ctx_queries:
        n_q, q_blk0, n_lat = CTX_LEN, SEQ // tq, 0
    else:
        n_q, q_blk0, n_lat = SEQ, 0, SEQ // tk
    m = group * tq
    vmem = (2 * n_tok * (dk + dv) * 2 + 4 * tq * group * (dk + dv) * 2
            + 3 * m * max(tk, CTX_LEN) * 4 + 4 * m * dv * 4)
    return pl.pallas_call(
        functools.partial(_flash_kernel, group=group, dk=dk, dv=dv, tk=tk, n_lat=n_lat, q_scale=q_scale,
                          row_split=row_split),
        grid=(n_kv_heads, n_q // tq),
        in_specs=[pl.BlockSpec((tq, group * dk), lambda h, i: (q_blk0 + i, q_col0 + h)),
                  pl.BlockSpec((n_tok, dk), lambda h, i: (0, k_col0 + h)),
                  pl.BlockSpec((n_tok, dv), lambda h, i: (0, v_col0 + h))],
        out_specs=pl.BlockSpec((tq, group * dv), lambda h, i: (i, h)),
        out_shape=jax.ShapeDtypeStruct((n_q, n_kv_heads * group * dv), BF16),
        compiler_params=_cparams(("arbitrary", "arbitrary"), vmem + (8 << 20)),
        name="flash_ctx" if ctx_queries else "flash",
    )(q_arr, k_arr, v_arr)


def _na_start_row(b):
    return jnp.clip(NA_RB * b - NA_KH // 2, 0, GRID_H - NA_U)


def _na_kernel(q_ref, k_ref, v_ref, bias_ref, o_ref, *, scale):
    b = pl.program_id(1)
    start = pl.multiple_of(_na_start_row(b) * GRID_W, GRID_W)
    nwin = NA_U * GRID_W
    q = (q_ref[...].astype(F32) * scale).astype(BF16)
    s_loc = _dot_nt(q, k_ref[pl.ds(start, nwin), :]) + bias_ref[...]
    s_ctx = _dot_nt(q, k_ref[SEQ:SEQ + CTX_LEN, :])
    m = jnp.maximum(jnp.max(s_loc, axis=-1, keepdims=True), jnp.max(s_ctx, axis=-1, keepdims=True))
    p_loc = jnp.exp(s_loc - m)
    p_ctx = jnp.exp(s_ctx - m)
    l = jnp.sum(p_loc, axis=-1, keepdims=True) + jnp.sum(p_ctx, axis=-1, keepdims=True)
    o = _dot(p_loc.astype(BF16), v_ref[pl.ds(start, nwin), :]) + _dot(p_ctx.astype(BF16), v_ref[SEQ:SEQ + CTX_LEN, :])
    o_ref[...] = (o / l).astype(o_ref.dtype)


def _na_bias_tables(rpb):
    h = rpb.shape[0]
    w = np.arange(GRID_W)
    c0 = np.clip(w - NA_KW // 2, 0, GRID_W - NA_KW)
    colmask = (w[None, :] >= c0[:, None]) & (w[None, :] < c0[:, None] + NA_KW)
    colidx = np.clip(w[None, :] - w[:, None] + NA_KW - 1, 0, 2 * NA_KW - 2)
    slabs = jnp.where(colmask[None, None], rpb[:, :, colidx], NEG)
    slabs = jnp.concatenate([slabs, jnp.full((h, 1, GRID_W, GRID_W), NEG, rpb.dtype)], axis=1)
    masked = 2 * NA_KH - 1
    nb = GRID_H // NA_RB
    tabs = []
    for b in (0, 1, nb - 1):
        start = int(np.clip(NA_RB * b - NA_KH // 2, 0, GRID_H - NA_U))
        a = np.full((NA_RB, NA_U), masked, np.int32)
        for j in range(NA_RB):
            r = NA_RB * b + j
            r0 = int(np.clip(r - NA_KH // 2, 0, GRID_H - NA_KH))
            for u in range(NA_U):
                kr = start + u
                if r0 <= kr < r0 + NA_KH:
                    a[j, u] = kr - r + NA_KH - 1
        t = slabs[:, a]
        tabs.append(t.transpose(0, 1, 3, 2, 4).reshape(h, NA_RB * GRID_W, NA_U * GRID_W))
    return jnp.stack(tabs, axis=1)


def _na_attention(qkv, bias_tabs):
    n_tok = qkv.shape[0]
    nb = GRID_H // NA_RB
    tq = NA_RB * GRID_W
    nwin = NA_U * GRID_W

    def cls(b):
        return jnp.where(b == 0, 0, jnp.where(b == nb - 1, 2, 1))

    vmem = 4 * n_tok * HEAD_DIM * 2 + 2 * tq * nwin * 4 + 4 * tq * (nwin + CTX_LEN) * 4
    return pl.pallas_call(
        functools.partial(_na_kernel, scale=HEAD_DIM ** -0.5),
        grid=(NA_HEADS, nb),
        in_specs=[pl.BlockSpec((tq, HEAD_DIM), lambda h, b: (b, h)),
                  pl.BlockSpec((n_tok, HEAD_DIM), lambda h, b: (0, NA_HEADS + h)),
                  pl.BlockSpec((n_tok, HEAD_DIM), lambda h, b: (0, 2 * NA_HEADS + h)),
                  pl.BlockSpec((None, None, tq, nwin), lambda h, b: (h, cls(b), 0, 0))],
        out_specs=pl.BlockSpec((tq, HEAD_DIM), lambda h, b: (b, h)),
        out_shape=jax.ShapeDtypeStruct((SEQ, NA_HEADS * HEAD_DIM), BF16),
        compiler_params=_cparams(("arbitrary", "arbitrary"), vmem + (8 << 20)),
        name="na_attention",
    )(qkv, qkv, qkv, bias_tabs)


def _dispatch_kernel(tail_ref, nu_ref, slot_ref, h_ref, xg_hbm, zbuf, sem, zsem, *, tb, n_blocks):
    te = EXPERT_TILE

    @pl.when(pl.program_id(0) == 0)
    def _():
        zbuf[...] = jnp.zeros_like(zbuf)

        def zero_tile(row0, wait):
            cp = pltpu.make_async_copy(zbuf, xg_hbm.at[pl.ds(pl.multiple_of(row0, te), te)], zsem)
            if wait:
                cp.wait()
            else:
                cp.start()

        def zero_tail(e, wait):
            @pl.when(tail_ref[e] >= 0)
            def _():
                zero_tile(tail_ref[e], wait)

        for wait in (False, True):
            lax.fori_loop(0, N_EXPERTS, lambda e, c: (zero_tail(e, wait), c)[1], 0)
            lax.fori_loop(nu_ref[0], n_blocks, lambda b, c: (zero_tile(b * te, wait), c)[1], 0)

    def body(t, carry):
        for k in range(MOE_TOP_K):
            pltpu.make_async_copy(h_ref.at[pl.ds(t, 1)], xg_hbm.at[pl.ds(slot_ref[k, t], 1)], sem).start()
        return carry

    lax.fori_loop(0, tb, body, 0, unroll=DMA_LOOP_UNROLL)
    for k in range(MOE_TOP_K):
        pltpu.make_async_copy(h_ref, xg_hbm.at[pl.ds(0, tb)], sem).wait()


def _dispatch(h32, slot_t, tail_start, n_used, n_rows, n_slots):
    d = h32.shape[1]
    tb = ROW_TILE
    grid_spec = pltpu.PrefetchScalarGridSpec(
        num_scalar_prefetch=2,
        grid=(n_rows // tb,),
        in_specs=[pl.BlockSpec((MOE_TOP_K, tb), lambda i, tl, nu: (0, i), memory_space=pltpu.SMEM),
                  pl.BlockSpec((tb, d), lambda i, tl, nu: (i, 0))],
        out_specs=pl.BlockSpec(memory_space=pl.ANY),
        scratch_shapes=[pltpu.VMEM((EXPERT_TILE, d), F32), pltpu.SemaphoreType.DMA(()),
                        pltpu.SemaphoreType.DMA(())],
    )
    return pl.pallas_call(
        functools.partial(_dispatch_kernel, tb=tb, n_blocks=n_slots // EXPERT_TILE),
        grid_spec=grid_spec,
        out_shape=jax.ShapeDtypeStruct((n_slots, d), F32),
        compiler_params=_cparams(("arbitrary",), 16 << 20),
        name="moe_dispatch",
    )(tail_start, n_used, slot_t, h32)


def _ffn_kernel(be_ref, nu_ref, x_ref, wg_ref, wu_ref, wd_ref, o_ref, wgb, wub, wdb):
    b = pl.program_id(0)
    prev = be_ref[jnp.maximum(b - 1, 0)]
    new_expert = jnp.logical_or(b == 0, be_ref[b] != prev)

    @pl.when(jnp.logical_and(new_expert, b < nu_ref[0]))
    def _():
        wgb[...] = wg_ref[...].astype(BF16)
        wub[...] = wu_ref[...].astype(BF16)
        wdb[...] = wd_ref[...].astype(BF16)

    @pl.when(b < nu_ref[0])
    def _():
        x = x_ref[...].astype(BF16)
        g = _dot(x, wgb[...])
        u = _dot(x, wub[...])
        a = (g * jax.nn.sigmoid(g) * u).astype(BF16)
        o_ref[...] = _dot(a, wdb[...]).astype(o_ref.dtype)

    @pl.when(b >= nu_ref[0])
    def _():
        o_ref[...] = jnp.zeros_like(o_ref)


def _expert_ffn(x_rows, block_expert, n_used, wg, wu, wd, layer, tb, out_dtype):
    p, d = x_rows.shape
    de = wg.shape[-1]
    nb = p // tb
    xsz = x_rows.dtype.itemsize
    osz = jnp.dtype(out_dtype).itemsize
    vmem = 2 * 3 * d * de * 4 + 3 * d * de * 2 + 2 * tb * d * (xsz + osz) + 4 * tb * de * 4 + tb * d * 4

    def blk(b, be, nu):
        return jnp.minimum(b, nu[0] - 1)

    grid_spec = pltpu.PrefetchScalarGridSpec(
        num_scalar_prefetch=2,
        grid=(nb,),
        in_specs=[pl.BlockSpec((tb, d), lambda b, be, nu: (blk(b, be, nu), 0)),
                  pl.BlockSpec((None, None, d, de), lambda b, be, nu: (layer, be[blk(b, be, nu)], 0, 0)),
                  pl.BlockSpec((None, None, d, de), lambda b, be, nu: (layer, be[blk(b, be, nu)], 0, 0)),
                  pl.BlockSpec((None, None, de, d), lambda b, be, nu: (layer, be[blk(b, be, nu)], 0, 0))],
        out_specs=pl.BlockSpec((tb, d), lambda b, be, nu: (b, 0)),
        scratch_shapes=[pltpu.VMEM((d, de), BF16), pltpu.VMEM((d, de), BF16), pltpu.VMEM((de, d), BF16)],
    )
    return pl.pallas_call(
        _ffn_kernel,
        grid_spec=grid_spec,
        out_shape=jax.ShapeDtypeStruct((p, d), out_dtype),
        compiler_params=_cparams(("arbitrary",), vmem + (4 << 20)),
        name="expert_ffn",
    )(block_expert, n_used, x_rows, wg, wu, wd)


def _combine_kernel(s_cur, s_nxt, y_hbm, ew_ref, ysh_ref, x_ref, gate_ref, o_ref, buf, sem, *, tc, nblk):
    i = pl.program_id(0)

    def gather(slot_ref, bslot):
        def body(t, carry):
            for k in range(MOE_TOP_K):
                pltpu.make_async_copy(y_hbm.at[pl.ds(slot_ref[k, t], 1)],
                                      buf.at[bslot, k, pl.ds(t, 1)], sem.at[bslot]).start()
            return carry

        lax.fori_loop(0, tc, body, 0, unroll=DMA_LOOP_UNROLL)

    @pl.when(i == 0)
    def _():
        gather(s_cur, 0)

    cur = i % 2

    @pl.when(i + 1 < nblk)
    def _():
        @pl.when(cur == 0)
        def _():
            gather(s_nxt, 1)

        @pl.when(cur == 1)
        def _():
            gather(s_nxt, 0)

    def finish(bslot):
        for k in range(MOE_TOP_K):
            pltpu.make_async_copy(y_hbm.at[pl.ds(0, tc)], buf.at[bslot, k], sem.at[bslot]).wait()
        ew = ew_ref[...]
        acc = ysh_ref[...].astype(F32)
        for k in range(MOE_TOP_K):
            acc = acc + ew[:, k:k + 1] * buf[bslot, k]
        o_ref[...] = x_ref[...] + gate_ref[...] * acc

    @pl.when(cur == 0)
    def _():
        finish(0)

    @pl.when(cur == 1)
    def _():
        finish(1)


def _combine(y, slot_t, ew, ysh, x, gate, n_rows):
    d = y.shape[1]
    tc = COMBINE_TILE
    nblk = n_rows // tc
    lat_blocks = SEQ // tc
    row = pl.BlockSpec((tc, d), lambda i: (i, 0))
    vmem = 2 * MOE_TOP_K * tc * d * 4 + 8 * tc * d * 4
    return pl.pallas_call(
        functools.partial(_combine_kernel, tc=tc, nblk=nblk),
        grid=(nblk,),
        in_specs=[pl.BlockSpec((MOE_TOP_K, tc), lambda i: (0, i), memory_space=pltpu.SMEM),
                  pl.BlockSpec((MOE_TOP_K, tc), lambda i: (0, jnp.minimum(i + 1, nblk - 1)),
                               memory_space=pltpu.SMEM),
                  pl.BlockSpec(memory_space=pl.ANY),
                  pl.BlockSpec((tc, MOE_TOP_K), lambda i: (i, 0)),
                  row, row,
                  pl.BlockSpec((None, 1, d), lambda i: (jnp.where(i >= lat_blocks, 1, 0), 0, 0))],
        out_specs=row,
        out_shape=jax.ShapeDtypeStruct((n_rows, d), F32),
        scratch_shapes=[pltpu.VMEM((2, MOE_TOP_K, tc, d), F32), pltpu.SemaphoreType.DMA((2,))],
        compiler_params=_cparams(("arbitrary",), vmem + (4 << 20)),
        name="moe_combine",
    )(slot_t, slot_t, y, ew, ysh, x, gate.reshape(2, 1, d))


def _routing_tables(counts, n_rows):
    tb = EXPERT_TILE
    n_blocks = -(-(n_rows * MOE_TOP_K) // tb) + N_EXPERTS
    padded = (counts + tb - 1) // tb * tb
    pad_end = jnp.cumsum(padded)
    pad_start = (pad_end - padded).astype(jnp.int32)
    tail_start = jnp.where(padded > 0, pad_end - tb, -1).astype(jnp.int32)
    blk_start = jnp.arange(n_blocks, dtype=jnp.int32) * tb
    block_expert = jnp.sum(blk_start[:, None] >= pad_end[None, :], axis=1)
    block_expert = jnp.minimum(block_expert, N_EXPERTS - 1).astype(jnp.int32)
    n_used = (pad_end[-1] // tb).astype(jnp.int32).reshape(1)
    return pad_start, tail_start, block_expert, n_used, n_blocks * tb


def _moe(layer, x_new, h, h32, eidx_t, ew_t, rank_t, counts, gate, n_rows, wg, wu, wd, sg, su, sd):
    pad_start, tail_start, block_expert, n_used, n_slots = _routing_tables(counts[:, 0], n_rows)
    e_iota = jnp.arange(N_EXPERTS, dtype=jnp.int32)
    slot_t = jnp.sum(jnp.where(eidx_t[:, :, None] == e_iota, pad_start, 0), axis=-1) + rank_t
    xg = _dispatch(h32, slot_t, tail_start, n_used, n_rows, n_slots)
    y = _expert_ffn(xg, block_expert, n_used, wg, wu, wd, layer, EXPERT_TILE, F32)
    tm = MM_TM if n_rows % MM_TM == 0 else 512
    ysh = _expert_ffn(h, jnp.zeros((n_rows // tm,), jnp.int32), jnp.full((1,), n_rows // tm, jnp.int32),
                      sg[:, None], su[:, None], sd[:, None], layer, tm, BF16)
    return _combine(y, slot_t, ew_t.T, ysh, x_new, gate, n_rows)


def _rope_tables(d, width):
    t = jnp.arange(SEQ)
    row = (t // GRID_W).astype(F32)
    col = (t % GRID_W).astype(F32)
    n = d // 4
    inv = ROPE_THETA ** (-jnp.arange(n, dtype=F32) / n)
    ang = jnp.concatenate([row[:, None] * inv, col[:, None] * inv], axis=-1)
    ang = jnp.concatenate([ang, jnp.zeros((CTX_LEN, d // 2), F32)], axis=0)
    cos = jnp.repeat(jnp.cos(ang), 2, axis=1)
    sin = jnp.repeat(jnp.sin(ang), 2, axis=1) * jnp.tile(jnp.array([-1.0, 1.0], F32), d // 2)
    if width > d:
        cos = jnp.concatenate([cos, jnp.ones((N_TOK, width - d), F32)], axis=1)
        sin = jnp.concatenate([sin, jnp.zeros((N_TOK, width - d), F32)], axis=1)
    return cos, sin


def _mixer_even(h, w_in, rpb, q_norm, w_uq, kv_norm, w_ukv, w_out):
    n_na = 3 * NA_HEADS * HEAD_DIM
    qkv = _matmul(h, w_in[:, :n_na], N_TOK, BF16, MM_TM, 1024)
    w_tail = jnp.pad(w_in[:, n_na:], ((0, 0), (0, 64)))
    lat = _matmul(h, w_tail, N_TOK, F32, MM_TM, 384)
    cos_t, sin_t = _rope_tables(MLA_ROPE, 128)
    nq, nkv, kr = _mla_prep1(lat, q_norm, kv_norm, cos_t, sin_t)
    wq = w_uq.reshape(MLA_Q_RANK, MLA_HEADS, MLA_NOPE + MLA_ROPE)
    wq = jnp.pad(wq, ((0, 0), (0, 0), (0, MLA_QK_PAD - MLA_NOPE - MLA_ROPE))).reshape(MLA_Q_RANK, -1)
    wkv = w_ukv.reshape(MLA_KV_RANK, MLA_HEADS, MLA_NOPE + MLA_V)
    wkv = jnp.concatenate([wkv[:, :, :MLA_NOPE].reshape(MLA_KV_RANK, -1),
                           wkv[:, :, MLA_NOPE:].reshape(MLA_KV_RANK, -1)], axis=1)
    q_raw = _matmul(nq, wq, N_TOK, F32, MM_TM, 1024)
    kv_all = _matmul(nkv, wkv, N_TOK, BF16, MM_TM, 1024)
    q_mla, k_cat = _mla_prep2(q_raw, kv_all, kr, cos_t, sin_t)

    o_na = _na_attention(qkv, _na_bias_tables(rpb))
    mla = dict(n_kv_heads=MLA_HEADS, group=1, dk=MLA_QK_PAD, dv=MLA_V, q_col0=0, k_col0=0, v_col0=MLA_HEADS)
    o_mla = _flash(q_mla, k_cat, kv_all, ctx_queries=False, tq=1024, tk=1024, **mla)
    o_mla_c = _flash(q_mla, k_cat, kv_all, ctx_queries=True, tq=256, tk=512, **mla)
    na = dict(n_kv_heads=NA_HEADS, group=1, dk=HEAD_DIM, dv=HEAD_DIM, q_col0=0, k_col0=NA_HEADS, v_col0=2 * NA_HEADS)
    o_na_c = _flash(qkv, qkv, qkv, ctx_queries=True, tq=256, tk=512, q_scale=HEAD_DIM ** -0.5, **na)
    o = jnp.concatenate([jnp.concatenate([o_na, o_mla], axis=1),
                         jnp.concatenate([o_na_c, o_mla_c], axis=1)], axis=0)
    return _matmul(o, w_out, N_TOK, F32, MM_TM, 1024)


def _mixer_odd_last(h, w_qkv, q_norm, k_norm, w_o):
    qkv = _matmul(h, w_qkv, N_TOK, BF16, MM_TM, 1024)
    cos_t, sin_t = _rope_tables(HEAD_DIM, HEAD_DIM)
    q_g, k_g = _gqa_prep(qkv, q_norm, k_norm, cos_t, sin_t)
    o = _flash(q_g, k_g, qkv, n_kv_heads=GQA_KV, group=GQA_GROUP, dk=HEAD_DIM, dv=HEAD_DIM,
               q_col0=0, k_col0=0, v_col0=GQA_HEADS + GQA_KV, ctx_queries=False, tq=512, tk=1024, row_split=0)
    return _matmul(o, w_o, SEQ, F32, 512, 1024)


def kernel(x, c, ctx, c_ctx, ada_w, ada_b, mix_norm, ffn_norm, ev_w_in, ev_rpb, ev_q_norm, ev_w_uq, ev_kv_norm, ev_w_ukv, ev_w_out, od_w_qkv, od_q_norm, od_k_norm, od_w_o, router_w, router_b, exp_w_gate, exp_w_up, exp_w_down, sh_w_gate, sh_w_up, sh_w_down, final_norm):
    assert DEPTH == 2 and x.shape == (1, SEQ, D_MODEL)
    mods = _adaln_mods(c, c_ctx, ada_w, ada_b)
    xt = jnp.concatenate([x[0], ctx[0]], axis=0)

    def parts(i):
        return [mods[i, :, k * D_MODEL:(k + 1) * D_MODEL] for k in range(6)]

    sh1, sc1, g1, sh2, sc2, g2 = parts(0)
    (h,) = _norm_call(xt, N_TOK, mix_norm[0], mod=(sh1, sc1))
    o = _mixer_even(h, ev_w_in[0], ev_rpb[0], ev_q_norm[0], ev_w_uq[0], ev_kv_norm[0], ev_w_ukv[0], ev_w_out[0])
    xt, h, h32, eidx_t, ew_t, rank_t, counts = _norm_call(xt, N_TOK, ffn_norm[0], resid=(o, g1), mod=(sh2, sc2),
                                                          router=(router_w[0], router_b[0]))
    xt = _moe(0, xt, h, h32, eidx_t, ew_t, rank_t, counts, g2, N_TOK, exp_w_gate, exp_w_up, exp_w_down,
              sh_w_gate, sh_w_up, sh_w_down)

    sh1, sc1, g1, sh2, sc2, g2 = parts(1)
    (h,) = _norm_call(xt, N_TOK, mix_norm[1], mod=(sh1, sc1))
    o = _mixer_odd_last(h, od_w_qkv[0], od_q_norm[0], od_k_norm[0], od_w_o[0])
    xl, h, h32, eidx_t, ew_t, rank_t, counts = _norm_call(xt, SEQ, ffn_norm[1], resid=(o, g1), mod=(sh2, sc2),
                                                          router=(router_w[1], router_b[1]))
    xl = _moe(1, xl, h, h32, eidx_t, ew_t, rank_t, counts, g2, SEQ, exp_w_gate, exp_w_up, exp_w_down,
              sh_w_gate, sh_w_up, sh_w_down)
    (out,) = _norm_call(xl, SEQ, final_norm, out_dtype=F32)
    return out[None]
```

```python
import functools

import numpy as np
import jax
import jax.numpy as jnp
from jax import lax
from jax.experimental import pallas as pl
from jax.experimental.pallas import tpu as pltpu

F32 = jnp.float32
BF16 = jnp.bfloat16

D_MODEL = 2048
SEQ = 8192
DEPTH = 2
GRID_W = 64
GRID_H = SEQ // GRID_W
CTX_LEN = 256
N_TOK = SEQ + CTX_LEN
EPS = 1e-6
ROPE_THETA = 10000.0
HEAD_DIM = 128
NA_HEADS = 8
NA_KH = 8
NA_KW = 16
MLA_HEADS = 8
MLA_Q_RANK = 512
MLA_KV_RANK = 512
MLA_NOPE = 128
MLA_ROPE = 64
MLA_V = 128
MLA_QK_PAD = 256
GQA_HEADS = 16
GQA_KV = 4
GQA_GROUP = GQA_HEADS // GQA_KV
N_EXPERTS = 64
N_GROUPS = 8
GROUP_SIZE = N_EXPERTS // N_GROUPS
TOPK_GROUPS = 4
MOE_TOP_K = 8
D_EXPERT = 512
ROUTED_SCALE = 2.5

NEG = -1e30
LOG2E = 1.4426950408889634

VMEM_LIMIT_CAP = 56 * 1024 * 1024

ROW_TILE = 256
MM_TM = 768
NA_RB = 8
NA_U = NA_RB + NA_KH - 1
EXPERT_TILE = 256
COMBINE_TILE = 128
DMA_LOOP_UNROLL = 4


def _cparams(sem, vmem_bytes):
    return pltpu.CompilerParams(
        dimension_semantics=sem,
        vmem_limit_bytes=int(min(max(vmem_bytes, 16 * 1024 * 1024), VMEM_LIMIT_CAP)),
    )


def _pack_bf16_pairs(x):
    w = x.shape[1] // 2

    def rounded(v):
        b = lax.bitcast_convert_type(v, jnp.uint32)
        return b + (((b >> 16) & jnp.uint32(1)) + jnp.uint32(0x7FFF))

    return (rounded(x[:, :w]) >> 16) | (rounded(x[:, w:]) & jnp.uint32(0xFFFF0000))


def _unpack_bf16_pairs(p):
    lo = lax.bitcast_convert_type(p << 16, F32)
    hi = lax.bitcast_convert_type(p & jnp.uint32(0xFFFF0000), F32)
    return jnp.concatenate([lo, hi], axis=1)


def _dot(a, b):
    return jnp.dot(a, b, preferred_element_type=F32)


def _dot_nt(a, b):
    return lax.dot_general(a, b, (((1,), (1,)), ((), ())), preferred_element_type=F32)


def _mods_kernel(cc_ref, w_ref, b_ref, o_ref, acc_ref, *, nk):
    k = pl.program_id(2)

    @pl.when(k == 0)
    def _():
        acc_ref[...] = jnp.zeros_like(acc_ref)

    a = cc_ref[...]
    a = a * jax.nn.sigmoid(a)
    w = w_ref[...]
    tk, tn = w.shape
    w3 = w.reshape(tk // 8, 8, tn)
    for v in range(2):
        col = a[:, v:v + 1].reshape(tk // 8, 8, 1)
        acc_ref[v] += jnp.sum(w3 * col, axis=0)

    @pl.when(k == nk - 1)
    def _():
        o_ref[...] = jnp.sum(acc_ref[...], axis=1) + b_ref[...]


def _adaln_mods(c, c_ctx, ada_w, ada_b):
    depth, d, n = ada_w.shape
    tk, tn = 512, 1536
    nk = d // tk
    cc = jnp.stack([c[0], c_ctx], axis=1)
    return pl.pallas_call(
        functools.partial(_mods_kernel, nk=nk),
        grid=(depth, n // tn, nk),
        in_specs=[
            pl.BlockSpec((tk, 2), lambda i, j, k: (k, 0)),
            pl.BlockSpec((None, tk, tn), lambda i, j, k: (i, k, j)),
            pl.BlockSpec((None, 1, tn), lambda i, j, k: (i, 0, j)),
        ],
        out_specs=pl.BlockSpec((None, 2, tn), lambda i, j, k: (i, 0, j)),
        out_shape=jax.ShapeDtypeStruct((depth, 2, n), F32),
        scratch_shapes=[pltpu.VMEM((2, 8, tn), F32)],
        compiler_params=_cparams(("arbitrary", "arbitrary", "arbitrary"), 4 * tk * tn * 4),
        name="adaln_mods",
    )(cc, ada_w, ada_b.reshape(depth, 1, n))


def _router_select(hf, rwt_ref, rb_ref, base_ref):
    logits = lax.dot_general(rwt_ref[...], hf, (((1,), (1,)), ((), ())),
                             precision=lax.Precision.HIGHEST, preferred_element_type=F32)
    scores = jax.nn.sigmoid(logits)
    sel = scores + rb_ref[...]
    tm = sel.shape[1]
    ninf = jnp.float32(-jnp.inf)
    sub = lax.broadcasted_iota(jnp.int32, (GROUP_SIZE, tm), 0)
    gi = lax.broadcasted_iota(jnp.int32, (N_GROUPS, tm), 0)
    gs = jnp.zeros((N_GROUPS, tm), F32)
    for g in range(N_GROUPS):
        blk = sel[g * GROUP_SIZE:(g + 1) * GROUP_SIZE, :]
        m1 = jnp.max(blk, axis=0, keepdims=True)
        first = jnp.min(jnp.where(blk == m1, sub, GROUP_SIZE), axis=0, keepdims=True)
        m2 = jnp.max(jnp.where(sub == first, ninf, blk), axis=0, keepdims=True)
        gs = jnp.where(gi == g, m1 + m2, gs)
    gsel = jnp.zeros((N_GROUPS, tm), F32)
    cur = gs
    for _ in range(TOPK_GROUPS):
        mx = jnp.max(cur, axis=0, keepdims=True)
        idx = jnp.min(jnp.where(cur == mx, gi, N_GROUPS), axis=0, keepdims=True)
        pick = gi == idx
        gsel = jnp.where(pick, 1.0, gsel)
        cur = jnp.where(pick, ninf, cur)
    ei = lax.broadcasted_iota(jnp.int32, (N_EXPERTS, tm), 0)
    eg = ei // GROUP_SIZE
    emask = jnp.zeros((N_EXPERTS, tm), F32)
    for g in range(N_GROUPS):
        emask = jnp.where(eg == g, gsel[g:g + 1, :], emask)
    cur = jnp.where(emask > 0.0, sel, ninf)
    idxs, ws, picks = [], [], []
    for _ in range(MOE_TOP_K):
        mx = jnp.max(cur, axis=0, keepdims=True)
        idx = jnp.min(jnp.where(cur == mx, ei, N_EXPERTS), axis=0, keepdims=True)
        pick = ei == idx
        ws.append(jnp.sum(jnp.where(pick, scores, 0.0), axis=0, keepdims=True))
        idxs.append(idx)
        picks.append(pick)
        cur = jnp.where(pick, ninf, cur)
    total = ws[0]
    for w in ws[1:]:
        total = total + w
    ws = [w / total * ROUTED_SCALE for w in ws]
    onehot = jnp.zeros((N_EXPERTS, tm), F32)
    for pick in picks:
        onehot = jnp.where(pick, 1.0, onehot)
    tri = (lax.broadcasted_iota(jnp.int32, (tm, tm), 0) < lax.broadcasted_iota(jnp.int32, (tm, tm), 1))
    before = _dot(onehot.astype(BF16), jnp.where(tri, 1.0, 0.0).astype(BF16)) + base_ref[...]
    ranks = [jnp.sum(jnp.where(pick, before, 0.0), axis=0, keepdims=True).astype(jnp.int32) for pick in picks]
    base_ref[...] += jnp.sum(onehot, axis=1, keepdims=True)
    return idxs, ws, ranks


def _norm_kernel(*refs, has_resid, modulated, with_router, out_dtype):
    it = iter(refs)
    x_ref = next(it)
    if has_resid:
        o_ref = next(it)
        gate_ref = next(it)
    g_ref = next(it)
    if modulated:
        shift_ref = next(it)
        scale_ref = next(it)
    if with_router:
        rwt_ref = next(it)
        rb_ref = next(it)
    if has_resid:
        xn_ref = next(it)
    h_ref = next(it)
    if with_router:
        hp_ref = next(it)
        eidx_ref = next(it)
        ew_ref = next(it)
        rank_ref = next(it)
        cnt_ref = next(it)
        base_ref = next(it)

        @pl.when(pl.program_id(0) == 0)
        def _():
            base_ref[...] = jnp.zeros_like(base_ref)

    x = x_ref[...]
    if has_resid:
        x = x + gate_ref[...] * o_ref[...].astype(F32)
        xn_ref[...] = x
    y = x * lax.rsqrt(jnp.mean(x * x, axis=-1, keepdims=True) + EPS)
    y = y * g_ref[...]
    if modulated:
        y = y * (1.0 + scale_ref[...]) + shift_ref[...]
    h_ref[...] = y.astype(out_dtype)
    if with_router:
        hp_ref[...] = _pack_bf16_pairs(y)
        idxs, ws, ranks = _router_select(y, rwt_ref, rb_ref, base_ref)
        for k in range(MOE_TOP_K):
            eidx_ref[k:k + 1, :] = idxs[k]
            ew_ref[k:k + 1, :] = ws[k]
            rank_ref[k:k + 1, :] = ranks[k]
        cnt_ref[...] = base_ref[...].astype(jnp.int32)


def _norm_call(x, n_rows, gain, *, resid=None, mod=None, router=None, out_dtype=BF16):
    d = x.shape[1]
    tm = ROW_TILE
    nblk = n_rows // tm
    lat_blocks = SEQ // tm

    def cls_map(i):
        return (jnp.where(i >= lat_blocks, 1, 0), 0, 0)

    row = pl.BlockSpec((tm, d), lambda i: (i, 0))
    vec = pl.BlockSpec((None, 1, d), cls_map)
    args, in_specs = [x], [row]
    if resid is not None:
        o, gate = resid
        args += [o, gate.reshape(2, 1, d)]
        in_specs += [row, vec]
    args.append(gain.reshape(1, d))
    in_specs.append(pl.BlockSpec((1, d), lambda i: (0, 0)))
    if mod is not None:
        shift, scale = mod
        args += [shift.reshape(2, 1, d), scale.reshape(2, 1, d)]
        in_specs += [vec, vec]
    if router is not None:
        rw, rb = router
        args += [rw.T, rb.reshape(N_EXPERTS, 1)]
        in_specs += [pl.BlockSpec((N_EXPERTS, d), lambda i: (0, 0)),
                     pl.BlockSpec((N_EXPERTS, 1), lambda i: (0, 0))]
    out_shape, out_specs = [], []
    if resid is not None:
        out_shape.append(jax.ShapeDtypeStruct((n_rows, d), F32))
        out_specs.append(row)
    out_shape.append(jax.ShapeDtypeStruct((n_rows, d), out_dtype))
    out_specs.append(row)
    if router is not None:
        out_shape += [jax.ShapeDtypeStruct((n_rows, d // 2), jnp.uint32),
                      jax.ShapeDtypeStruct((MOE_TOP_K, n_rows), jnp.int32),
                      jax.ShapeDtypeStruct((MOE_TOP_K, n_rows), F32),
                      jax.ShapeDtypeStruct((MOE_TOP_K, n_rows), jnp.int32),
                      jax.ShapeDtypeStruct((N_EXPERTS, 1), jnp.int32)]
        topk = pl.BlockSpec((MOE_TOP_K, tm), lambda i: (0, i))
        out_specs += [pl.BlockSpec((tm, d // 2), lambda i: (i, 0)), topk, topk, topk,
                      pl.BlockSpec((N_EXPERTS, 1), lambda i: (0, 0))]
    return pl.pallas_call(
        functools.partial(_norm_kernel, has_resid=resid is not None, modulated=mod is not None,
                          with_router=router is not None, out_dtype=out_dtype),
        grid=(nblk,),
        in_specs=in_specs,
        out_specs=out_specs,
        out_shape=out_shape,
        scratch_shapes=[pltpu.VMEM((N_EXPERTS, 1), F32)] if router is not None else [],
        compiler_params=_cparams(("arbitrary",), 16 * tm * d * 4),
        name="norm_router" if router is not None else "norm",
    )(*args)


def _mm_kernel(a_ref, w_ref, o_ref, wb_ref):
    @pl.when(pl.program_id(1) == 0)
    def _():
        wb_ref[...] = w_ref[...].astype(BF16)

    o_ref[...] = _dot(a_ref[...], wb_ref[...]).astype(o_ref.dtype)


def _matmul(a, w, m_rows, out_dtype, tm, tn):
    k, n = w.shape
    assert m_rows % tm == 0 and n % tn == 0 and a.shape[1] == k
    osz = jnp.dtype(out_dtype).itemsize
    vmem = 2 * k * tn * 4 + k * tn * 2 + 2 * tm * k * 2 + 2 * tm * tn * osz + tm * tn * 4
    return pl.pallas_call(
        _mm_kernel,
        grid=(n // tn, m_rows // tm),
        in_specs=[pl.BlockSpec((tm, k), lambda j, i: (i, 0)),
                  pl.BlockSpec((k, tn), lambda j, i: (0, j))],
        out_specs=pl.BlockSpec((tm, tn), lambda j, i: (i, j)),
        out_shape=jax.ShapeDtypeStruct((m_rows, n), out_dtype),
        scratch_shapes=[pltpu.VMEM((k, tn), BF16)],
        compiler_params=_cparams(("arbitrary", "arbitrary"), vmem + (4 << 20)),
        name="matmul",
    )(a, w)


def _rope(x, cos_e, sin_s):
    w = x.shape[-1]
    lane = lax.broadcasted_iota(jnp.int32, x.shape, 1)
    nxt = pltpu.roll(x, w - 1, 1)
    prv = pltpu.roll(x, 1, 1)
    swapped = jnp.where(lane % 2 == 0, nxt, prv)
    return x * cos_e + swapped * sin_s


def _rms(x, g):
    return x * lax.rsqrt(jnp.mean(x * x, axis=-1, keepdims=True) + EPS) * g


def _mla_prep1_kernel(lat_ref, qg_ref, kvg_ref, cos_ref, sin_ref, nq_ref, nkv_ref, kr_ref):
    lat = lat_ref[...]
    nq_ref[...] = _rms(lat[:, :MLA_Q_RANK], qg_ref[...]).astype(BF16)
    nkv_ref[...] = _rms(lat[:, MLA_Q_RANK:MLA_Q_RANK + MLA_KV_RANK], kvg_ref[...]).astype(BF16)
    kr = lat[:, MLA_Q_RANK + MLA_KV_RANK:]
    kr_ref[...] = _rope(kr, cos_ref[...], sin_ref[...]).astype(BF16)


def _mla_prep1(lat, q_norm, kv_norm, cos_t, sin_t):
    n, wdt = lat.shape
    tm = ROW_TILE
    row = lambda w: pl.BlockSpec((tm, w), lambda i: (i, 0))
    one = lambda w: pl.BlockSpec((1, w), lambda i: (0, 0))
    return pl.pallas_call(
        _mla_prep1_kernel,
        grid=(n // tm,),
        in_specs=[row(wdt), one(MLA_Q_RANK), one(MLA_KV_RANK), row(128), row(128)],
        out_specs=[row(MLA_Q_RANK), row(MLA_KV_RANK), row(128)],
        out_shape=[jax.ShapeDtypeStruct((n, MLA_Q_RANK), BF16),
                   jax.ShapeDtypeStruct((n, MLA_KV_RANK), BF16),
                   jax.ShapeDtypeStruct((n, 128), BF16)],
        compiler_params=_cparams(("arbitrary",), 8 << 20),
        name="mla_prep1",
    )(lat, q_norm.reshape(1, -1), kv_norm.reshape(1, -1), cos_t, sin_t)


def _mla_prep2_kernel(q_ref, kn_ref, kr_ref, cos_ref, sin_ref, qo_ref, ko_ref, *, scale):
    cos_e = cos_ref[...]
    sin_s = sin_ref[...]
    kr = kr_ref[...]
    for h in range(MLA_HEADS):
        lo = h * MLA_QK_PAD
        qo_ref[:, lo:lo + MLA_NOPE] = (q_ref[:, lo:lo + MLA_NOPE].astype(F32) * scale).astype(BF16)
        qr = q_ref[:, lo + MLA_NOPE:lo + MLA_QK_PAD].astype(F32)
        qo_ref[:, lo + MLA_NOPE:lo + MLA_QK_PAD] = (_rope(qr, cos_e, sin_s) * scale).astype(BF16)
        ko_ref[:, lo:lo + MLA_NOPE] = kn_ref[:, h * MLA_NOPE:(h + 1) * MLA_NOPE]
        ko_ref[:, lo + MLA_NOPE:lo + MLA_QK_PAD] = kr


def _mla_prep2(q_raw, kv_all, kr, cos_t, sin_t):
    n = q_raw.shape[0]
    tm = ROW_TILE
    wq = MLA_HEADS * MLA_QK_PAD
    row = lambda w: pl.BlockSpec((tm, w), lambda i: (i, 0))
    return pl.pallas_call(
        functools.partial(_mla_prep2_kernel, scale=(MLA_NOPE + MLA_ROPE) ** -0.5 * LOG2E),
        grid=(n // tm,),
        in_specs=[row(wq), row(MLA_HEADS * MLA_NOPE), row(128), row(128), row(128)],
        out_specs=[row(wq), row(wq)],
        out_shape=[jax.ShapeDtypeStruct((n, wq), BF16), jax.ShapeDtypeStruct((n, wq), BF16)],
        compiler_params=_cparams(("arbitrary",), 16 << 20),
        name="mla_prep2",
    )(q_raw, kv_all, kr, cos_t, sin_t)


def _gqa_prep_kernel(qkv_ref, qg_ref, kg_ref, cos_ref, sin_ref, qo_ref, ko_ref, *, scale):
    cos_e = cos_ref[...]
    sin_s = sin_ref[...]
    for h in range(GQA_HEADS):
        xh = qkv_ref[:, h * HEAD_DIM:(h + 1) * HEAD_DIM].astype(F32)
        qo_ref[:, h * HEAD_DIM:(h + 1) * HEAD_DIM] = (
            _rope(_rms(xh, qg_ref[...]), cos_e, sin_s) * scale).astype(BF16)
    for h in range(GQA_KV):
        lo = (GQA_HEADS + h) * HEAD_DIM
        xh = qkv_ref[:, lo:lo + HEAD_DIM].astype(F32)
        ko_ref[:, h * HEAD_DIM:(h + 1) * HEAD_DIM] = _rope(_rms(xh, kg_ref[...]), cos_e, sin_s).astype(BF16)


def _gqa_prep(qkv, q_norm, k_norm, cos_t, sin_t):
    n = qkv.shape[0]
    tm = ROW_TILE
    wqk = (GQA_HEADS + GQA_KV) * HEAD_DIM
    row = lambda w: pl.BlockSpec((tm, w), lambda i: (i, 0))
    one = pl.BlockSpec((1, HEAD_DIM), lambda i: (0, 0))
    return pl.pallas_call(
        functools.partial(_gqa_prep_kernel, scale=HEAD_DIM ** -0.5 * LOG2E),
        grid=(n // tm,),
        in_specs=[row(wqk), one, one, row(HEAD_DIM), row(HEAD_DIM)],
        out_specs=[row(GQA_HEADS * HEAD_DIM), row(GQA_KV * HEAD_DIM)],
        out_shape=[jax.ShapeDtypeStruct((n, GQA_HEADS * HEAD_DIM), BF16),
                   jax.ShapeDtypeStruct((n, GQA_KV * HEAD_DIM), BF16)],
        compiler_params=_cparams(("arbitrary",), 16 << 20),
        name="gqa_prep",
    )(qkv, q_norm.reshape(1, -1), k_norm.reshape(1, -1), cos_t, sin_t)


def _flash_kernel(q_ref, k_ref, v_ref, o_ref, *, group, dk, dv, tk, n_lat, q_scale, row_split):
    tq = q_ref.shape[0]
    rs = tq // max(row_split, 1)
    if row_split == 0:
        streams = [None]
        qs = [jnp.concatenate([q_ref[:, g * dk:(g + 1) * dk] for g in range(group)], axis=0)]
    else:
        streams = [(g, r) for g in range(group) for r in range(row_split)]
        qs = [q_ref[r * rs:(r + 1) * rs, g * dk:(g + 1) * dk] for g, r in streams]
    if q_scale is not None:
        qs = [(q.astype(F32) * q_scale).astype(BF16) for q in qs]

    def first(q):
        s = _dot_nt(q, k_ref[SEQ:SEQ + CTX_LEN, :])
        m = jnp.max(s, axis=-1, keepdims=True)
        p = jnp.exp2(s - m)
        l = jnp.sum(p, axis=-1, keepdims=True)
        return m, l, _dot(p.astype(BF16), v_ref[SEQ:SEQ + CTX_LEN, :])

    def step(q, kb, vb, m, l, acc):
        s = _dot_nt(q, kb)
        m_new = jnp.maximum(m, jnp.max(s, axis=-1, keepdims=True))
        alpha = jnp.exp2(m - m_new)
        p = jnp.exp2(s - m_new)
        l = alpha * l + jnp.sum(p, axis=-1, keepdims=True)
        acc = alpha * acc + _dot(p.astype(BF16), vb)
        return m_new, l, acc

    def body(j, carry):
        off = pl.multiple_of(j * tk, tk)
        kb = k_ref[pl.ds(off, tk), :]
        vb = v_ref[pl.ds(off, tk), :]
        return tuple(step(q, kb, vb, *c) for q, c in zip(qs, carry))

    carry = tuple(first(q) for q in qs)
    if n_lat:
        carry = lax.fori_loop(0, n_lat, body, carry)
    if row_split == 0:
        m, l, acc = carry[0]
        o = (acc / l).astype(o_ref.dtype)
        for g in range(group):
            o_ref[:, g * dv:(g + 1) * dv] = o[g * tq:(g + 1) * tq]
    else:
        for (g, r), (m, l, acc) in zip(streams, carry):
            o_ref[r * rs:(r + 1) * rs, g * dv:(g + 1) * dv] = (acc / l).astype(o_ref.dtype)


def _flash(q_arr, k_arr, v_arr, *, n_kv_heads, group, dk, dv, q_col0, k_col0, v_col0, ctx_queries, tq, tk,
           q_scale=None, row_split=1):
    n_tok = k_arr.shape[0]
    if ctx_queries:
        n_q, q_blk0, n_lat = CTX_LEN, SEQ // tq, 0
    else:
        n_q, q_blk0, n_lat = SEQ, 0, SEQ // tk
    m = group * tq
    vmem = (2 * n_tok * (dk + dv) * 2 + 4 * tq * group * (dk + dv) * 2
            + 3 * m * max(tk, CTX_LEN) * 4 + 4 * m * dv * 4)
    return pl.pallas_call(
        functools.partial(_flash_kernel, group=group, dk=dk, dv=dv, tk=tk, n_lat=n_lat, q_scale=q_scale,
                          row_split=row_split),
        grid=(n_kv_heads, n_q // tq),
        in_specs=[pl.BlockSpec((tq, group * dk), lambda h, i: (q_blk0 + i, q_col0 + h)),
                  pl.BlockSpec((n_tok, dk), lambda h, i: (0, k_col0 + h)),
                  pl.BlockSpec((n_tok, dv), lambda h, i: (0, v_col0 + h))],
        out_specs=pl.BlockSpec((tq, group * dv), lambda h, i: (i, h)),
        out_shape=jax.ShapeDtypeStruct((n_q, n_kv_heads * group * dv), BF16),
        compiler_params=_cparams(("arbitrary", "arbitrary"), vmem + (8 << 20)),
        name="flash_ctx" if ctx_queries else "flash",
    )(q_arr, k_arr, v_arr)


def _na_start_row(b):
    return jnp.clip(NA_RB * b - NA_KH // 2, 0, GRID_H - NA_U)


def _na_kernel(q_ref, k_ref, v_ref, bias_ref, o_ref, *, scale):
    b = pl.program_id(1)
    start = pl.multiple_of(_na_start_row(b) * GRID_W, GRID_W)
    nwin = NA_U * GRID_W
    q = (q_ref[...].astype(F32) * scale).astype(BF16)
    s_loc = _dot_nt(q, k_ref[pl.ds(start, nwin), :]) + bias_ref[...]
    s_ctx = _dot_nt(q, k_ref[SEQ:SEQ + CTX_LEN, :])
    m = jnp.maximum(jnp.max(s_loc, axis=-1, keepdims=True), jnp.max(s_ctx, axis=-1, keepdims=True))
    p_loc = jnp.exp(s_loc - m)
    p_ctx = jnp.exp(s_ctx - m)
    l = jnp.sum(p_loc, axis=-1, keepdims=True) + jnp.sum(p_ctx, axis=-1, keepdims=True)
    o = _dot(p_loc.astype(BF16), v_ref[pl.ds(start, nwin), :]) + _dot(p_ctx.astype(BF16), v_ref[SEQ:SEQ + CTX_LEN, :])
    o_ref[...] = (o / l).astype(o_ref.dtype)


def _na_bias_tables(rpb):
    h = rpb.shape[0]
    w = np.arange(GRID_W)
    c0 = np.clip(w - NA_KW // 2, 0, GRID_W - NA_KW)
    colmask = (w[None, :] >= c0[:, None]) & (w[None, :] < c0[:, None] + NA_KW)
    colidx = np.clip(w[None, :] - w[:, None] + NA_KW - 1, 0, 2 * NA_KW - 2)
    slabs = jnp.where(colmask[None, None], rpb[:, :, colidx], NEG)
    slabs = jnp.concatenate([slabs, jnp.full((h, 1, GRID_W, GRID_W), NEG, rpb.dtype)], axis=1)
    masked = 2 * NA_KH - 1
    nb = GRID_H // NA_RB
    tabs = []
    for b in (0, 1, nb - 1):
        start = int(np.clip(NA_RB * b - NA_KH // 2, 0, GRID_H - NA_U))
        a = np.full((NA_RB, NA_U), masked, np.int32)
        for j in range(NA_RB):
            r = NA_RB * b + j
            r0 = int(np.clip(r - NA_KH // 2, 0, GRID_H - NA_KH))
            for u in range(NA_U):
                kr = start + u
                if r0 <= kr < r0 + NA_KH:
                    a[j, u] = kr - r + NA_KH - 1
        t = slabs[:, a]
        tabs.append(t.transpose(0, 1, 3, 2, 4).reshape(h, NA_RB * GRID_W, NA_U * GRID_W))
    return jnp.stack(tabs, axis=1)


def _na_attention(qkv, bias_tabs):
    n_tok = qkv.shape[0]
    nb = GRID_H // NA_RB
    tq = NA_RB * GRID_W
    nwin = NA_U * GRID_W

    def cls(b):
        return jnp.where(b == 0, 0, jnp.where(b == nb - 1, 2, 1))

    vmem = 4 * n_tok * HEAD_DIM * 2 + 2 * tq * nwin * 4 + 4 * tq * (nwin + CTX_LEN) * 4
    return pl.pallas_call(
        functools.partial(_na_kernel, scale=HEAD_DIM ** -0.5),
        grid=(NA_HEADS, nb),
        in_specs=[pl.BlockSpec((tq, HEAD_DIM), lambda h, b: (b, h)),
                  pl.BlockSpec((n_tok, HEAD_DIM), lambda h, b: (0, NA_HEADS + h)),
                  pl.BlockSpec((n_tok, HEAD_DIM), lambda h, b: (0, 2 * NA_HEADS + h)),
                  pl.BlockSpec((None, None, tq, nwin), lambda h, b: (h, cls(b), 0, 0))],
        out_specs=pl.BlockSpec((tq, HEAD_DIM), lambda h, b: (b, h)),
        out_shape=jax.ShapeDtypeStruct((SEQ, NA_HEADS * HEAD_DIM), BF16),
        compiler_params=_cparams(("arbitrary", "arbitrary"), vmem + (8 << 20)),
        name="na_attention",
    )(qkv, qkv, qkv, bias_tabs)


def _dispatch_kernel(tail_ref, nu_ref, slot_ref, h_ref, xg_hbm, zbuf, sem, zsem, *, tb, n_blocks):
    te = EXPERT_TILE

    @pl.when(pl.program_id(0) == 0)
    def _():
        zbuf[...] = jnp.zeros_like(zbuf)

        def zero_tile(row0, wait):
            cp = pltpu.make_async_copy(zbuf, xg_hbm.at[pl.ds(pl.multiple_of(row0, te), te)], zsem)
            if wait:
                cp.wait()
            else:
                cp.start()

        def zero_tail(e, wait):
            @pl.when(tail_ref[e] >= 0)
            def _():
                zero_tile(tail_ref[e], wait)

        for wait in (False, True):
            lax.fori_loop(0, N_EXPERTS, lambda e, c: (zero_tail(e, wait), c)[1], 0)
            lax.fori_loop(nu_ref[0], n_blocks, lambda b, c: (zero_tile(b * te, wait), c)[1], 0)

    def body(t, carry):
        for k in range(MOE_TOP_K):
            pltpu.make_async_copy(h_ref.at[pl.ds(t, 1)], xg_hbm.at[pl.ds(slot_ref[k, t], 1)], sem).start()
        return carry

    lax.fori_loop(0, tb, body, 0, unroll=DMA_LOOP_UNROLL)
    for k in range(MOE_TOP_K):
        pltpu.make_async_copy(h_ref, xg_hbm.at[pl.ds(0, tb)], sem).wait()


def _dispatch(h32, slot_t, tail_start, n_used, n_rows, n_slots):
    d = h32.shape[1]
    tb = ROW_TILE
    grid_spec = pltpu.PrefetchScalarGridSpec(
        num_scalar_prefetch=2,
        grid=(n_rows // tb,),
        in_specs=[pl.BlockSpec((MOE_TOP_K, tb), lambda i, tl, nu: (0, i), memory_space=pltpu.SMEM),
                  pl.BlockSpec((tb, d), lambda i, tl, nu: (i, 0))],
        out_specs=pl.BlockSpec(memory_space=pl.ANY),
        scratch_shapes=[pltpu.VMEM((EXPERT_TILE, d), h32.dtype), pltpu.SemaphoreType.DMA(()),
                        pltpu.SemaphoreType.DMA(())],
    )
    return pl.pallas_call(
        functools.partial(_dispatch_kernel, tb=tb, n_blocks=n_slots // EXPERT_TILE),
        grid_spec=grid_spec,
        out_shape=jax.ShapeDtypeStruct((n_slots, d), h32.dtype),
        compiler_params=_cparams(("arbitrary",), 16 << 20),
        name="moe_dispatch",
    )(tail_start, n_used, slot_t, h32)


def _ffn_kernel(be_ref, nu_ref, x_ref, wg_ref, wu_ref, wd_ref, o_ref, wgb, wub, wdb, *, packed):
    b = pl.program_id(0)
    prev = be_ref[jnp.maximum(b - 1, 0)]
    new_expert = jnp.logical_or(b == 0, be_ref[b] != prev)

    @pl.when(jnp.logical_and(new_expert, b < nu_ref[0]))
    def _():
        wgb[...] = wg_ref[...].astype(BF16)
        wub[...] = wu_ref[...].astype(BF16)
        wdb[...] = wd_ref[...].astype(BF16)

    @pl.when(b < nu_ref[0])
    def _():
        x = _unpack_bf16_pairs(x_ref[...]).astype(BF16) if packed else x_ref[...]
        g = _dot(x, wgb[...])
        u = _dot(x, wub[...])
        a = (g * jax.nn.sigmoid(g) * u).astype(BF16)
        y = _dot(a, wdb[...])
        o_ref[...] = _pack_bf16_pairs(y) if packed else y.astype(o_ref.dtype)

    @pl.when(b >= nu_ref[0])
    def _():
        o_ref[...] = jnp.zeros_like(o_ref)


def _expert_ffn(x_rows, block_expert, n_used, wg, wu, wd, layer, tb, packed):
    p, dx = x_rows.shape
    d, de = wg.shape[-2:]
    nb = p // tb
    out_dtype = x_rows.dtype
    vmem = 2 * 3 * d * de * 4 + 3 * d * de * 2 + 4 * tb * d * 2 + 4 * tb * de * 4 + 3 * tb * d * 4

    def blk(b, be, nu):
        return jnp.minimum(b, nu[0] - 1)

    grid_spec = pltpu.PrefetchScalarGridSpec(
        num_scalar_prefetch=2,
        grid=(nb,),
        in_specs=[pl.BlockSpec((tb, dx), lambda b, be, nu: (blk(b, be, nu), 0)),
                  pl.BlockSpec((None, None, d, de), lambda b, be, nu: (layer, be[blk(b, be, nu)], 0, 0)),
                  pl.BlockSpec((None, None, d, de), lambda b, be, nu: (layer, be[blk(b, be, nu)], 0, 0)),
                  pl.BlockSpec((None, None, de, d), lambda b, be, nu: (layer, be[blk(b, be, nu)], 0, 0))],
        out_specs=pl.BlockSpec((tb, dx), lambda b, be, nu: (b, 0)),
        scratch_shapes=[pltpu.VMEM((d, de), BF16), pltpu.VMEM((d, de), BF16), pltpu.VMEM((de, d), BF16)],
    )
    return pl.pallas_call(
        functools.partial(_ffn_kernel, packed=packed),
        grid_spec=grid_spec,
        out_shape=jax.ShapeDtypeStruct((p, dx), out_dtype),
        compiler_params=_cparams(("arbitrary",), vmem + (4 << 20)),
        name="expert_ffn",
    )(block_expert, n_used, x_rows, wg, wu, wd)


def _combine_kernel(s_cur, s_nxt, y_hbm, ew_ref, ysh_ref, x_ref, gate_ref, o_ref, buf, sem, *, tc, nblk):
    i = pl.program_id(0)

    def gather(slot_ref, bslot):
        def body(t, carry):
            for k in range(MOE_TOP_K):
                pltpu.make_async_copy(y_hbm.at[pl.ds(slot_ref[k, t], 1)],
                                      buf.at[bslot, k, pl.ds(t, 1)], sem.at[bslot]).start()
            return carry

        lax.fori_loop(0, tc, body, 0, unroll=DMA_LOOP_UNROLL)

    @pl.when(i == 0)
    def _():
        gather(s_cur, 0)

    cur = i % 2

    @pl.when(i + 1 < nblk)
    def _():
        @pl.when(cur == 0)
        def _():
            gather(s_nxt, 1)

        @pl.when(cur == 1)
        def _():
            gather(s_nxt, 0)

    def finish(bslot):
        for k in range(MOE_TOP_K):
            pltpu.make_async_copy(y_hbm.at[pl.ds(0, tc)], buf.at[bslot, k], sem.at[bslot]).wait()
        ew = ew_ref[...]
        acc = ysh_ref[...].astype(F32)
        for k in range(MOE_TOP_K):
            acc = acc + ew[:, k:k + 1] * _unpack_bf16_pairs(buf[bslot, k])
        o_ref[...] = x_ref[...] + gate_ref[...] * acc

    @pl.when(cur == 0)
    def _():
        finish(0)

    @pl.when(cur == 1)
    def _():
        finish(1)


def _combine(y, slot_t, ew, ysh, x, gate, n_rows):
    d = x.shape[1]
    tc = COMBINE_TILE
    nblk = n_rows // tc
    lat_blocks = SEQ // tc
    row = pl.BlockSpec((tc, d), lambda i: (i, 0))
    vmem = 2 * MOE_TOP_K * tc * d * 4 + 8 * tc * d * 4
    return pl.pallas_call(
        functools.partial(_combine_kernel, tc=tc, nblk=nblk),
        grid=(nblk,),
        in_specs=[pl.BlockSpec((MOE_TOP_K, tc), lambda i: (0, i), memory_space=pltpu.SMEM),
                  pl.BlockSpec((MOE_TOP_K, tc), lambda i: (0, jnp.minimum(i + 1, nblk - 1)),
                               memory_space=pltpu.SMEM),
                  pl.BlockSpec(memory_space=pl.ANY),
                  pl.BlockSpec((tc, MOE_TOP_K), lambda i: (i, 0)),
                  row, row,
                  pl.BlockSpec((None, 1, d), lambda i: (jnp.where(i >= lat_blocks, 1, 0), 0, 0))],
        out_specs=row,
        out_shape=jax.ShapeDtypeStruct((n_rows, d), F32),
        scratch_shapes=[pltpu.VMEM((2, MOE_TOP_K, tc, d // 2), jnp.uint32), pltpu.SemaphoreType.DMA((2,))],
        compiler_params=_cparams(("arbitrary",), vmem + (4 << 20)),
        name="moe_combine",
    )(slot_t, slot_t, y, ew, ysh, x, gate.reshape(2, 1, d))


def _routing_tables(counts, n_rows):
    tb = EXPERT_TILE
    n_blocks = -(-(n_rows * MOE_TOP_K) // tb) + N_EXPERTS
    padded = (counts + tb - 1) // tb * tb
    pad_end = jnp.cumsum(padded)
    pad_start = (pad_end - padded).astype(jnp.int32)
    tail_start = jnp.where(padded > 0, pad_end - tb, -1).astype(jnp.int32)
    blk_start = jnp.arange(n_blocks, dtype=jnp.int32) * tb
    block_expert = jnp.sum(blk_start[:, None] >= pad_end[None, :], axis=1)
    block_expert = jnp.minimum(block_expert, N_EXPERTS - 1).astype(jnp.int32)
    n_used = (pad_end[-1] // tb).astype(jnp.int32).reshape(1)
    return pad_start, tail_start, block_expert, n_used, n_blocks * tb


def _moe(layer, x_new, h, h32, eidx_t, ew_t, rank_t, counts, gate, n_rows, wg, wu, wd, sg, su, sd):
    pad_start, tail_start, block_expert, n_used, n_slots = _routing_tables(counts[:, 0], n_rows)
    e_iota = jnp.arange(N_EXPERTS, dtype=jnp.int32)
    slot_t = jnp.sum(jnp.where(eidx_t[:, :, None] == e_iota, pad_start, 0), axis=-1) + rank_t
    xg = _dispatch(h32, slot_t, tail_start, n_used, n_rows, n_slots)
    y = _expert_ffn(xg, block_expert, n_used, wg, wu, wd, layer, EXPERT_TILE, True)
    tm = MM_TM if n_rows % MM_TM == 0 else 512
    ysh = _expert_ffn(h, jnp.zeros((n_rows // tm,), jnp.int32), jnp.full((1,), n_rows // tm, jnp.int32),
                      sg[:, None], su[:, None], sd[:, None], layer, tm, False)
    return _combine(y, slot_t, ew_t.T, ysh, x_new, gate, n_rows)


def _rope_tables(d, width):
    t = jnp.arange(SEQ)
    row = (t // GRID_W).astype(F32)
    col = (t % GRID_W).astype(F32)
    n = d // 4
    inv = ROPE_THETA ** (-jnp.arange(n, dtype=F32) / n)
    ang = jnp.concatenate([row[:, None] * inv, col[:, None] * inv], axis=-1)
    ang = jnp.concatenate([ang, jnp.zeros((CTX_LEN, d // 2), F32)], axis=0)
    cos = jnp.repeat(jnp.cos(ang), 2, axis=1)
    sin = jnp.repeat(jnp.sin(ang), 2, axis=1) * jnp.tile(jnp.array([-1.0, 1.0], F32), d // 2)
    if width > d:
        cos = jnp.concatenate([cos, jnp.ones((N_TOK, width - d), F32)], axis=1)
        sin = jnp.concatenate([sin, jnp.zeros((N_TOK, width - d), F32)], axis=1)
    return cos, sin


def _mixer_even(h, w_in, rpb, q_norm, w_uq, kv_norm, w_ukv, w_out):
    n_na = 3 * NA_HEADS * HEAD_DIM
    qkv = _matmul(h, w_in[:, :n_na], N_TOK, BF16, MM_TM, 1024)
    w_tail = jnp.pad(w_in[:, n_na:], ((0, 0), (0, 64)))
    lat = _matmul(h, w_tail, N_TOK, F32, MM_TM, 384)
    cos_t, sin_t = _rope_tables(MLA_ROPE, 128)
    nq, nkv, kr = _mla_prep1(lat, q_norm, kv_norm, cos_t, sin_t)
    wq = w_uq.reshape(MLA_Q_RANK, MLA_HEADS, MLA_NOPE + MLA_ROPE)
    wq = jnp.pad(wq, ((0, 0), (0, 0), (0, MLA_QK_PAD - MLA_NOPE - MLA_ROPE))).reshape(MLA_Q_RANK, -1)
    wkv = w_ukv.reshape(MLA_KV_RANK, MLA_HEADS, MLA_NOPE + MLA_V)
    wkv = jnp.concatenate([wkv[:, :, :MLA_NOPE].reshape(MLA_KV_RANK, -1),
                           wkv[:, :, MLA_NOPE:].reshape(MLA_KV_RANK, -1)], axis=1)
    q_raw = _matmul(nq, wq, N_TOK, F32, MM_TM, 1024)
    kv_all = _matmul(nkv, wkv, N_TOK, BF16, MM_TM, 1024)
    q_mla, k_cat = _mla_prep2(q_raw, kv_all, kr, cos_t, sin_t)

    o_na = _na_attention(qkv, _na_bias_tables(rpb))
    mla = dict(n_kv_heads=MLA_HEADS, group=1, dk=MLA_QK_PAD, dv=MLA_V, q_col0=0, k_col0=0, v_col0=MLA_HEADS)
    o_mla = _flash(q_mla, k_cat, kv_all, ctx_queries=False, tq=1024, tk=1024, **mla)
    o_mla_c = _flash(q_mla, k_cat, kv_all, ctx_queries=True, tq=256, tk=512, **mla)
    na = dict(n_kv_heads=NA_HEADS, group=1, dk=HEAD_DIM, dv=HEAD_DIM, q_col0=0, k_col0=NA_HEADS, v_col0=2 * NA_HEADS)
    o_na_c = _flash(qkv, qkv, qkv, ctx_queries=True, tq=256, tk=512, q_scale=HEAD_DIM ** -0.5 * LOG2E, **na)
    o = jnp.concatenate([jnp.concatenate([o_na, o_mla], axis=1),
                         jnp.concatenate([o_na_c, o_mla_c], axis=1)], axis=0)
    return _matmul(o, w_out, N_TOK, F32, MM_TM, 1024)


def _mixer_odd_last(h, w_qkv, q_norm, k_norm, w_o):
    qkv = _matmul(h, w_qkv, N_TOK, BF16, MM_TM, 1024)
    cos_t, sin_t = _rope_tables(HEAD_DIM, HEAD_DIM)
    q_g, k_g = _gqa_prep(qkv, q_norm, k_norm, cos_t, sin_t)
    o = _flash(q_g, k_g, qkv, n_kv_heads=GQA_KV, group=GQA_GROUP, dk=HEAD_DIM, dv=HEAD_DIM,
               q_col0=0, k_col0=0, v_col0=GQA_HEADS + GQA_KV, ctx_queries=False, tq=512, tk=1024, row_split=0)
    return _matmul(o, w_o, SEQ, F32, 512, 1024)


def kernel(x, c, ctx, c_ctx, ada_w, ada_b, mix_norm, ffn_norm, ev_w_in, ev_rpb, ev_q_norm, ev_w_uq, ev_kv_norm, ev_w_ukv, ev_w_out, od_w_qkv, od_q_norm, od_k_norm, od_w_o, router_w, router_b, exp_w_gate, exp_w_up, exp_w_down, sh_w_gate, sh_w_up, sh_w_down, final_norm):
    assert DEPTH == 2 and x.shape == (1, SEQ, D_MODEL)
    mods = _adaln_mods(c, c_ctx, ada_w, ada_b)
    xt = jnp.concatenate([x[0], ctx[0]], axis=0)

    def parts(i):
        return [mods[i, :, k * D_MODEL:(k + 1) * D_MODEL] for k in range(6)]

    sh1, sc1, g1, sh2, sc2, g2 = parts(0)
    (h,) = _norm_call(xt, N_TOK, mix_norm[0], mod=(sh1, sc1))
    o = _mixer_even(h, ev_w_in[0], ev_rpb[0], ev_q_norm[0], ev_w_uq[0], ev_kv_norm[0], ev_w_ukv[0], ev_w_out[0])
    xt, h, h32, eidx_t, ew_t, rank_t, counts = _norm_call(xt, N_TOK, ffn_norm[0], resid=(o, g1), mod=(sh2, sc2),
                                                          router=(router_w[0], router_b[0]))
    xt = _moe(0, xt, h, h32, eidx_t, ew_t, rank_t, counts, g2, N_TOK, exp_w_gate, exp_w_up, exp_w_down,
              sh_w_gate, sh_w_up, sh_w_down)

    sh1, sc1, g1, sh2, sc2, g2 = parts(1)
    (h,) = _norm_call(xt, N_TOK, mix_norm[1], mod=(sh1, sc1))
    o = _mixer_odd_last(h, od_w_qkv[0], od_q_norm[0], od_k_norm[0], od_w_o[0])
    xl, h, h32, eidx_t, ew_t, rank_t, counts = _norm_call(xt, SEQ, ffn_norm[1], resid=(o, g1), mod=(sh2, sc2),
                                                          router=(router_w[1], router_b[1]))
    xl = _moe(1, xl, h, h32, eidx_t, ew_t, rank_t, counts, g2, SEQ, exp_w_gate, exp_w_up, exp_w_down,
              sh_w_gate, sh_w_up, sh_w_down)
    (out,) = _norm_call(xl, SEQ, final_norm, out_dtype=F32)
    return out[None]
```

```python
import functools

import numpy as np
import jax
import jax.numpy as jnp
from jax import lax
from jax.experimental import pallas as pl
from jax.experimental.pallas import tpu as pltpu

F32 = jnp.float32
BF16 = jnp.bfloat16

D_MODEL = 2048
SEQ = 8192
DEPTH = 2
GRID_W = 64
GRID_H = SEQ // GRID_W
CTX_LEN = 256
N_TOK = SEQ + CTX_LEN
EPS = 1e-6
ROPE_THETA = 10000.0
HEAD_DIM = 128
NA_HEADS = 8
NA_KH = 8
NA_KW = 16
MLA_HEADS = 8
MLA_Q_RANK = 512
MLA_KV_RANK = 512
MLA_NOPE = 128
MLA_ROPE = 64
MLA_V = 128
MLA_QK_PAD = 256
GQA_HEADS = 16
GQA_KV = 4
GQA_GROUP = GQA_HEADS // GQA_KV
N_EXPERTS = 64
N_GROUPS = 8
GROUP_SIZE = N_EXPERTS // N_GROUPS
TOPK_GROUPS = 4
MOE_TOP_K = 8
D_EXPERT = 512
ROUTED_SCALE = 2.5

NEG = -1e30
LOG2E = 1.4426950408889634

VMEM_LIMIT_CAP = 56 * 1024 * 1024

ROW_TILE = 256
MM_TM = 768
NA_RB = 4
NA_U = NA_RB + NA_KH - 1
EXPERT_TILE = 256
COMBINE_TILE = 128
DMA_LOOP_UNROLL = 4


def _cparams(sem, vmem_bytes):
    return pltpu.CompilerParams(
        dimension_semantics=sem,
        vmem_limit_bytes=int(min(max(vmem_bytes, 16 * 1024 * 1024), VMEM_LIMIT_CAP)),
    )


def _pack_bf16_pairs(x):
    w = x.shape[1] // 2

    def rounded(v):
        b = lax.bitcast_convert_type(v, jnp.uint32)
        return b + (((b >> 16) & jnp.uint32(1)) + jnp.uint32(0x7FFF))

    return (rounded(x[:, :w]) >> 16) | (rounded(x[:, w:]) & jnp.uint32(0xFFFF0000))


def _unpack_bf16_pairs(p):
    lo = lax.bitcast_convert_type(p << 16, F32)
    hi = lax.bitcast_convert_type(p & jnp.uint32(0xFFFF0000), F32)
    return jnp.concatenate([lo, hi], axis=1)


def _dot(a, b):
    return jnp.dot(a, b, preferred_element_type=F32)


def _dot_nt(a, b):
    return lax.dot_general(a, b, (((1,), (1,)), ((), ())), preferred_element_type=F32)


def _mods_kernel(cc_ref, w_ref, b_ref, o_ref, acc_ref, *, nk):
    k = pl.program_id(2)

    @pl.when(k == 0)
    def _():
        acc_ref[...] = jnp.zeros_like(acc_ref)

    a = cc_ref[...]
    a = a * jax.nn.sigmoid(a)
    w = w_ref[...]
    tk, tn = w.shape
    w3 = w.reshape(tk // 8, 8, tn)
    for v in range(2):
        col = a[:, v:v + 1].reshape(tk // 8, 8, 1)
        acc_ref[v] += jnp.sum(w3 * col, axis=0)

    @pl.when(k == nk - 1)
    def _():
        o_ref[...] = jnp.sum(acc_ref[...], axis=1) + b_ref[...]


def _adaln_mods(c, c_ctx, ada_w, ada_b):
    depth, d, n = ada_w.shape
    tk, tn = 512, 1536
    nk = d // tk
    cc = jnp.stack([c[0], c_ctx], axis=1)
    return pl.pallas_call(
        functools.partial(_mods_kernel, nk=nk),
        grid=(depth, n // tn, nk),
        in_specs=[
            pl.BlockSpec((tk, 2), lambda i, j, k: (k, 0)),
            pl.BlockSpec((None, tk, tn), lambda i, j, k: (i, k, j)),
            pl.BlockSpec((None, 1, tn), lambda i, j, k: (i, 0, j)),
        ],
        out_specs=pl.BlockSpec((None, 2, tn), lambda i, j, k: (i, 0, j)),
        out_shape=jax.ShapeDtypeStruct((depth, 2, n), F32),
        scratch_shapes=[pltpu.VMEM((2, 8, tn), F32)],
        compiler_params=_cparams(("arbitrary", "arbitrary", "arbitrary"), 4 * tk * tn * 4),
        name="adaln_mods",
    )(cc, ada_w, ada_b.reshape(depth, 1, n))


def _router_select(hf, rwt_ref, rb_ref, base_ref):
    logits = lax.dot_general(rwt_ref[...], hf, (((1,), (1,)), ((), ())),
                             precision=lax.Precision.HIGHEST, preferred_element_type=F32)
    scores = jax.nn.sigmoid(logits)
    sel = scores + rb_ref[...]
    tm = sel.shape[1]
    ninf = jnp.float32(-jnp.inf)
    sub = lax.broadcasted_iota(jnp.int32, (GROUP_SIZE, tm), 0)
    gi = lax.broadcasted_iota(jnp.int32, (N_GROUPS, tm), 0)
    gs = jnp.zeros((N_GROUPS, tm), F32)
    for g in range(N_GROUPS):
        blk = sel[g * GROUP_SIZE:(g + 1) * GROUP_SIZE, :]
        m1 = jnp.max(blk, axis=0, keepdims=True)
        first = jnp.min(jnp.where(blk == m1, sub, GROUP_SIZE), axis=0, keepdims=True)
        m2 = jnp.max(jnp.where(sub == first, ninf, blk), axis=0, keepdims=True)
        gs = jnp.where(gi == g, m1 + m2, gs)
    gsel = jnp.zeros((N_GROUPS, tm), F32)
    cur = gs
    for _ in range(TOPK_GROUPS):
        mx = jnp.max(cur, axis=0, keepdims=True)
        idx = jnp.min(jnp.where(cur == mx, gi, N_GROUPS), axis=0, keepdims=True)
        pick = gi == idx
        gsel = jnp.where(pick, 1.0, gsel)
        cur = jnp.where(pick, ninf, cur)
    ei = lax.broadcasted_iota(jnp.int32, (N_EXPERTS, tm), 0)
    eg = ei // GROUP_SIZE
    emask = jnp.zeros((N_EXPERTS, tm), F32)
    for g in range(N_GROUPS):
        emask = jnp.where(eg == g, gsel[g:g + 1, :], emask)
    cur = jnp.where(emask > 0.0, sel, ninf)
    idxs, ws, picks = [], [], []
    for _ in range(MOE_TOP_K):
        mx = jnp.max(cur, axis=0, keepdims=True)
        idx = jnp.min(jnp.where(cur == mx, ei, N_EXPERTS), axis=0, keepdims=True)
        pick = ei == idx
        ws.append(jnp.sum(jnp.where(pick, scores, 0.0), axis=0, keepdims=True))
        idxs.append(idx)
        picks.append(pick)
        cur = jnp.where(pick, ninf, cur)
    total = ws[0]
    for w in ws[1:]:
        total = total + w
    ws = [w / total * ROUTED_SCALE for w in ws]
    onehot = jnp.zeros((N_EXPERTS, tm), F32)
    for pick in picks:
        onehot = jnp.where(pick, 1.0, onehot)
    tri = (lax.broadcasted_iota(jnp.int32, (tm, tm), 0) < lax.broadcasted_iota(jnp.int32, (tm, tm), 1))
    before = _dot(onehot.astype(BF16), jnp.where(tri, 1.0, 0.0).astype(BF16)) + base_ref[...]
    ranks = [jnp.sum(jnp.where(pick, before, 0.0), axis=0, keepdims=True).astype(jnp.int32) for pick in picks]
    base_ref[...] += jnp.sum(onehot, axis=1, keepdims=True)
    return idxs, ws, ranks


def _norm_kernel(*refs, has_resid, modulated, with_router, out_dtype):
    it = iter(refs)
    x_ref = next(it)
    if has_resid:
        o_ref = next(it)
        gate_ref = next(it)
    g_ref = next(it)
    if modulated:
        shift_ref = next(it)
        scale_ref = next(it)
    if with_router:
        rwt_ref = next(it)
        rb_ref = next(it)
    if has_resid:
        xn_ref = next(it)
    h_ref = next(it)
    if with_router:
        hp_ref = next(it)
        eidx_ref = next(it)
        ew_ref = next(it)
        rank_ref = next(it)
        cnt_ref = next(it)
        base_ref = next(it)

        @pl.when(pl.program_id(0) == 0)
        def _():
            base_ref[...] = jnp.zeros_like(base_ref)

    x = x_ref[...]
    if has_resid:
        x = x + gate_ref[...] * o_ref[...].astype(F32)
        xn_ref[...] = x
    y = x * lax.rsqrt(jnp.mean(x * x, axis=-1, keepdims=True) + EPS)
    y = y * g_ref[...]
    if modulated:
        y = y * (1.0 + scale_ref[...]) + shift_ref[...]
    h_ref[...] = y.astype(out_dtype)
    if with_router:
        hp_ref[...] = _pack_bf16_pairs(y)
        idxs, ws, ranks = _router_select(y, rwt_ref, rb_ref, base_ref)
        for k in range(MOE_TOP_K):
            eidx_ref[k:k + 1, :] = idxs[k]
            ew_ref[k:k + 1, :] = ws[k]
            rank_ref[k:k + 1, :] = ranks[k]
        cnt_ref[...] = base_ref[...].astype(jnp.int32)


def _norm_call(x, n_rows, gain, *, resid=None, mod=None, router=None, out_dtype=BF16):
    d = x.shape[1]
    tm = ROW_TILE
    nblk = n_rows // tm
    lat_blocks = SEQ // tm

    def cls_map(i):
        return (jnp.where(i >= lat_blocks, 1, 0), 0, 0)

    row = pl.BlockSpec((tm, d), lambda i: (i, 0))
    vec = pl.BlockSpec((None, 1, d), cls_map)
    args, in_specs = [x], [row]
    if resid is not None:
        o, gate = resid
        args += [o, gate.reshape(2, 1, d)]
        in_specs += [row, vec]
    args.append(gain.reshape(1, d))
    in_specs.append(pl.BlockSpec((1, d), lambda i: (0, 0)))
    if mod is not None:
        shift, scale = mod
        args += [shift.reshape(2, 1, d), scale.reshape(2, 1, d)]
        in_specs += [vec, vec]
    if router is not None:
        rw, rb = router
        args += [rw.T, rb.reshape(N_EXPERTS, 1)]
        in_specs += [pl.BlockSpec((N_EXPERTS, d), lambda i: (0, 0)),
                     pl.BlockSpec((N_EXPERTS, 1), lambda i: (0, 0))]
    out_shape, out_specs = [], []
    if resid is not None:
        out_shape.append(jax.ShapeDtypeStruct((n_rows, d), F32))
        out_specs.append(row)
    out_shape.append(jax.ShapeDtypeStruct((n_rows, d), out_dtype))
    out_specs.append(row)
    if router is not None:
        out_shape += [jax.ShapeDtypeStruct((n_rows, d // 2), jnp.uint32),
                      jax.ShapeDtypeStruct((MOE_TOP_K, n_rows), jnp.int32),
                      jax.ShapeDtypeStruct((MOE_TOP_K, n_rows), F32),
                      jax.ShapeDtypeStruct((MOE_TOP_K, n_rows), jnp.int32),
                      jax.ShapeDtypeStruct((N_EXPERTS, 1), jnp.int32)]
        topk = pl.BlockSpec((MOE_TOP_K, tm), lambda i: (0, i))
        out_specs += [pl.BlockSpec((tm, d // 2), lambda i: (i, 0)), topk, topk, topk,
                      pl.BlockSpec((N_EXPERTS, 1), lambda i: (0, 0))]
    return pl.pallas_call(
        functools.partial(_norm_kernel, has_resid=resid is not None, modulated=mod is not None,
                          with_router=router is not None, out_dtype=out_dtype),
        grid=(nblk,),
        in_specs=in_specs,
        out_specs=out_specs,
        out_shape=out_shape,
        scratch_shapes=[pltpu.VMEM((N_EXPERTS, 1), F32)] if router is not None else [],
        compiler_params=_cparams(("arbitrary",), 16 * tm * d * 4),
        name="norm_router" if router is not None else "norm",
    )(*args)


def _mm_kernel(a_ref, w_ref, o_ref, wb_ref):
    @pl.when(pl.program_id(1) == 0)
    def _():
        wb_ref[...] = w_ref[...].astype(BF16)

    o_ref[...] = _dot(a_ref[...], wb_ref[...]).astype(o_ref.dtype)


def _matmul(a, w, m_rows, out_dtype, tm, tn):
    k, n = w.shape
    assert m_rows % tm == 0 and n % tn == 0 and a.shape[1] == k
    osz = jnp.dtype(out_dtype).itemsize
    vmem = 2 * k * tn * 4 + k * tn * 2 + 2 * tm * k * 2 + 2 * tm * tn * osz + tm * tn * 4
    return pl.pallas_call(
        _mm_kernel,
        grid=(n // tn, m_rows // tm),
        in_specs=[pl.BlockSpec((tm, k), lambda j, i: (i, 0)),
                  pl.BlockSpec((k, tn), lambda j, i: (0, j))],
        out_specs=pl.BlockSpec((tm, tn), lambda j, i: (i, j)),
        out_shape=jax.ShapeDtypeStruct((m_rows, n), out_dtype),
        scratch_shapes=[pltpu.VMEM((k, tn), BF16)],
        compiler_params=_cparams(("arbitrary", "arbitrary"), vmem + (4 << 20)),
        name="matmul",
    )(a, w)


def _rope(x, cos_e, sin_s):
    w = x.shape[-1]
    lane = lax.broadcasted_iota(jnp.int32, x.shape, 1)
    nxt = pltpu.roll(x, w - 1, 1)
    prv = pltpu.roll(x, 1, 1)
    swapped = jnp.where(lane % 2 == 0, nxt, prv)
    return x * cos_e + swapped * sin_s


def _rms(x, g):
    return x * lax.rsqrt(jnp.mean(x * x, axis=-1, keepdims=True) + EPS) * g


def _mla_prep1_kernel(lat_ref, qg_ref, kvg_ref, cos_ref, sin_ref, nq_ref, nkv_ref, kr_ref):
    lat = lat_ref[...]
    nq_ref[...] = _rms(lat[:, :MLA_Q_RANK], qg_ref[...]).astype(BF16)
    nkv_ref[...] = _rms(lat[:, MLA_Q_RANK:MLA_Q_RANK + MLA_KV_RANK], kvg_ref[...]).astype(BF16)
    kr = lat[:, MLA_Q_RANK + MLA_KV_RANK:]
    kr_ref[...] = _rope(kr, cos_ref[...], sin_ref[...]).astype(BF16)


def _mla_prep1(lat, q_norm, kv_norm, cos_t, sin_t):
    n, wdt = lat.shape
    tm = ROW_TILE
    row = lambda w: pl.BlockSpec((tm, w), lambda i: (i, 0))
    one = lambda w: pl.BlockSpec((1, w), lambda i: (0, 0))
    return pl.pallas_call(
        _mla_prep1_kernel,
        grid=(n // tm,),
        in_specs=[row(wdt), one(MLA_Q_RANK), one(MLA_KV_RANK), row(128), row(128)],
        out_specs=[row(MLA_Q_RANK), row(MLA_KV_RANK), row(128)],
        out_shape=[jax.ShapeDtypeStruct((n, MLA_Q_RANK), BF16),
                   jax.ShapeDtypeStruct((n, MLA_KV_RANK), BF16),
                   jax.ShapeDtypeStruct((n, 128), BF16)],
        compiler_params=_cparams(("arbitrary",), 8 << 20),
        name="mla_prep1",
    )(lat, q_norm.reshape(1, -1), kv_norm.reshape(1, -1), cos_t, sin_t)


def _mla_prep2_kernel(q_ref, kn_ref, kr_ref, cos_ref, sin_ref, qo_ref, ko_ref, *, scale):
    cos_e = cos_ref[...]
    sin_s = sin_ref[...]
    kr = kr_ref[...]
    for h in range(MLA_HEADS):
        lo = h * MLA_QK_PAD
        qo_ref[:, lo:lo + MLA_NOPE] = (q_ref[:, lo:lo + MLA_NOPE].astype(F32) * scale).astype(BF16)
        qr = q_ref[:, lo + MLA_NOPE:lo + MLA_QK_PAD].astype(F32)
        qo_ref[:, lo + MLA_NOPE:lo + MLA_QK_PAD] = (_rope(qr, cos_e, sin_s) * scale).astype(BF16)
        ko_ref[:, lo:lo + MLA_NOPE] = kn_ref[:, h * MLA_NOPE:(h + 1) * MLA_NOPE]
        ko_ref[:, lo + MLA_NOPE:lo + MLA_QK_PAD] = kr


def _mla_prep2(q_raw, kv_all, kr, cos_t, sin_t):
    n = q_raw.shape[0]
    tm = ROW_TILE
    wq = MLA_HEADS * MLA_QK_PAD
    row = lambda w: pl.BlockSpec((tm, w), lambda i: (i, 0))
    return pl.pallas_call(
        functools.partial(_mla_prep2_kernel, scale=(MLA_NOPE + MLA_ROPE) ** -0.5 * LOG2E),
        grid=(n // tm,),
        in_specs=[row(wq), row(MLA_HEADS * MLA_NOPE), row(128), row(128), row(128)],
        out_specs=[row(wq), row(wq)],
        out_shape=[jax.ShapeDtypeStruct((n, wq), BF16), jax.ShapeDtypeStruct((n, wq), BF16)],
        compiler_params=_cparams(("arbitrary",), 16 << 20),
        name="mla_prep2",
    )(q_raw, kv_all, kr, cos_t, sin_t)


def _gqa_prep_kernel(qkv_ref, qg_ref, kg_ref, cos_ref, sin_ref, qo_ref, ko_ref, *, scale):
    cos_e = cos_ref[...]
    sin_s = sin_ref[...]
    for h in range(GQA_HEADS):
        xh = qkv_ref[:, h * HEAD_DIM:(h + 1) * HEAD_DIM].astype(F32)
        qo_ref[:, h * HEAD_DIM:(h + 1) * HEAD_DIM] = (
            _rope(_rms(xh, qg_ref[...]), cos_e, sin_s) * scale).astype(BF16)
    for h in range(GQA_KV):
        lo = (GQA_HEADS + h) * HEAD_DIM
        xh = qkv_ref[:, lo:lo + HEAD_DIM].astype(F32)
        ko_ref[:, h * HEAD_DIM:(h + 1) * HEAD_DIM] = _rope(_rms(xh, kg_ref[...]), cos_e, sin_s).astype(BF16)


def _gqa_prep(qkv, q_norm, k_norm, cos_t, sin_t):
    n = qkv.shape[0]
    tm = ROW_TILE
    wqk = (GQA_HEADS + GQA_KV) * HEAD_DIM
    row = lambda w: pl.BlockSpec((tm, w), lambda i: (i, 0))
    one = pl.BlockSpec((1, HEAD_DIM), lambda i: (0, 0))
    return pl.pallas_call(
        functools.partial(_gqa_prep_kernel, scale=HEAD_DIM ** -0.5 * LOG2E),
        grid=(n // tm,),
        in_specs=[row(wqk), one, one, row(HEAD_DIM), row(HEAD_DIM)],
        out_specs=[row(GQA_HEADS * HEAD_DIM), row(GQA_KV * HEAD_DIM)],
        out_shape=[jax.ShapeDtypeStruct((n, GQA_HEADS * HEAD_DIM), BF16),
                   jax.ShapeDtypeStruct((n, GQA_KV * HEAD_DIM), BF16)],
        compiler_params=_cparams(("arbitrary",), 16 << 20),
        name="gqa_prep",
    )(qkv, q_norm.reshape(1, -1), k_norm.reshape(1, -1), cos_t, sin_t)


def _flash_kernel(q_ref, k_ref, v_ref, o_ref, sa_ref, sb_ref, *, group, dk, dv, tk, n_lat, q_scale):
    tq = q_ref.shape[0]
    q = jnp.concatenate([q_ref[:, g * dk:(g + 1) * dk] for g in range(group)], axis=0) if group > 1 else q_ref[...]
    if q_scale is not None:
        q = (q.astype(F32) * q_scale).astype(BF16)

    s = _dot_nt(q, k_ref[SEQ:SEQ + CTX_LEN, :])
    m = jnp.max(s, axis=-1, keepdims=True)
    p = jnp.exp2(s - m)
    l = jnp.sum(p, axis=-1, keepdims=True)
    acc = _dot(p.astype(BF16), v_ref[SEQ:SEQ + CTX_LEN, :])

    if n_lat:
        def rows(ref, j):
            return ref[pl.ds(pl.multiple_of(j * tk, tk), tk), :]

        def update(s, vb, m, l, acc):
            m_new = jnp.maximum(m, jnp.max(s, axis=-1, keepdims=True))
            alpha = jnp.exp2(m - m_new)
            p = jnp.exp2(s - m_new)
            l = alpha * l + jnp.sum(p, axis=-1, keepdims=True)
            return m_new, l, alpha * acc + _dot(p.astype(BF16), vb)

        def pair(jj, carry, last):
            j0 = 2 * jj
            sb_ref[...] = _dot_nt(q, rows(k_ref, j0 + 1))
            carry = update(sa_ref[...], rows(v_ref, j0), *carry)
            if not last:
                sa_ref[...] = _dot_nt(q, rows(k_ref, j0 + 2))
            return update(sb_ref[...], rows(v_ref, j0 + 1), *carry)

        sa_ref[...] = _dot_nt(q, rows(k_ref, 0))
        carry = lax.fori_loop(0, n_lat // 2 - 1, lambda jj, c: pair(jj, c, False), (m, l, acc))
        m, l, acc = pair(n_lat // 2 - 1, carry, True)

    o = (acc / l).astype(o_ref.dtype)
    for g in range(group):
        o_ref[:, g * dv:(g + 1) * dv] = o[g * tq:(g + 1) * tq]


def _flash(q_arr, k_arr, v_arr, *, n_kv_heads, group, dk, dv, q_col0, k_col0, v_col0, ctx_queries, tq, tk,
           q_scale=None):
    n_tok = k_arr.shape[0]
    if ctx_queries:
        n_q, q_blk0, n_lat = CTX_LEN, SEQ // tq, 0
    else:
        n_q, q_blk0, n_lat = SEQ, 0, SEQ // tk
        assert n_lat % 2 == 0
    m = group * tq
    s_shape = (m, tk) if n_lat else (8, 128)
    vmem = (2 * n_tok * (dk + dv) * 2 + 4 * tq * group * (dk + dv) * 2
            + 5 * m * max(tk, CTX_LEN) * 4 + 4 * m * dv * 4)
    return pl.pallas_call(
        functools.partial(_flash_kernel, group=group, dk=dk, dv=dv, tk=tk, n_lat=n_lat, q_scale=q_scale),
        grid=(n_kv_heads, n_q // tq),
        in_specs=[pl.BlockSpec((tq, group * dk), lambda h, i: (q_blk0 + i, q_col0 + h)),
                  pl.BlockSpec((n_tok, dk), lambda h, i: (0, k_col0 + h)),
                  pl.BlockSpec((n_tok, dv), lambda h, i: (0, v_col0 + h))],
        out_specs=pl.BlockSpec((tq, group * dv), lambda h, i: (i, h)),
        out_shape=jax.ShapeDtypeStruct((n_q, n_kv_heads * group * dv), BF16),
        scratch_shapes=[pltpu.VMEM(s_shape, F32), pltpu.VMEM(s_shape, F32)],
        compiler_params=_cparams(("arbitrary", "arbitrary"), vmem + (8 << 20)),
        name="flash_ctx" if ctx_queries else "flash",
    )(q_arr, k_arr, v_arr)


def _na_start_row(b):
    return jnp.clip(NA_RB * b - NA_KH // 2, 0, GRID_H - NA_U)


def _na_kernel(q_ref, k_ref, v_ref, bias_ref, o_ref, *, scale):
    b = pl.program_id(1)
    start = pl.multiple_of(_na_start_row(b) * GRID_W, GRID_W)
    nwin = NA_U * GRID_W
    q = (q_ref[...].astype(F32) * scale).astype(BF16)
    s_loc = _dot_nt(q, k_ref[pl.ds(start, nwin), :]) + bias_ref[...]
    s_ctx = _dot_nt(q, k_ref[SEQ:SEQ + CTX_LEN, :])
    m = jnp.maximum(jnp.max(s_loc, axis=-1, keepdims=True), jnp.max(s_ctx, axis=-1, keepdims=True))
    p_loc = jnp.exp(s_loc - m)
    p_ctx = jnp.exp(s_ctx - m)
    l = jnp.sum(p_loc, axis=-1, keepdims=True) + jnp.sum(p_ctx, axis=-1, keepdims=True)
    o = _dot(p_loc.astype(BF16), v_ref[pl.ds(start, nwin), :]) + _dot(p_ctx.astype(BF16), v_ref[SEQ:SEQ + CTX_LEN, :])
    o_ref[...] = (o / l).astype(o_ref.dtype)


def _na_bias_tables(rpb):
    h = rpb.shape[0]
    w = np.arange(GRID_W)
    c0 = np.clip(w - NA_KW // 2, 0, GRID_W - NA_KW)
    colmask = (w[None, :] >= c0[:, None]) & (w[None, :] < c0[:, None] + NA_KW)
    colidx = np.clip(w[None, :] - w[:, None] + NA_KW - 1, 0, 2 * NA_KW - 2)
    slabs = jnp.where(colmask[None, None], rpb[:, :, colidx], NEG)
    slabs = jnp.concatenate([slabs, jnp.full((h, 1, GRID_W, GRID_W), NEG, rpb.dtype)], axis=1)
    masked = 2 * NA_KH - 1
    nb = GRID_H // NA_RB
    tabs = []
    for b in (0, 1, nb - 1):
        start = int(np.clip(NA_RB * b - NA_KH // 2, 0, GRID_H - NA_U))
        a = np.full((NA_RB, NA_U), masked, np.int32)
        for j in range(NA_RB):
            r = NA_RB * b + j
            r0 = int(np.clip(r - NA_KH // 2, 0, GRID_H - NA_KH))
            for u in range(NA_U):
                kr = start + u
                if r0 <= kr < r0 + NA_KH:
                    a[j, u] = kr - r + NA_KH - 1
        t = slabs[:, a]
        tabs.append(t.transpose(0, 1, 3, 2, 4).reshape(h, NA_RB * GRID_W, NA_U * GRID_W))
    return jnp.stack(tabs, axis=1)


def _na_attention(qkv, bias_tabs):
    n_tok = qkv.shape[0]
    nb = GRID_H // NA_RB
    tq = NA_RB * GRID_W
    nwin = NA_U * GRID_W

    def cls(b):
        return jnp.where(b == 0, 0, jnp.where(b == nb - 1, 2, 1))

    vmem = 4 * n_tok * HEAD_DIM * 2 + 2 * tq * nwin * 4 + 4 * tq * (nwin + CTX_LEN) * 4
    return pl.pallas_call(
        functools.partial(_na_kernel, scale=HEAD_DIM ** -0.5),
        grid=(NA_HEADS, nb),
        in_specs=[pl.BlockSpec((tq, HEAD_DIM), lambda h, b: (b, h)),
                  pl.BlockSpec((n_tok, HEAD_DIM), lambda h, b: (0, NA_HEADS + h)),
                  pl.BlockSpec((n_tok, HEAD_DIM), lambda h, b: (0, 2 * NA_HEADS + h)),
                  pl.BlockSpec((None, None, tq, nwin), lambda h, b: (h, cls(b), 0, 0))],
        out_specs=pl.BlockSpec((tq, HEAD_DIM), lambda h, b: (b, h)),
        out_shape=jax.ShapeDtypeStruct((SEQ, NA_HEADS * HEAD_DIM), BF16),
        compiler_params=_cparams(("arbitrary", "arbitrary"), vmem + (8 << 20)),
        name="na_attention",
    )(qkv, qkv, qkv, bias_tabs)


def _dispatch_kernel(tail_ref, nu_ref, slot_ref, h_ref, xg_hbm, zbuf, sem, zsem, *, tb, n_blocks):
    te = EXPERT_TILE

    @pl.when(pl.program_id(0) == 0)
    def _():
        zbuf[...] = jnp.zeros_like(zbuf)

        def zero_tile(row0, wait):
            cp = pltpu.make_async_copy(zbuf, xg_hbm.at[pl.ds(pl.multiple_of(row0, te), te)], zsem)
            if wait:
                cp.wait()
            else:
                cp.start()

        def zero_tail(e, wait):
            @pl.when(tail_ref[e] >= 0)
            def _():
                zero_tile(tail_ref[e], wait)

        for wait in (False, True):
            lax.fori_loop(0, N_EXPERTS, lambda e, c: (zero_tail(e, wait), c)[1], 0)
            lax.fori_loop(nu_ref[0], n_blocks, lambda b, c: (zero_tile(b * te, wait), c)[1], 0)

    def body(t, carry):
        for k in range(MOE_TOP_K):
            pltpu.make_async_copy(h_ref.at[pl.ds(t, 1)], xg_hbm.at[pl.ds(slot_ref[k, t], 1)], sem).start(
                priority=k % 2)
        return carry

    lax.fori_loop(0, tb, body, 0, unroll=DMA_LOOP_UNROLL)
    for k in range(MOE_TOP_K):
        pltpu.make_async_copy(h_ref, xg_hbm.at[pl.ds(0, tb)], sem).wait()


def _dispatch(h32, slot_t, tail_start, n_used, n_rows, n_slots):
    d = h32.shape[1]
    tb = ROW_TILE
    grid_spec = pltpu.PrefetchScalarGridSpec(
        num_scalar_prefetch=2,
        grid=(n_rows // tb,),
        in_specs=[pl.BlockSpec((MOE_TOP_K, tb), lambda i, tl, nu: (0, i), memory_space=pltpu.SMEM),
                  pl.BlockSpec((tb, d), lambda i, tl, nu: (i, 0))],
        out_specs=pl.BlockSpec(memory_space=pl.ANY),
        scratch_shapes=[pltpu.VMEM((EXPERT_TILE, d), h32.dtype), pltpu.SemaphoreType.DMA(()),
                        pltpu.SemaphoreType.DMA(())],
    )
    return pl.pallas_call(
        functools.partial(_dispatch_kernel, tb=tb, n_blocks=n_slots // EXPERT_TILE),
        grid_spec=grid_spec,
        out_shape=jax.ShapeDtypeStruct((n_slots, d), h32.dtype),
        compiler_params=_cparams(("arbitrary",), 16 << 20),
        name="moe_dispatch",
    )(tail_start, n_used, slot_t, h32)


def _ffn_kernel(be_ref, nu_ref, x_ref, wg_ref, wu_ref, wd_ref, o_ref, wgb, wub, wdb, *, packed):
    b = pl.program_id(0)
    prev = be_ref[jnp.maximum(b - 1, 0)]
    new_expert = jnp.logical_or(b == 0, be_ref[b] != prev)

    @pl.when(jnp.logical_and(new_expert, b < nu_ref[0]))
    def _():
        wgb[...] = wg_ref[...].astype(BF16)
        wub[...] = wu_ref[...].astype(BF16)
        wdb[...] = wd_ref[...].astype(BF16)

    @pl.when(b < nu_ref[0])
    def _():
        x = _unpack_bf16_pairs(x_ref[...]).astype(BF16) if packed else x_ref[...]
        g = _dot(x, wgb[...])
        u = _dot(x, wub[...])
        a = (g * jax.nn.sigmoid(g) * u).astype(BF16)
        y = _dot(a, wdb[...])
        o_ref[...] = _pack_bf16_pairs(y) if packed else y.astype(o_ref.dtype)

    @pl.when(b >= nu_ref[0])
    def _():
        o_ref[...] = jnp.zeros_like(o_ref)


def _expert_ffn(x_rows, block_expert, n_used, wg, wu, wd, layer, tb, packed):
    p, dx = x_rows.shape
    d, de = wg.shape[-2:]
    nb = p // tb
    out_dtype = x_rows.dtype
    vmem = 2 * 3 * d * de * 4 + 3 * d * de * 2 + 4 * tb * d * 2 + 4 * tb * de * 4 + 3 * tb * d * 4

    def blk(b, be, nu):
        return jnp.minimum(b, nu[0] - 1)

    grid_spec = pltpu.PrefetchScalarGridSpec(
        num_scalar_prefetch=2,
        grid=(nb,),
        in_specs=[pl.BlockSpec((tb, dx), lambda b, be, nu: (blk(b, be, nu), 0)),
                  pl.BlockSpec((None, None, d, de), lambda b, be, nu: (layer, be[blk(b, be, nu)], 0, 0)),
                  pl.BlockSpec((None, None, d, de), lambda b, be, nu: (layer, be[blk(b, be, nu)], 0, 0)),
                  pl.BlockSpec((None, None, de, d), lambda b, be, nu: (layer, be[blk(b, be, nu)], 0, 0))],
        out_specs=pl.BlockSpec((tb, dx), lambda b, be, nu: (b, 0)),
        scratch_shapes=[pltpu.VMEM((d, de), BF16), pltpu.VMEM((d, de), BF16), pltpu.VMEM((de, d), BF16)],
    )
    return pl.pallas_call(
        functools.partial(_ffn_kernel, packed=packed),
        grid_spec=grid_spec,
        out_shape=jax.ShapeDtypeStruct((p, dx), out_dtype),
        compiler_params=_cparams(("arbitrary",), vmem + (4 << 20)),
        name="expert_ffn",
    )(block_expert, n_used, x_rows, wg, wu, wd)


def _combine_kernel(s_cur, s_nxt, y_hbm, ew_ref, ysh_ref, x_ref, gate_ref, o_ref, buf, sem, *, tc, nblk):
    i = pl.program_id(0)

    def gather(slot_ref, bslot):
        def body(t, carry):
            for k in range(MOE_TOP_K):
                pltpu.make_async_copy(y_hbm.at[pl.ds(slot_ref[k, t], 1)],
                                      buf.at[bslot, k, pl.ds(t, 1)], sem.at[bslot]).start()
            return carry

        lax.fori_loop(0, tc, body, 0, unroll=DMA_LOOP_UNROLL)

    @pl.when(i == 0)
    def _():
        gather(s_cur, 0)

    cur = i % 2

    @pl.when(i + 1 < nblk)
    def _():
        @pl.when(cur == 0)
        def _():
            gather(s_nxt, 1)

        @pl.when(cur == 1)
        def _():
            gather(s_nxt, 0)

    def finish(bslot):
        for k in range(MOE_TOP_K):
            pltpu.make_async_copy(y_hbm.at[pl.ds(0, tc)], buf.at[bslot, k], sem.at[bslot]).wait()
        ew = ew_ref[...]
        acc = ysh_ref[...].astype(F32)
        for k in range(MOE_TOP_K):
            acc = acc + ew[:, k:k + 1] * _unpack_bf16_pairs(buf[bslot, k])
        o_ref[...] = x_ref[...] + gate_ref[...] * acc

    @pl.when(cur == 0)
    def _():
        finish(0)

    @pl.when(cur == 1)
    def _():
        finish(1)


def _combine(y, slot_t, ew, ysh, x, gate, n_rows):
    d = x.shape[1]
    tc = COMBINE_TILE
    nblk = n_rows // tc
    lat_blocks = SEQ // tc
    row = pl.BlockSpec((tc, d), lambda i: (i, 0))
    vmem = 2 * MOE_TOP_K * tc * d * 4 + 8 * tc * d * 4
    return pl.pallas_call(
        functools.partial(_combine_kernel, tc=tc, nblk=nblk),
        grid=(nblk,),
        in_specs=[pl.BlockSpec((MOE_TOP_K, tc), lambda i: (0, i), memory_space=pltpu.SMEM),
                  pl.BlockSpec((MOE_TOP_K, tc), lambda i: (0, jnp.minimum(i + 1, nblk - 1)),
                               memory_space=pltpu.SMEM),
                  pl.BlockSpec(memory_space=pl.ANY),
                  pl.BlockSpec((tc, MOE_TOP_K), lambda i: (i, 0)),
                  row, row,
                  pl.BlockSpec((None, 1, d), lambda i: (jnp.where(i >= lat_blocks, 1, 0), 0, 0))],
        out_specs=row,
        out_shape=jax.ShapeDtypeStruct((n_rows, d), F32),
        scratch_shapes=[pltpu.VMEM((2, MOE_TOP_K, tc, d // 2), jnp.uint32), pltpu.SemaphoreType.DMA((2,))],
        compiler_params=_cparams(("arbitrary",), vmem + (4 << 20)),
        name="moe_combine",
    )(slot_t, slot_t, y, ew, ysh, x, gate.reshape(2, 1, d))


def _routing_tables(counts, n_rows):
    tb = EXPERT_TILE
    n_blocks = -(-(n_rows * MOE_TOP_K) // tb) + N_EXPERTS
    padded = (counts + tb - 1) // tb * tb
    pad_end = jnp.cumsum(padded)
    pad_start = (pad_end - padded).astype(jnp.int32)
    tail_start = jnp.where(padded > 0, pad_end - tb, -1).astype(jnp.int32)
    blk_start = jnp.arange(n_blocks, dtype=jnp.int32) * tb
    block_expert = jnp.sum(blk_start[:, None] >= pad_end[None, :], axis=1)
    block_expert = jnp.minimum(block_expert, N_EXPERTS - 1).astype(jnp.int32)
    n_used = (pad_end[-1] // tb).astype(jnp.int32).reshape(1)
    return pad_start, tail_start, block_expert, n_used, n_blocks * tb


def _moe(layer, x_new, h, h32, eidx_t, ew_t, rank_t, counts, gate, n_rows, wg, wu, wd, sg, su, sd):
    pad_start, tail_start, block_expert, n_used, n_slots = _routing_tables(counts[:, 0], n_rows)
    e_iota = jnp.arange(N_EXPERTS, dtype=jnp.int32)
    slot_t = jnp.sum(jnp.where(eidx_t[:, :, None] == e_iota, pad_start, 0), axis=-1) + rank_t
    xg = _dispatch(h32, slot_t, tail_start, n_used, n_rows, n_slots)
    y = _expert_ffn(xg, block_expert, n_used, wg, wu, wd, layer, EXPERT_TILE, True)
    tm = MM_TM if n_rows % MM_TM == 0 else 512
    ysh = _expert_ffn(h, jnp.zeros((n_rows // tm,), jnp.int32), jnp.full((1,), n_rows // tm, jnp.int32),
                      sg[:, None], su[:, None], sd[:, None], layer, tm, False)
    return _combine(y, slot_t, ew_t.T, ysh, x_new, gate, n_rows)


def _rope_tables(d, width):
    t = jnp.arange(SEQ)
    row = (t // GRID_W).astype(F32)
    col = (t % GRID_W).astype(F32)
    n = d // 4
    inv = ROPE_THETA ** (-jnp.arange(n, dtype=F32) / n)
    ang = jnp.concatenate([row[:, None] * inv, col[:, None] * inv], axis=-1)
    ang = jnp.concatenate([ang, jnp.zeros((CTX_LEN, d // 2), F32)], axis=0)
    cos = jnp.repeat(jnp.cos(ang), 2, axis=1)
    sin = jnp.repeat(jnp.sin(ang), 2, axis=1) * jnp.tile(jnp.array([-1.0, 1.0], F32), d // 2)
    if width > d:
        cos = jnp.concatenate([cos, jnp.ones((N_TOK, width - d), F32)], axis=1)
        sin = jnp.concatenate([sin, jnp.zeros((N_TOK, width - d), F32)], axis=1)
    return cos, sin


def _mixer_even(h, w_in, rpb, q_norm, w_uq, kv_norm, w_ukv, w_out):
    n_na = 3 * NA_HEADS * HEAD_DIM
    qkv = _matmul(h, w_in[:, :n_na], N_TOK, BF16, MM_TM, 1024)
    w_tail = jnp.pad(w_in[:, n_na:], ((0, 0), (0, 64)))
    lat = _matmul(h, w_tail, N_TOK, F32, MM_TM, 384)
    cos_t, sin_t = _rope_tables(MLA_ROPE, 128)
    nq, nkv, kr = _mla_prep1(lat, q_norm, kv_norm, cos_t, sin_t)
    wq = w_uq.reshape(MLA_Q_RANK, MLA_HEADS, MLA_NOPE + MLA_ROPE)
    wq = jnp.pad(wq, ((0, 0), (0, 0), (0, MLA_QK_PAD - MLA_NOPE - MLA_ROPE))).reshape(MLA_Q_RANK, -1)
    wkv = w_ukv.reshape(MLA_KV_RANK, MLA_HEADS, MLA_NOPE + MLA_V)
    wkv = jnp.concatenate([wkv[:, :, :MLA_NOPE].reshape(MLA_KV_RANK, -1),
                           wkv[:, :, MLA_NOPE:].reshape(MLA_KV_RANK, -1)], axis=1)
    q_raw = _matmul(nq, wq, N_TOK, F32, MM_TM, 1024)
    kv_all = _matmul(nkv, wkv, N_TOK, BF16, MM_TM, 1024)
    q_mla, k_cat = _mla_prep2(q_raw, kv_all, kr, cos_t, sin_t)

    o_na = _na_attention(qkv, _na_bias_tables(rpb))
    mla = dict(n_kv_heads=MLA_HEADS, group=1, dk=MLA_QK_PAD, dv=MLA_V, q_col0=0, k_col0=0, v_col0=MLA_HEADS)
    o_mla = _flash(q_mla, k_cat, kv_all, ctx_queries=False, tq=1024, tk=1024, **mla)
    o_mla_c = _flash(q_mla, k_cat, kv_all, ctx_queries=True, tq=256, tk=512, **mla)
    na = dict(n_kv_heads=NA_HEADS, group=1, dk=HEAD_DIM, dv=HEAD_DIM, q_col0=0, k_col0=NA_HEADS, v_col0=2 * NA_HEADS)
    o_na_c = _flash(qkv, qkv, qkv, ctx_queries=True, tq=256, tk=512, q_scale=HEAD_DIM ** -0.5 * LOG2E, **na)
    o = jnp.concatenate([jnp.concatenate([o_na, o_mla], axis=1),
                         jnp.concatenate([o_na_c, o_mla_c], axis=1)], axis=0)
    return _matmul(o, w_out, N_TOK, F32, MM_TM, 1024)


def _mixer_odd_last(h, w_qkv, q_norm, k_norm, w_o):
    qkv = _matmul(h, w_qkv, N_TOK, BF16, MM_TM, 1024)
    cos_t, sin_t = _rope_tables(HEAD_DIM, HEAD_DIM)
    q_g, k_g = _gqa_prep(qkv, q_norm, k_norm, cos_t, sin_t)
    o = _flash(q_g, k_g, qkv, n_kv_heads=GQA_KV, group=GQA_GROUP, dk=HEAD_DIM, dv=HEAD_DIM,
               q_col0=0, k_col0=0, v_col0=GQA_HEADS + GQA_KV, ctx_queries=False, tq=256, tk=1024)
    return _matmul(o, w_o, SEQ, F32, 512, 1024)


def kernel(x, c, ctx, c_ctx, ada_w, ada_b, mix_norm, ffn_norm, ev_w_in, ev_rpb, ev_q_norm, ev_w_uq, ev_kv_norm, ev_w_ukv, ev_w_out, od_w_qkv, od_q_norm, od_k_norm, od_w_o, router_w, router_b, exp_w_gate, exp_w_up, exp_w_down, sh_w_gate, sh_w_up, sh_w_down, final_norm):
    assert DEPTH == 2 and x.shape == (1, SEQ, D_MODEL)
    mods = _adaln_mods(c, c_ctx, ada_w, ada_b)
    xt = jnp.concatenate([x[0], ctx[0]], axis=0)

    def parts(i):
        return [mods[i, :, k * D_MODEL:(k + 1) * D_MODEL] for k in range(6)]

    sh1, sc1, g1, sh2, sc2, g2 = parts(0)
    (h,) = _norm_call(xt, N_TOK, mix_norm[0], mod=(sh1, sc1))
    o = _mixer_even(h, ev_w_in[0], ev_rpb[0], ev_q_norm[0], ev_w_uq[0], ev_kv_norm[0], ev_w_ukv[0], ev_w_out[0])
    xt, h, h32, eidx_t, ew_t, rank_t, counts = _norm_call(xt, N_TOK, ffn_norm[0], resid=(o, g1), mod=(sh2, sc2),
                                                          router=(router_w[0], router_b[0]))
    xt = _moe(0, xt, h, h32, eidx_t, ew_t, rank_t, counts, g2, N_TOK, exp_w_gate, exp_w_up, exp_w_down,
              sh_w_gate, sh_w_up, sh_w_down)

    sh1, sc1, g1, sh2, sc2, g2 = parts(1)
    (h,) = _norm_call(xt, N_TOK, mix_norm[1], mod=(sh1, sc1))
    o = _mixer_odd_last(h, od_w_qkv[0], od_q_norm[0], od_k_norm[0], od_w_o[0])
    xl, h, h32, eidx_t, ew_t, rank_t, counts = _norm_call(xt, SEQ, ffn_norm[1], resid=(o, g1), mod=(sh2, sc2),
                                                          router=(router_w[1], router_b[1]))
    xl = _moe(1, xl, h, h32, eidx_t, ew_t, rank_t, counts, g2, SEQ, exp_w_gate, exp_w_up, exp_w_down,
              sh_w_gate, sh_w_up, sh_w_down)
    (out,) = _norm_call(xl, SEQ, final_norm, out_dtype=F32)
    return out[None]
```

```python
import functools

import numpy as np
import jax
import jax.numpy as jnp
from jax import lax
from jax.experimental import pallas as pl
from jax.experimental.pallas import tpu as pltpu

F32 = jnp.float32
BF16 = jnp.bfloat16

D_MODEL = 2048
SEQ = 8192
DEPTH = 2
GRID_W = 64
GRID_H = SEQ // GRID_W
CTX_LEN = 256
N_TOK = SEQ + CTX_LEN
EPS = 1e-6
ROPE_THETA = 10000.0
HEAD_DIM = 128
NA_HEADS = 8
NA_KH = 8
NA_KW = 16
MLA_HEADS = 8
MLA_Q_RANK = 512
MLA_KV_RANK = 512
MLA_NOPE = 128
MLA_ROPE = 64
MLA_V = 128
MLA_QK_PAD = 256
GQA_HEADS = 16
GQA_KV = 4
GQA_GROUP = GQA_HEADS // GQA_KV
N_EXPERTS = 64
N_GROUPS = 8
GROUP_SIZE = N_EXPERTS // N_GROUPS
TOPK_GROUPS = 4
MOE_TOP_K = 8
D_EXPERT = 512
ROUTED_SCALE = 2.5

NEG = -1e30
LOG2E = 1.4426950408889634

VMEM_LIMIT_CAP = 56 * 1024 * 1024

ROW_TILE = 256
MM_TM = 768
NA_RB = 4
NA_U = NA_RB + NA_KH - 1
EXPERT_TILE = 256
COMBINE_TILE = 128
DMA_LOOP_UNROLL = 128


def _cparams(sem, vmem_bytes):
    return pltpu.CompilerParams(
        dimension_semantics=sem,
        vmem_limit_bytes=int(min(max(vmem_bytes, 16 * 1024 * 1024), VMEM_LIMIT_CAP)),
    )


def _pack_bf16_pairs(x):
    w = x.shape[1] // 2

    def rounded(v):
        b = lax.bitcast_convert_type(v, jnp.uint32)
        return b + (((b >> 16) & jnp.uint32(1)) + jnp.uint32(0x7FFF))

    return (rounded(x[:, :w]) >> 16) | (rounded(x[:, w:]) & jnp.uint32(0xFFFF0000))


def _unpack_bf16_pairs(p):
    lo = lax.bitcast_convert_type(p << 16, F32)
    hi = lax.bitcast_convert_type(p & jnp.uint32(0xFFFF0000), F32)
    return jnp.concatenate([lo, hi], axis=1)


def _dot(a, b):
    return jnp.dot(a, b, preferred_element_type=F32)


def _dot_nt(a, b):
    return lax.dot_general(a, b, (((1,), (1,)), ((), ())), preferred_element_type=F32)


def _mods_kernel(cc_ref, w_ref, b_ref, o_ref, acc_ref, *, nk):
    k = pl.program_id(2)

    @pl.when(k == 0)
    def _():
        acc_ref[...] = jnp.zeros_like(acc_ref)

    a = cc_ref[...]
    a = a * jax.nn.sigmoid(a)
    w = w_ref[...]
    tk, tn = w.shape
    w3 = w.reshape(tk // 8, 8, tn)
    for v in range(2):
        col = a[:, v:v + 1].reshape(tk // 8, 8, 1)
        acc_ref[v] += jnp.sum(w3 * col, axis=0)

    @pl.when(k == nk - 1)
    def _():
        o_ref[...] = jnp.sum(acc_ref[...], axis=1) + b_ref[...]


def _adaln_mods(c, c_ctx, ada_w, ada_b):
    depth, d, n = ada_w.shape
    tk, tn = 512, 1536
    nk = d // tk
    cc = jnp.stack([c[0], c_ctx], axis=1)
    return pl.pallas_call(
        functools.partial(_mods_kernel, nk=nk),
        grid=(depth, n // tn, nk),
        in_specs=[
            pl.BlockSpec((tk, 2), lambda i, j, k: (k, 0)),
            pl.BlockSpec((None, tk, tn), lambda i, j, k: (i, k, j)),
            pl.BlockSpec((None, 1, tn), lambda i, j, k: (i, 0, j)),
        ],
        out_specs=pl.BlockSpec((None, 2, tn), lambda i, j, k: (i, 0, j)),
        out_shape=jax.ShapeDtypeStruct((depth, 2, n), F32),
        scratch_shapes=[pltpu.VMEM((2, 8, tn), F32)],
        compiler_params=_cparams(("arbitrary", "arbitrary", "arbitrary"), 4 * tk * tn * 4),
        name="adaln_mods",
    )(cc, ada_w, ada_b.reshape(depth, 1, n))


def _router_select(hf, rwt_ref, rb_ref, base_ref):
    logits = lax.dot_general(rwt_ref[...], hf, (((1,), (1,)), ((), ())),
                             precision=lax.Precision.HIGHEST, preferred_element_type=F32)
    scores = jax.nn.sigmoid(logits)
    sel = scores + rb_ref[...]
    tm = sel.shape[1]
    ninf = jnp.float32(-jnp.inf)
    sub = lax.broadcasted_iota(jnp.int32, (GROUP_SIZE, tm), 0)
    gi = lax.broadcasted_iota(jnp.int32, (N_GROUPS, tm), 0)
    gs = jnp.zeros((N_GROUPS, tm), F32)
    for g in range(N_GROUPS):
        blk = sel[g * GROUP_SIZE:(g + 1) * GROUP_SIZE, :]
        m1 = jnp.max(blk, axis=0, keepdims=True)
        first = jnp.min(jnp.where(blk == m1, sub, GROUP_SIZE), axis=0, keepdims=True)
        m2 = jnp.max(jnp.where(sub == first, ninf, blk), axis=0, keepdims=True)
        gs = jnp.where(gi == g, m1 + m2, gs)
    gsel = jnp.zeros((N_GROUPS, tm), F32)
    cur = gs
    for _ in range(TOPK_GROUPS):
        mx = jnp.max(cur, axis=0, keepdims=True)
        idx = jnp.min(jnp.where(cur == mx, gi, N_GROUPS), axis=0, keepdims=True)
        pick = gi == idx
        gsel = jnp.where(pick, 1.0, gsel)
        cur = jnp.where(pick, ninf, cur)
    ei = lax.broadcasted_iota(jnp.int32, (N_EXPERTS, tm), 0)
    eg = ei // GROUP_SIZE
    emask = jnp.zeros((N_EXPERTS, tm), F32)
    for g in range(N_GROUPS):
        emask = jnp.where(eg == g, gsel[g:g + 1, :], emask)
    cur = jnp.where(emask > 0.0, sel, ninf)
    idxs, ws, picks = [], [], []
    for _ in range(MOE_TOP_K):
        mx = jnp.max(cur, axis=0, keepdims=True)
        idx = jnp.min(jnp.where(cur == mx, ei, N_EXPERTS), axis=0, keepdims=True)
        pick = ei == idx
        ws.append(jnp.sum(jnp.where(pick, scores, 0.0), axis=0, keepdims=True))
        idxs.append(idx)
        picks.append(pick)
        cur = jnp.where(pick, ninf, cur)
    total = ws[0]
    for w in ws[1:]:
        total = total + w
    ws = [w / total * ROUTED_SCALE for w in ws]
    onehot = jnp.zeros((N_EXPERTS, tm), F32)
    for pick in picks:
        onehot = jnp.where(pick, 1.0, onehot)
    tri = (lax.broadcasted_iota(jnp.int32, (tm, tm), 0) < lax.broadcasted_iota(jnp.int32, (tm, tm), 1))
    before = _dot(onehot.astype(BF16), jnp.where(tri, 1.0, 0.0).astype(BF16)) + base_ref[...]
    ranks = [jnp.sum(jnp.where(pick, before, 0.0), axis=0, keepdims=True).astype(jnp.int32) for pick in picks]
    base_ref[...] += jnp.sum(onehot, axis=1, keepdims=True)
    return idxs, ws, ranks


def _norm_kernel(*refs, has_resid, modulated, with_router, out_dtype):
    it = iter(refs)
    x_ref = next(it)
    if has_resid:
        o_ref = next(it)
        gate_ref = next(it)
    g_ref = next(it)
    if modulated:
        shift_ref = next(it)
        scale_ref = next(it)
    if with_router:
        rwt_ref = next(it)
        rb_ref = next(it)
    if has_resid:
        xn_ref = next(it)
    h_ref = next(it)
    if with_router:
        hp_ref = next(it)
        eidx_ref = next(it)
        ew_ref = next(it)
        rank_ref = next(it)
        cnt_ref = next(it)
        base_ref = next(it)

        @pl.when(pl.program_id(0) == 0)
        def _():
            base_ref[...] = jnp.zeros_like(base_ref)

    x = x_ref[...]
    if has_resid:
        x = x + gate_ref[...] * o_ref[...].astype(F32)
        xn_ref[...] = x
    y = x * lax.rsqrt(jnp.mean(x * x, axis=-1, keepdims=True) + EPS)
    y = y * g_ref[...]
    if modulated:
        y = y * (1.0 + scale_ref[...]) + shift_ref[...]
    h_ref[...] = y.astype(out_dtype)
    if with_router:
        hp_ref[...] = _pack_bf16_pairs(y)
        idxs, ws, ranks = _router_select(y, rwt_ref, rb_ref, base_ref)
        for k in range(MOE_TOP_K):
            eidx_ref[k:k + 1, :] = idxs[k]
            ew_ref[k:k + 1, :] = ws[k]
            rank_ref[k:k + 1, :] = ranks[k]
        cnt_ref[...] = base_ref[...].astype(jnp.int32)


def _norm_call(x, n_rows, gain, *, resid=None, mod=None, router=None, out_dtype=BF16):
    d = x.shape[1]
    tm = ROW_TILE
    nblk = n_rows // tm
    lat_blocks = SEQ // tm

    def cls_map(i):
        return (jnp.where(i >= lat_blocks, 1, 0), 0, 0)

    row = pl.BlockSpec((tm, d), lambda i: (i, 0))
    vec = pl.BlockSpec((None, 1, d), cls_map)
    args, in_specs = [x], [row]
    if resid is not None:
        o, gate = resid
        args += [o, gate.reshape(2, 1, d)]
        in_specs += [row, vec]
    args.append(gain.reshape(1, d))
    in_specs.append(pl.BlockSpec((1, d), lambda i: (0, 0)))
    if mod is not None:
        shift, scale = mod
        args += [shift.reshape(2, 1, d), scale.reshape(2, 1, d)]
        in_specs += [vec, vec]
    if router is not None:
        rw, rb = router
        args += [rw.T, rb.reshape(N_EXPERTS, 1)]
        in_specs += [pl.BlockSpec((N_EXPERTS, d), lambda i: (0, 0)),
                     pl.BlockSpec((N_EXPERTS, 1), lambda i: (0, 0))]
    out_shape, out_specs = [], []
    if resid is not None:
        out_shape.append(jax.ShapeDtypeStruct((n_rows, d), F32))
        out_specs.append(row)
    out_shape.append(jax.ShapeDtypeStruct((n_rows, d), out_dtype))
    out_specs.append(row)
    if router is not None:
        out_shape += [jax.ShapeDtypeStruct((n_rows, d // 2), jnp.uint32),
                      jax.ShapeDtypeStruct((MOE_TOP_K, n_rows), jnp.int32),
                      jax.ShapeDtypeStruct((MOE_TOP_K, n_rows), F32),
                      jax.ShapeDtypeStruct((MOE_TOP_K, n_rows), jnp.int32),
                      jax.ShapeDtypeStruct((N_EXPERTS, 1), jnp.int32)]
        topk = pl.BlockSpec((MOE_TOP_K, tm), lambda i: (0, i))
        out_specs += [pl.BlockSpec((tm, d // 2), lambda i: (i, 0)), topk, topk, topk,
                      pl.BlockSpec((N_EXPERTS, 1), lambda i: (0, 0))]
    return pl.pallas_call(
        functools.partial(_norm_kernel, has_resid=resid is not None, modulated=mod is not None,
                          with_router=router is not None, out_dtype=out_dtype),
        grid=(nblk,),
        in_specs=in_specs,
        out_specs=out_specs,
        out_shape=out_shape,
        scratch_shapes=[pltpu.VMEM((N_EXPERTS, 1), F32)] if router is not None else [],
        compiler_params=_cparams(("arbitrary",), 16 * tm * d * 4),
        name="norm_router" if router is not None else "norm",
    )(*args)


def _mm_kernel(a_ref, w_ref, o_ref, wb_ref):
    @pl.when(pl.program_id(1) == 0)
    def _():
        wb_ref[...] = w_ref[...].astype(BF16)

    o_ref[...] = _dot(a_ref[...], wb_ref[...]).astype(o_ref.dtype)


def _matmul(a, w, m_rows, out_dtype, tm, tn):
    k, n = w.shape
    assert m_rows % tm == 0 and n % tn == 0 and a.shape[1] == k
    osz = jnp.dtype(out_dtype).itemsize
    vmem = 2 * k * tn * 4 + k * tn * 2 + 2 * tm * k * 2 + 2 * tm * tn * osz + tm * tn * 4
    return pl.pallas_call(
        _mm_kernel,
        grid=(n // tn, m_rows // tm),
        in_specs=[pl.BlockSpec((tm, k), lambda j, i: (i, 0)),
                  pl.BlockSpec((k, tn), lambda j, i: (0, j))],
        out_specs=pl.BlockSpec((tm, tn), lambda j, i: (i, j)),
        out_shape=jax.ShapeDtypeStruct((m_rows, n), out_dtype),
        scratch_shapes=[pltpu.VMEM((k, tn), BF16)],
        compiler_params=_cparams(("arbitrary", "arbitrary"), vmem + (4 << 20)),
        name="matmul",
    )(a, w)


def _rope(x, cos_e, sin_s):
    w = x.shape[-1]
    lane = lax.broadcasted_iota(jnp.int32, x.shape, 1)
    nxt = pltpu.roll(x, w - 1, 1)
    prv = pltpu.roll(x, 1, 1)
    swapped = jnp.where(lane % 2 == 0, nxt, prv)
    return x * cos_e + swapped * sin_s


def _rms(x, g):
    return x * lax.rsqrt(jnp.mean(x * x, axis=-1, keepdims=True) + EPS) * g


def _mla_prep1_kernel(lat_ref, qg_ref, kvg_ref, cos_ref, sin_ref, nq_ref, nkv_ref, kr_ref):
    lat = lat_ref[...]
    nq_ref[...] = _rms(lat[:, :MLA_Q_RANK], qg_ref[...]).astype(BF16)
    nkv_ref[...] = _rms(lat[:, MLA_Q_RANK:MLA_Q_RANK + MLA_KV_RANK], kvg_ref[...]).astype(BF16)
    kr = lat[:, MLA_Q_RANK + MLA_KV_RANK:]
    kr_ref[...] = _rope(kr, cos_ref[...], sin_ref[...]).astype(BF16)


def _mla_prep1(lat, q_norm, kv_norm, cos_t, sin_t):
    n, wdt = lat.shape
    tm = ROW_TILE
    row = lambda w: pl.BlockSpec((tm, w), lambda i: (i, 0))
    one = lambda w: pl.BlockSpec((1, w), lambda i: (0, 0))
    return pl.pallas_call(
        _mla_prep1_kernel,
        grid=(n // tm,),
        in_specs=[row(wdt), one(MLA_Q_RANK), one(MLA_KV_RANK), row(128), row(128)],
        out_specs=[row(MLA_Q_RANK), row(MLA_KV_RANK), row(128)],
        out_shape=[jax.ShapeDtypeStruct((n, MLA_Q_RANK), BF16),
                   jax.ShapeDtypeStruct((n, MLA_KV_RANK), BF16),
                   jax.ShapeDtypeStruct((n, 128), BF16)],
        compiler_params=_cparams(("arbitrary",), 8 << 20),
        name="mla_prep1",
    )(lat, q_norm.reshape(1, -1), kv_norm.reshape(1, -1), cos_t, sin_t)


def _mla_prep2_kernel(q_ref, kn_ref, kr_ref, cos_ref, sin_ref, qo_ref, ko_ref, *, scale):
    cos_e = cos_ref[...]
    sin_s = sin_ref[...]
    kr = kr_ref[...]
    for h in range(MLA_HEADS):
        lo = h * MLA_QK_PAD
        qo_ref[:, lo:lo + MLA_NOPE] = (q_ref[:, lo:lo + MLA_NOPE].astype(F32) * scale).astype(BF16)
        qr = q_ref[:, lo + MLA_NOPE:lo + MLA_QK_PAD].astype(F32)
        qo_ref[:, lo + MLA_NOPE:lo + MLA_QK_PAD] = (_rope(qr, cos_e, sin_s) * scale).astype(BF16)
        ko_ref[:, lo:lo + MLA_NOPE] = kn_ref[:, h * MLA_NOPE:(h + 1) * MLA_NOPE]
        ko_ref[:, lo + MLA_NOPE:lo + MLA_QK_PAD] = kr


def _mla_prep2(q_raw, kv_all, kr, cos_t, sin_t):
    n = q_raw.shape[0]
    tm = ROW_TILE
    wq = MLA_HEADS * MLA_QK_PAD
    row = lambda w: pl.BlockSpec((tm, w), lambda i: (i, 0))
    return pl.pallas_call(
        functools.partial(_mla_prep2_kernel, scale=(MLA_NOPE + MLA_ROPE) ** -0.5 * LOG2E),
        grid=(n // tm,),
        in_specs=[row(wq), row(MLA_HEADS * MLA_NOPE), row(128), row(128), row(128)],
        out_specs=[row(wq), row(wq)],
        out_shape=[jax.ShapeDtypeStruct((n, wq), BF16), jax.ShapeDtypeStruct((n, wq), BF16)],
        compiler_params=_cparams(("arbitrary",), 16 << 20),
        name="mla_prep2",
    )(q_raw, kv_all, kr, cos_t, sin_t)


def _gqa_prep_kernel(qkv_ref, qg_ref, kg_ref, cos_ref, sin_ref, qo_ref, ko_ref, *, scale):
    cos_e = cos_ref[...]
    sin_s = sin_ref[...]
    for h in range(GQA_HEADS):
        xh = qkv_ref[:, h * HEAD_DIM:(h + 1) * HEAD_DIM].astype(F32)
        qo_ref[:, h * HEAD_DIM:(h + 1) * HEAD_DIM] = (
            _rope(_rms(xh, qg_ref[...]), cos_e, sin_s) * scale).astype(BF16)
    for h in range(GQA_KV):
        lo = (GQA_HEADS + h) * HEAD_DIM
        xh = qkv_ref[:, lo:lo + HEAD_DIM].astype(F32)
        ko_ref[:, h * HEAD_DIM:(h + 1) * HEAD_DIM] = _rope(_rms(xh, kg_ref[...]), cos_e, sin_s).astype(BF16)


def _gqa_prep(qkv, q_norm, k_norm, cos_t, sin_t):
    n = qkv.shape[0]
    tm = ROW_TILE
    wqk = (GQA_HEADS + GQA_KV) * HEAD_DIM
    row = lambda w: pl.BlockSpec((tm, w), lambda i: (i, 0))
    one = pl.BlockSpec((1, HEAD_DIM), lambda i: (0, 0))
    return pl.pallas_call(
        functools.partial(_gqa_prep_kernel, scale=HEAD_DIM ** -0.5 * LOG2E),
        grid=(n // tm,),
        in_specs=[row(wqk), one, one, row(HEAD_DIM), row(HEAD_DIM)],
        out_specs=[row(GQA_HEADS * HEAD_DIM), row(GQA_KV * HEAD_DIM)],
        out_shape=[jax.ShapeDtypeStruct((n, GQA_HEADS * HEAD_DIM), BF16),
                   jax.ShapeDtypeStruct((n, GQA_KV * HEAD_DIM), BF16)],
        compiler_params=_cparams(("arbitrary",), 16 << 20),
        name="gqa_prep",
    )(qkv, q_norm.reshape(1, -1), k_norm.reshape(1, -1), cos_t, sin_t)


def _flash_kernel(q_ref, k_ref, v_ref, o_ref, sa_ref, sb_ref, *, group, dk, dv, tk, n_lat, q_scale):
    tq = q_ref.shape[0]
    q = jnp.concatenate([q_ref[:, g * dk:(g + 1) * dk] for g in range(group)], axis=0) if group > 1 else q_ref[...]
    if q_scale is not None:
        q = (q.astype(F32) * q_scale).astype(BF16)

    s = _dot_nt(q, k_ref[SEQ:SEQ + CTX_LEN, :])
    m = jnp.max(s, axis=-1, keepdims=True)
    p = jnp.exp2(s - m)
    l = jnp.sum(p, axis=-1, keepdims=True)
    acc = _dot(p.astype(BF16), v_ref[SEQ:SEQ + CTX_LEN, :])

    if n_lat:
        def rows(ref, j):
            return ref[pl.ds(pl.multiple_of(j * tk, tk), tk), :]

        def update(s, vb, m, l, acc):
            m_new = jnp.maximum(m, jnp.max(s, axis=-1, keepdims=True))
            alpha = jnp.exp2(m - m_new)
            p = jnp.exp2(s - m_new)
            l = alpha * l + jnp.sum(p, axis=-1, keepdims=True)
            return m_new, l, alpha * acc + _dot(p.astype(BF16), vb)

        def pair(jj, carry, last):
            j0 = 2 * jj
            sb_ref[...] = _dot_nt(q, rows(k_ref, j0 + 1))
            carry = update(sa_ref[...], rows(v_ref, j0), *carry)
            if not last:
                sa_ref[...] = _dot_nt(q, rows(k_ref, j0 + 2))
            return update(sb_ref[...], rows(v_ref, j0 + 1), *carry)

        sa_ref[...] = _dot_nt(q, rows(k_ref, 0))
        carry = lax.fori_loop(0, n_lat // 2 - 1, lambda jj, c: pair(jj, c, False), (m, l, acc))
        m, l, acc = pair(n_lat // 2 - 1, carry, True)

    o = (acc / l).astype(o_ref.dtype)
    for g in range(group):
        o_ref[:, g * dv:(g + 1) * dv] = o[g * tq:(g + 1) * tq]


def _flash(q_arr, k_arr, v_arr, *, n_kv_heads, group, dk, dv, q_col0, k_col0, v_col0, ctx_queries, tq, tk,
           q_scale=None):
    n_tok = k_arr.shape[0]
    if ctx_queries:
        n_q, q_blk0, n_lat = CTX_LEN, SEQ // tq, 0
    else:
        n_q, q_blk0, n_lat = SEQ, 0, SEQ // tk
        assert n_lat % 2 == 0
    m = group * tq
    s_shape = (m, tk) if n_lat else (8, 128)
    vmem = (2 * n_tok * (dk + dv) * 2 + 4 * tq * group * (dk + dv) * 2
            + 5 * m * max(tk, CTX_LEN) * 4 + 4 * m * dv * 4)
    return pl.pallas_call(
        functools.partial(_flash_kernel, group=group, dk=dk, dv=dv, tk=tk, n_lat=n_lat, q_scale=q_scale),
        grid=(n_kv_heads, n_q // tq),
        in_specs=[pl.BlockSpec((tq, group * dk), lambda h, i: (q_blk0 + i, q_col0 + h)),
                  pl.BlockSpec((n_tok, dk), lambda h, i: (0, k_col0 + h)),
                  pl.BlockSpec((n_tok, dv), lambda h, i: (0, v_col0 + h))],
        out_specs=pl.BlockSpec((tq, group * dv), lambda h, i: (i, h)),
        out_shape=jax.ShapeDtypeStruct((n_q, n_kv_heads * group * dv), BF16),
        scratch_shapes=[pltpu.VMEM(s_shape, F32), pltpu.VMEM(s_shape, F32)],
        compiler_params=_cparams(("arbitrary", "arbitrary"), vmem + (8 << 20)),
        name="flash_ctx" if ctx_queries else "flash",
    )(q_arr, k_arr, v_arr)


def _na_start_row(b):
    return jnp.clip(NA_RB * b - NA_KH // 2, 0, GRID_H - NA_U)


def _na_kernel(q_ref, k_ref, v_ref, bias_ref, o_ref, *, scale):
    b = pl.program_id(1)
    start = pl.multiple_of(_na_start_row(b) * GRID_W, GRID_W)
    nwin = NA_U * GRID_W
    q = (q_ref[...].astype(F32) * scale).astype(BF16)
    s_loc = _dot_nt(q, k_ref[pl.ds(start, nwin), :]) + bias_ref[...]
    s_ctx = _dot_nt(q, k_ref[SEQ:SEQ + CTX_LEN, :])
    m = jnp.maximum(jnp.max(s_loc, axis=-1, keepdims=True), jnp.max(s_ctx, axis=-1, keepdims=True))
    p_loc = jnp.exp(s_loc - m)
    p_ctx = jnp.exp(s_ctx - m)
    l = jnp.sum(p_loc, axis=-1, keepdims=True) + jnp.sum(p_ctx, axis=-1, keepdims=True)
    o = _dot(p_loc.astype(BF16), v_ref[pl.ds(start, nwin), :]) + _dot(p_ctx.astype(BF16), v_ref[SEQ:SEQ + CTX_LEN, :])
    o_ref[...] = (o / l).astype(o_ref.dtype)


def _na_bias_tables(rpb):
    h = rpb.shape[0]
    w = np.arange(GRID_W)
    c0 = np.clip(w - NA_KW // 2, 0, GRID_W - NA_KW)
    colmask = (w[None, :] >= c0[:, None]) & (w[None, :] < c0[:, None] + NA_KW)
    colidx = np.clip(w[None, :] - w[:, None] + NA_KW - 1, 0, 2 * NA_KW - 2)
    slabs = jnp.where(colmask[None, None], rpb[:, :, colidx], NEG)
    slabs = jnp.concatenate([slabs, jnp.full((h, 1, GRID_W, GRID_W), NEG, rpb.dtype)], axis=1)
    masked = 2 * NA_KH - 1
    nb = GRID_H // NA_RB
    tabs = []
    for b in (0, 1, nb - 1):
        start = int(np.clip(NA_RB * b - NA_KH // 2, 0, GRID_H - NA_U))
        a = np.full((NA_RB, NA_U), masked, np.int32)
        for j in range(NA_RB):
            r = NA_RB * b + j
            r0 = int(np.clip(r - NA_KH // 2, 0, GRID_H - NA_KH))
            for u in range(NA_U):
                kr = start + u
                if r0 <= kr < r0 + NA_KH:
                    a[j, u] = kr - r + NA_KH - 1
        t = slabs[:, a]
        tabs.append(t.transpose(0, 1, 3, 2, 4).reshape(h, NA_RB * GRID_W, NA_U * GRID_W))
    return jnp.stack(tabs, axis=1)


def _na_attention(qkv, bias_tabs):
    n_tok = qkv.shape[0]
    nb = GRID_H // NA_RB
    tq = NA_RB * GRID_W
    nwin = NA_U * GRID_W

    def cls(b):
        return jnp.where(b == 0, 0, jnp.where(b == nb - 1, 2, 1))

    vmem = 4 * n_tok * HEAD_DIM * 2 + 2 * tq * nwin * 4 + 4 * tq * (nwin + CTX_LEN) * 4
    return pl.pallas_call(
        functools.partial(_na_kernel, scale=HEAD_DIM ** -0.5),
        grid=(NA_HEADS, nb),
        in_specs=[pl.BlockSpec((tq, HEAD_DIM), lambda h, b: (b, h)),
                  pl.BlockSpec((n_tok, HEAD_DIM), lambda h, b: (0, NA_HEADS + h)),
                  pl.BlockSpec((n_tok, HEAD_DIM), lambda h, b: (0, 2 * NA_HEADS + h)),
                  pl.BlockSpec((None, None, tq, nwin), lambda h, b: (h, cls(b), 0, 0))],
        out_specs=pl.BlockSpec((tq, HEAD_DIM), lambda h, b: (b, h)),
        out_shape=jax.ShapeDtypeStruct((SEQ, NA_HEADS * HEAD_DIM), BF16),
        compiler_params=_cparams(("arbitrary", "arbitrary"), vmem + (8 << 20)),
        name="na_attention",
    )(qkv, qkv, qkv, bias_tabs)


def _dispatch_kernel(tail_ref, nu_ref, slot_ref, h_ref, xg_hbm, zbuf, sem, zsem, *, tb, n_blocks):
    te = EXPERT_TILE

    @pl.when(pl.program_id(0) == 0)
    def _():
        zbuf[...] = jnp.zeros_like(zbuf)

        def zero_tile(row0, wait):
            cp = pltpu.make_async_copy(zbuf, xg_hbm.at[pl.ds(pl.multiple_of(row0, te), te)], zsem)
            if wait:
                cp.wait()
            else:
                cp.start()

        def zero_tail(e, wait):
            @pl.when(tail_ref[e] >= 0)
            def _():
                zero_tile(tail_ref[e], wait)

        for wait in (False, True):
            lax.fori_loop(0, N_EXPERTS, lambda e, c: (zero_tail(e, wait), c)[1], 0)
            lax.fori_loop(nu_ref[0], n_blocks, lambda b, c: (zero_tile(b * te, wait), c)[1], 0)

    def body(t, carry):
        for k in range(MOE_TOP_K):
            pltpu.make_async_copy(h_ref.at[pl.ds(t, 1)], xg_hbm.at[pl.ds(slot_ref[k, t], 1)], sem).start(
                priority=k % 2)
        return carry

    lax.fori_loop(0, tb, body, 0, unroll=DMA_LOOP_UNROLL)
    for k in range(MOE_TOP_K):
        pltpu.make_async_copy(h_ref, xg_hbm.at[pl.ds(0, tb)], sem).wait()


def _dispatch(h32, slot_t, tail_start, n_used, n_rows, n_slots):
    d = h32.shape[1]
    tb = ROW_TILE
    grid_spec = pltpu.PrefetchScalarGridSpec(
        num_scalar_prefetch=2,
        grid=(n_rows // tb,),
        in_specs=[pl.BlockSpec((MOE_TOP_K, tb), lambda i, tl, nu: (0, i), memory_space=pltpu.SMEM),
                  pl.BlockSpec((tb, d), lambda i, tl, nu: (i, 0))],
        out_specs=pl.BlockSpec(memory_space=pl.ANY),
        scratch_shapes=[pltpu.VMEM((EXPERT_TILE, d), h32.dtype), pltpu.SemaphoreType.DMA(()),
                        pltpu.SemaphoreType.DMA(())],
    )
    return pl.pallas_call(
        functools.partial(_dispatch_kernel, tb=tb, n_blocks=n_slots // EXPERT_TILE),
        grid_spec=grid_spec,
        out_shape=jax.ShapeDtypeStruct((n_slots, d), h32.dtype),
        compiler_params=_cparams(("arbitrary",), 16 << 20),
        name="moe_dispatch",
    )(tail_start, n_used, slot_t, h32)


FFN_LEAD = 2


def _ffn_kernel(be_ref, ord_ref, nu_ref, x_ref, wg_ref, wu_ref, wd_ref, o_ref, wgb, wub, wdb, *, packed, nb):
    b = pl.program_id(0)
    c = b - FFN_LEAD

    def at(ref, i):
        return ref[jnp.clip(i, 0, nb - 1)]

    def starts_expert(i):
        return jnp.logical_or(b == 0, at(be_ref, i) != at(be_ref, i - 1))

    @pl.when(starts_expert(c))
    def _():
        wdb[at(ord_ref, c) % 2] = wd_ref[...].astype(BF16)

    @pl.when(jnp.logical_and(c >= 0, c < nu_ref[0]))
    def _():
        slot = at(ord_ref, c) % 2
        x = _unpack_bf16_pairs(x_ref[...]).astype(BF16) if packed else x_ref[...]
        g = _dot(x, wgb[slot])
        u = _dot(x, wub[slot])
        a = (g * jax.nn.sigmoid(g) * u).astype(BF16)
        y = _dot(a, wdb[slot])
        o_ref[...] = _pack_bf16_pairs(y) if packed else y.astype(o_ref.dtype)

    @pl.when(c >= nu_ref[0])
    def _():
        o_ref[...] = jnp.zeros_like(o_ref)

    @pl.when(starts_expert(b))
    def _():
        wgb[at(ord_ref, b) % 2] = wg_ref[...].astype(BF16)

    @pl.when(starts_expert(b - 1))
    def _():
        wub[at(ord_ref, b - 1) % 2] = wu_ref[...].astype(BF16)


def _expert_ffn(x_rows, block_expert, n_used, wg, wu, wd, layer, tb, packed):
    p, dx = x_rows.shape
    d, de = wg.shape[-2:]
    nb = p // tb
    out_dtype = x_rows.dtype
    vmem = 2 * 3 * d * de * 4 + 2 * 3 * d * de * 2 + 4 * tb * d * 2 + 4 * tb * de * 4 + 3 * tb * d * 4
    idx = jnp.arange(nb, dtype=jnp.int32)
    be = jnp.where(idx < n_used[0], block_expert, block_expert[jnp.maximum(n_used[0] - 1, 0)])
    ordinal = jnp.cumsum(jnp.concatenate([jnp.zeros((1,), jnp.int32),
                                          (be[1:] != be[:-1]).astype(jnp.int32)])).astype(jnp.int32)

    def blk(i):
        return jnp.clip(i, 0, nb - 1)

    def wspec(shape, lag):
        return pl.BlockSpec((None, None) + shape, lambda b, be, od, nu: (layer, be[blk(b - lag)], 0, 0))

    grid_spec = pltpu.PrefetchScalarGridSpec(
        num_scalar_prefetch=3,
        grid=(nb + FFN_LEAD,),
        in_specs=[pl.BlockSpec((tb, dx), lambda b, be, od, nu: (jnp.minimum(blk(b - FFN_LEAD), nu[0] - 1), 0)),
                  wspec((d, de), 0), wspec((d, de), 1), wspec((de, d), 2)],
        out_specs=pl.BlockSpec((tb, dx), lambda b, be, od, nu: (blk(b - FFN_LEAD), 0)),
        scratch_shapes=[pltpu.VMEM((2, d, de), BF16), pltpu.VMEM((2, d, de), BF16), pltpu.VMEM((2, de, d), BF16)],
    )
    return pl.pallas_call(
        functools.partial(_ffn_kernel, packed=packed, nb=nb),
        grid_spec=grid_spec,
        out_shape=jax.ShapeDtypeStruct((p, dx), out_dtype),
        compiler_params=_cparams(("arbitrary",), vmem + (4 << 20)),
        name="expert_ffn",
    )(be, ordinal, n_used, x_rows, wg, wu, wd)


def _combine_kernel(s_cur, s_nxt, y_hbm, ew_ref, ysh_ref, x_ref, gate_ref, o_ref, buf, sem, *, tc, nblk):
    i = pl.program_id(0)

    def gather(slot_ref, bslot):
        def body(t, carry):
            for k in range(MOE_TOP_K):
                pltpu.make_async_copy(y_hbm.at[pl.ds(slot_ref[k, t], 1)],
                                      buf.at[bslot, k, pl.ds(t, 1)], sem.at[bslot]).start()
            return carry

        lax.fori_loop(0, tc, body, 0, unroll=DMA_LOOP_UNROLL)

    @pl.when(i == 0)
    def _():
        gather(s_cur, 0)

    cur = i % 2

    @pl.when(i + 1 < nblk)
    def _():
        @pl.when(cur == 0)
        def _():
            gather(s_nxt, 1)

        @pl.when(cur == 1)
        def _():
            gather(s_nxt, 0)

    def finish(bslot):
        for k in range(MOE_TOP_K):
            pltpu.make_async_copy(y_hbm.at[pl.ds(0, tc)], buf.at[bslot, k], sem.at[bslot]).wait()
        ew = ew_ref[...]
        acc = ysh_ref[...].astype(F32)
        for k in range(MOE_TOP_K):
            acc = acc + ew[:, k:k + 1] * _unpack_bf16_pairs(buf[bslot, k])
        o_ref[...] = x_ref[...] + gate_ref[...] * acc

    @pl.when(cur == 0)
    def _():
        finish(0)

    @pl.when(cur == 1)
    def _():
        finish(1)


def _combine(y, slot_t, ew, ysh, x, gate, n_rows):
    d = x.shape[1]
    tc = COMBINE_TILE
    nblk = n_rows // tc
    lat_blocks = SEQ // tc
    row = pl.BlockSpec((tc, d), lambda i: (i, 0))
    vmem = 2 * MOE_TOP_K * tc * d * 4 + 8 * tc * d * 4
    return pl.pallas_call(
        functools.partial(_combine_kernel, tc=tc, nblk=nblk),
        grid=(nblk,),
        in_specs=[pl.BlockSpec((MOE_TOP_K, tc), lambda i: (0, i), memory_space=pltpu.SMEM),
                  pl.BlockSpec((MOE_TOP_K, tc), lambda i: (0, jnp.minimum(i + 1, nblk - 1)),
                               memory_space=pltpu.SMEM),
                  pl.BlockSpec(memory_space=pl.ANY),
                  pl.BlockSpec((tc, MOE_TOP_K), lambda i: (i, 0)),
                  row, row,
                  pl.BlockSpec((None, 1, d), lambda i: (jnp.where(i >= lat_blocks, 1, 0), 0, 0))],
        out_specs=row,
        out_shape=jax.ShapeDtypeStruct((n_rows, d), F32),
        scratch_shapes=[pltpu.VMEM((2, MOE_TOP_K, tc, d // 2), jnp.uint32), pltpu.SemaphoreType.DMA((2,))],
        compiler_params=_cparams(("arbitrary",), vmem + (4 << 20)),
        name="moe_combine",
    )(slot_t, slot_t, y, ew, ysh, x, gate.reshape(2, 1, d))


def _routing_tables(counts, n_rows):
    tb = EXPERT_TILE
    n_blocks = -(-(n_rows * MOE_TOP_K) // tb) + N_EXPERTS
    padded = (counts + tb - 1) // tb * tb
    pad_end = jnp.cumsum(padded)
    pad_start = (pad_end - padded).astype(jnp.int32)
    tail_start = jnp.where(padded > 0, pad_end - tb, -1).astype(jnp.int32)
    blk_start = jnp.arange(n_blocks, dtype=jnp.int32) * tb
    block_expert = jnp.sum(blk_start[:, None] >= pad_end[None, :], axis=1)
    block_expert = jnp.minimum(block_expert, N_EXPERTS - 1).astype(jnp.int32)
    n_used = (pad_end[-1] // tb).astype(jnp.int32).reshape(1)
    return pad_start, tail_start, block_expert, n_used, n_blocks * tb


def _moe(layer, x_new, h, h32, eidx_t, ew_t, rank_t, counts, gate, n_rows, wg, wu, wd, sg, su, sd):
    pad_start, tail_start, block_expert, n_used, n_slots = _routing_tables(counts[:, 0], n_rows)
    e_iota = jnp.arange(N_EXPERTS, dtype=jnp.int32)
    slot_t = jnp.sum(jnp.where(eidx_t[:, :, None] == e_iota, pad_start, 0), axis=-1) + rank_t
    xg = _dispatch(h32, slot_t, tail_start, n_used, n_rows, n_slots)
    y = _expert_ffn(xg, block_expert, n_used, wg, wu, wd, layer, EXPERT_TILE, True)
    tm = MM_TM if n_rows % MM_TM == 0 else 512
    ysh = _expert_ffn(h, jnp.zeros((n_rows // tm,), jnp.int32), jnp.full((1,), n_rows // tm, jnp.int32),
                      sg[:, None], su[:, None], sd[:, None], layer, tm, False)
    return _combine(y, slot_t, ew_t.T, ysh, x_new, gate, n_rows)


def _rope_tables(d, width):
    t = jnp.arange(SEQ)
    row = (t // GRID_W).astype(F32)
    col = (t % GRID_W).astype(F32)
    n = d // 4
    inv = ROPE_THETA ** (-jnp.arange(n, dtype=F32) / n)
    ang = jnp.concatenate([row[:, None] * inv, col[:, None] * inv], axis=-1)
    ang = jnp.concatenate([ang, jnp.zeros((CTX_LEN, d // 2), F32)], axis=0)
    cos = jnp.repeat(jnp.cos(ang), 2, axis=1)
    sin = jnp.repeat(jnp.sin(ang), 2, axis=1) * jnp.tile(jnp.array([-1.0, 1.0], F32), d // 2)
    if width > d:
        cos = jnp.concatenate([cos, jnp.ones((N_TOK, width - d), F32)], axis=1)
        sin = jnp.concatenate([sin, jnp.zeros((N_TOK, width - d), F32)], axis=1)
    return cos, sin


def _mixer_even(h, w_in, rpb, q_norm, w_uq, kv_norm, w_ukv, w_out):
    n_na = 3 * NA_HEADS * HEAD_DIM
    qkv = _matmul(h, w_in[:, :n_na], N_TOK, BF16, MM_TM, 1024)
    w_tail = jnp.pad(w_in[:, n_na:], ((0, 0), (0, 64)))
    lat = _matmul(h, w_tail, N_TOK, F32, MM_TM, 384)
    cos_t, sin_t = _rope_tables(MLA_ROPE, 128)
    nq, nkv, kr = _mla_prep1(lat, q_norm, kv_norm, cos_t, sin_t)
    wq = w_uq.reshape(MLA_Q_RANK, MLA_HEADS, MLA_NOPE + MLA_ROPE)
    wq = jnp.pad(wq, ((0, 0), (0, 0), (0, MLA_QK_PAD - MLA_NOPE - MLA_ROPE))).reshape(MLA_Q_RANK, -1)
    wkv = w_ukv.reshape(MLA_KV_RANK, MLA_HEADS, MLA_NOPE + MLA_V)
    wkv = jnp.concatenate([wkv[:, :, :MLA_NOPE].reshape(MLA_KV_RANK, -1),
                           wkv[:, :, MLA_NOPE:].reshape(MLA_KV_RANK, -1)], axis=1)
    q_raw = _matmul(nq, wq, N_TOK, F32, MM_TM, 1024)
    kv_all = _matmul(nkv, wkv, N_TOK, BF16, MM_TM, 1024)
    q_mla, k_cat = _mla_prep2(q_raw, kv_all, kr, cos_t, sin_t)

    o_na = _na_attention(qkv, _na_bias_tables(rpb))
    mla = dict(n_kv_heads=MLA_HEADS, group=1, dk=MLA_QK_PAD, dv=MLA_V, q_col0=0, k_col0=0, v_col0=MLA_HEADS)
    o_mla = _flash(q_mla, k_cat, kv_all, ctx_queries=False, tq=1024, tk=1024, **mla)
    o_mla_c = _flash(q_mla, k_cat, kv_all, ctx_queries=True, tq=256, tk=512, **mla)
    na = dict(n_kv_heads=NA_HEADS, group=1, dk=HEAD_DIM, dv=HEAD_DIM, q_col0=0, k_col0=NA_HEADS, v_col0=2 * NA_HEADS)
    o_na_c = _flash(qkv, qkv, qkv, ctx_queries=True, tq=256, tk=512, q_scale=HEAD_DIM ** -0.5 * LOG2E, **na)
    o = jnp.concatenate([jnp.concatenate([o_na, o_mla], axis=1),
                         jnp.concatenate([o_na_c, o_mla_c], axis=1)], axis=0)
    return _matmul(o, w_out, N_TOK, F32, MM_TM, 1024)


def _mixer_odd_last(h, w_qkv, q_norm, k_norm, w_o):
    qkv = _matmul(h, w_qkv, N_TOK, BF16, MM_TM, 1024)
    cos_t, sin_t = _rope_tables(HEAD_DIM, HEAD_DIM)
    q_g, k_g = _gqa_prep(qkv, q_norm, k_norm, cos_t, sin_t)
    o = _flash(q_g, k_g, qkv, n_kv_heads=GQA_KV, group=GQA_GROUP, dk=HEAD_DIM, dv=HEAD_DIM,
               q_col0=0, k_col0=0, v_col0=GQA_HEADS + GQA_KV, ctx_queries=False, tq=256, tk=1024)
    return _matmul(o, w_o, SEQ, F32, 512, 1024)


def kernel(x, c, ctx, c_ctx, ada_w, ada_b, mix_norm, ffn_norm, ev_w_in, ev_rpb, ev_q_norm, ev_w_uq, ev_kv_norm, ev_w_ukv, ev_w_out, od_w_qkv, od_q_norm, od_k_norm, od_w_o, router_w, router_b, exp_w_gate, exp_w_up, exp_w_down, sh_w_gate, sh_w_up, sh_w_down, final_norm):
    assert DEPTH == 2 and x.shape == (1, SEQ, D_MODEL)
    mods = _adaln_mods(c, c_ctx, ada_w, ada_b)
    xt = jnp.concatenate([x[0], ctx[0]], axis=0)

    def parts(i):
        return [mods[i, :, k * D_MODEL:(k + 1) * D_MODEL] for k in range(6)]

    sh1, sc1, g1, sh2, sc2, g2 = parts(0)
    (h,) = _norm_call(xt, N_TOK, mix_norm[0], mod=(sh1, sc1))
    o = _mixer_even(h, ev_w_in[0], ev_rpb[0], ev_q_norm[0], ev_w_uq[0], ev_kv_norm[0], ev_w_ukv[0], ev_w_out[0])
    xt, h, h32, eidx_t, ew_t, rank_t, counts = _norm_call(xt, N_TOK, ffn_norm[0], resid=(o, g1), mod=(sh2, sc2),
                                                          router=(router_w[0], router_b[0]))
    xt = _moe(0, xt, h, h32, eidx_t, ew_t, rank_t, counts, g2, N_TOK, exp_w_gate, exp_w_up, exp_w_down,
              sh_w_gate, sh_w_up, sh_w_down)

    sh1, sc1, g1, sh2, sc2, g2 = parts(1)
    (h,) = _norm_call(xt, N_TOK, mix_norm[1], mod=(sh1, sc1))
    o = _mixer_odd_last(h, od_w_qkv[0], od_q_norm[0], od_k_norm[0], od_w_o[0])
    xl, h, h32, eidx_t, ew_t, rank_t, counts = _norm_call(xt, SEQ, ffn_norm[1], resid=(o, g1), mod=(sh2, sc2),
                                                          router=(router_w[1], router_b[1]))
    xl = _moe(1, xl, h, h32, eidx_t, ew_t, rank_t, counts, g2, SEQ, exp_w_gate, exp_w_up, exp_w_down,
              sh_w_gate, sh_w_up, sh_w_down)
    (out,) = _norm_call(xl, SEQ, final_norm, out_dtype=F32)
    return out[None]
```

```python
import functools

import numpy as np
import jax
import jax.numpy as jnp
from jax import lax
from jax.experimental import pallas as pl
from jax.experimental.pallas import tpu as pltpu

F32 = jnp.float32
BF16 = jnp.bfloat16

D_MODEL = 2048
SEQ = 8192
DEPTH = 2
GRID_W = 64
GRID_H = SEQ // GRID_W
CTX_LEN = 256
N_TOK = SEQ + CTX_LEN
EPS = 1e-6
ROPE_THETA = 10000.0
HEAD_DIM = 128
NA_HEADS = 8
NA_KH = 8
NA_KW = 16
MLA_HEADS = 8
MLA_Q_RANK = 512
MLA_KV_RANK = 512
MLA_NOPE = 128
MLA_ROPE = 64
MLA_V = 128
MLA_QK_PAD = 256
GQA_HEADS = 16
GQA_KV = 4
GQA_GROUP = GQA_HEADS // GQA_KV
N_EXPERTS = 64
N_GROUPS = 8
GROUP_SIZE = N_EXPERTS // N_GROUPS
TOPK_GROUPS = 4
MOE_TOP_K = 8
D_EXPERT = 512
ROUTED_SCALE = 2.5

NEG = -1e30
LOG2E = 1.4426950408889634

VMEM_LIMIT_CAP = 56 * 1024 * 1024

ROW_TILE = 256
MM_TM = 768
NA_RB = 4
NA_U = NA_RB + NA_KH - 1
EXPERT_TILE = 256
COMBINE_TILE = 128
DMA_LOOP_UNROLL = 128


def _cparams(sem, vmem_bytes):
    return pltpu.CompilerParams(
        dimension_semantics=sem,
        vmem_limit_bytes=int(min(max(vmem_bytes, 16 * 1024 * 1024), VMEM_LIMIT_CAP)),
    )


def _pack_bf16_pairs(x):
    w = x.shape[1] // 2

    def rounded(v):
        b = lax.bitcast_convert_type(v, jnp.uint32)
        return b + (((b >> 16) & jnp.uint32(1)) + jnp.uint32(0x7FFF))

    return (rounded(x[:, :w]) >> 16) | (rounded(x[:, w:]) & jnp.uint32(0xFFFF0000))


def _unpack_bf16_pairs(p):
    lo = lax.bitcast_convert_type(p << 16, F32)
    hi = lax.bitcast_convert_type(p & jnp.uint32(0xFFFF0000), F32)
    return jnp.concatenate([lo, hi], axis=1)


def _dot(a, b):
    return jnp.dot(a, b, preferred_element_type=F32)


def _dot_nt(a, b):
    return lax.dot_general(a, b, (((1,), (1,)), ((), ())), preferred_element_type=F32)


def _mods_kernel(cc_ref, w_ref, b_ref, o_ref, acc_ref, *, nk):
    k = pl.program_id(2)

    @pl.when(k == 0)
    def _():
        acc_ref[...] = jnp.zeros_like(acc_ref)

    a = cc_ref[...]
    a = a * jax.nn.sigmoid(a)
    w = w_ref[...]
    tk, tn = w.shape
    w3 = w.reshape(tk // 8, 8, tn)
    for v in range(2):
        col = a[:, v:v + 1].reshape(tk // 8, 8, 1)
        acc_ref[v] += jnp.sum(w3 * col, axis=0)

    @pl.when(k == nk - 1)
    def _():
        o_ref[...] = jnp.sum(acc_ref[...], axis=1) + b_ref[...]


def _adaln_mods(c, c_ctx, ada_w, ada_b):
    depth, d, n = ada_w.shape
    tk, tn = 512, 1536
    nk = d // tk
    cc = jnp.stack([c[0], c_ctx], axis=1)
    return pl.pallas_call(
        functools.partial(_mods_kernel, nk=nk),
        grid=(depth, n // tn, nk),
        in_specs=[
            pl.BlockSpec((tk, 2), lambda i, j, k: (k, 0)),
            pl.BlockSpec((None, tk, tn), lambda i, j, k: (i, k, j)),
            pl.BlockSpec((None, 1, tn), lambda i, j, k: (i, 0, j)),
        ],
        out_specs=pl.BlockSpec((None, 2, tn), lambda i, j, k: (i, 0, j)),
        out_shape=jax.ShapeDtypeStruct((depth, 2, n), F32),
        scratch_shapes=[pltpu.VMEM((2, 8, tn), F32)],
        compiler_params=_cparams(("arbitrary", "arbitrary", "arbitrary"), 4 * tk * tn * 4),
        name="adaln_mods",
    )(cc, ada_w, ada_b.reshape(depth, 1, n))


def _router_select(hf, rwt_ref, rb_ref, base_ref):
    logits = lax.dot_general(rwt_ref[...], hf, (((1,), (1,)), ((), ())),
                             precision=lax.Precision.HIGHEST, preferred_element_type=F32)
    scores = jax.nn.sigmoid(logits)
    sel = scores + rb_ref[...]
    tm = sel.shape[1]
    ninf = jnp.float32(-jnp.inf)
    sub = lax.broadcasted_iota(jnp.int32, (GROUP_SIZE, tm), 0)
    gi = lax.broadcasted_iota(jnp.int32, (N_GROUPS, tm), 0)
    gs = jnp.zeros((N_GROUPS, tm), F32)
    for g in range(N_GROUPS):
        blk = sel[g * GROUP_SIZE:(g + 1) * GROUP_SIZE, :]
        m1 = jnp.max(blk, axis=0, keepdims=True)
        first = jnp.min(jnp.where(blk == m1, sub, GROUP_SIZE), axis=0, keepdims=True)
        m2 = jnp.max(jnp.where(sub == first, ninf, blk), axis=0, keepdims=True)
        gs = jnp.where(gi == g, m1 + m2, gs)
    gsel = jnp.zeros((N_GROUPS, tm), F32)
    cur = gs
    for _ in range(TOPK_GROUPS):
        mx = jnp.max(cur, axis=0, keepdims=True)
        idx = jnp.min(jnp.where(cur == mx, gi, N_GROUPS), axis=0, keepdims=True)
        pick = gi == idx
        gsel = jnp.where(pick, 1.0, gsel)
        cur = jnp.where(pick, ninf, cur)
    ei = lax.broadcasted_iota(jnp.int32, (N_EXPERTS, tm), 0)
    eg = ei // GROUP_SIZE
    emask = jnp.zeros((N_EXPERTS, tm), F32)
    for g in range(N_GROUPS):
        emask = jnp.where(eg == g, gsel[g:g + 1, :], emask)
    cur = jnp.where(emask > 0.0, sel, ninf)
    idxs, ws, picks = [], [], []
    for _ in range(MOE_TOP_K):
        mx = jnp.max(cur, axis=0, keepdims=True)
        idx = jnp.min(jnp.where(cur == mx, ei, N_EXPERTS), axis=0, keepdims=True)
        pick = ei == idx
        ws.append(jnp.sum(jnp.where(pick, scores, 0.0), axis=0, keepdims=True))
        idxs.append(idx)
        picks.append(pick)
        cur = jnp.where(pick, ninf, cur)
    total = ws[0]
    for w in ws[1:]:
        total = total + w
    ws = [w / total * ROUTED_SCALE for w in ws]
    onehot = jnp.zeros((N_EXPERTS, tm), F32)
    for pick in picks:
        onehot = jnp.where(pick, 1.0, onehot)
    tri = (lax.broadcasted_iota(jnp.int32, (tm, tm), 0) < lax.broadcasted_iota(jnp.int32, (tm, tm), 1))
    before = _dot(onehot.astype(BF16), jnp.where(tri, 1.0, 0.0).astype(BF16)) + base_ref[...]
    ranks = [jnp.sum(jnp.where(pick, before, 0.0), axis=0, keepdims=True).astype(jnp.int32) for pick in picks]
    base_ref[...] += jnp.sum(onehot, axis=1, keepdims=True)
    return idxs, ws, ranks


def _norm_kernel(*refs, has_resid, modulated, with_router, out_dtype):
    it = iter(refs)
    x_ref = next(it)
    if has_resid:
        o_ref = next(it)
        gate_ref = next(it)
    g_ref = next(it)
    if modulated:
        shift_ref = next(it)
        scale_ref = next(it)
    if with_router:
        rwt_ref = next(it)
        rb_ref = next(it)
    if has_resid:
        xn_ref = next(it)
    h_ref = next(it)
    if with_router:
        hp_ref = next(it)
        eidx_ref = next(it)
        ew_ref = next(it)
        rank_ref = next(it)
        cnt_ref = next(it)
        base_ref = next(it)

        @pl.when(pl.program_id(0) == 0)
        def _():
            base_ref[...] = jnp.zeros_like(base_ref)

    x = x_ref[...]
    if has_resid:
        x = x + gate_ref[...] * o_ref[...].astype(F32)
        xn_ref[...] = x
    y = x * lax.rsqrt(jnp.mean(x * x, axis=-1, keepdims=True) + EPS)
    y = y * g_ref[...]
    if modulated:
        y = y * (1.0 + scale_ref[...]) + shift_ref[...]
    h_ref[...] = y.astype(out_dtype)
    if with_router:
        hp_ref[...] = _pack_bf16_pairs(y)
        idxs, ws, ranks = _router_select(y, rwt_ref, rb_ref, base_ref)
        for k in range(MOE_TOP_K):
            eidx_ref[k:k + 1, :] = idxs[k]
            ew_ref[k:k + 1, :] = ws[k]
            rank_ref[k:k + 1, :] = ranks[k]
        cnt_ref[...] = base_ref[...].astype(jnp.int32)


def _norm_call(x, n_rows, gain, *, resid=None, mod=None, router=None, out_dtype=BF16):
    d = x.shape[1]
    tm = ROW_TILE
    nblk = n_rows // tm
    lat_blocks = SEQ // tm

    def cls_map(i):
        return (jnp.where(i >= lat_blocks, 1, 0), 0, 0)

    row = pl.BlockSpec((tm, d), lambda i: (i, 0))
    vec = pl.BlockSpec((None, 1, d), cls_map)
    args, in_specs = [x], [row]
    if resid is not None:
        o, gate = resid
        args += [o, gate.reshape(2, 1, d)]
        in_specs += [row, vec]
    args.append(gain.reshape(1, d))
    in_specs.append(pl.BlockSpec((1, d), lambda i: (0, 0)))
    if mod is not None:
        shift, scale = mod
        args += [shift.reshape(2, 1, d), scale.reshape(2, 1, d)]
        in_specs += [vec, vec]
    if router is not None:
        rw, rb = router
        args += [rw.T, rb.reshape(N_EXPERTS, 1)]
        in_specs += [pl.BlockSpec((N_EXPERTS, d), lambda i: (0, 0)),
                     pl.BlockSpec((N_EXPERTS, 1), lambda i: (0, 0))]
    out_shape, out_specs = [], []
    if resid is not None:
        out_shape.append(jax.ShapeDtypeStruct((n_rows, d), F32))
        out_specs.append(row)
    out_shape.append(jax.ShapeDtypeStruct((n_rows, d), out_dtype))
    out_specs.append(row)
    if router is not None:
        out_shape += [jax.ShapeDtypeStruct((n_rows, d // 2), jnp.uint32),
                      jax.ShapeDtypeStruct((MOE_TOP_K, n_rows), jnp.int32),
                      jax.ShapeDtypeStruct((MOE_TOP_K, n_rows), F32),
                      jax.ShapeDtypeStruct((MOE_TOP_K, n_rows), jnp.int32),
                      jax.ShapeDtypeStruct((N_EXPERTS, 1), jnp.int32)]
        topk = pl.BlockSpec((MOE_TOP_K, tm), lambda i: (0, i))
        out_specs += [pl.BlockSpec((tm, d // 2), lambda i: (i, 0)), topk, topk, topk,
                      pl.BlockSpec((N_EXPERTS, 1), lambda i: (0, 0))]
    return pl.pallas_call(
        functools.partial(_norm_kernel, has_resid=resid is not None, modulated=mod is not None,
                          with_router=router is not None, out_dtype=out_dtype),
        grid=(nblk,),
        in_specs=in_specs,
        out_specs=out_specs,
        out_shape=out_shape,
        scratch_shapes=[pltpu.VMEM((N_EXPERTS, 1), F32)] if router is not None else [],
        compiler_params=_cparams(("arbitrary",), 16 * tm * d * 4),
        name="norm_router" if router is not None else "norm",
    )(*args)


def _mm_kernel(a_ref, w_ref, o_ref, wb_ref):
    @pl.when(pl.program_id(1) == 0)
    def _():
        wb_ref[...] = w_ref[...].astype(BF16)

    o_ref[...] = _dot(a_ref[...], wb_ref[...]).astype(o_ref.dtype)


def _matmul(a, w, m_rows, out_dtype, tm, tn):
    k, n = w.shape
    assert m_rows % tm == 0 and n % tn == 0 and a.shape[1] == k
    osz = jnp.dtype(out_dtype).itemsize
    vmem = 2 * k * tn * 4 + k * tn * 2 + 2 * tm * k * 2 + 2 * tm * tn * osz + tm * tn * 4
    return pl.pallas_call(
        _mm_kernel,
        grid=(n // tn, m_rows // tm),
        in_specs=[pl.BlockSpec((tm, k), lambda j, i: (i, 0)),
                  pl.BlockSpec((k, tn), lambda j, i: (0, j))],
        out_specs=pl.BlockSpec((tm, tn), lambda j, i: (i, j)),
        out_shape=jax.ShapeDtypeStruct((m_rows, n), out_dtype),
        scratch_shapes=[pltpu.VMEM((k, tn), BF16)],
        compiler_params=_cparams(("arbitrary", "arbitrary"), vmem + (4 << 20)),
        name="matmul",
    )(a, w)


def _rope(x, cos_e, sin_s):
    w = x.shape[-1]
    lane = lax.broadcasted_iota(jnp.int32, x.shape, 1)
    nxt = pltpu.roll(x, w - 1, 1)
    prv = pltpu.roll(x, 1, 1)
    swapped = jnp.where(lane % 2 == 0, nxt, prv)
    return x * cos_e + swapped * sin_s


def _rms(x, g):
    return x * lax.rsqrt(jnp.mean(x * x, axis=-1, keepdims=True) + EPS) * g


def _mla_prep1_kernel(lat_ref, qg_ref, kvg_ref, cos_ref, sin_ref, nq_ref, nkv_ref, kr_ref):
    lat = lat_ref[...]
    nq_ref[...] = _rms(lat[:, :MLA_Q_RANK], qg_ref[...]).astype(BF16)
    nkv_ref[...] = _rms(lat[:, MLA_Q_RANK:MLA_Q_RANK + MLA_KV_RANK], kvg_ref[...]).astype(BF16)
    kr = lat[:, MLA_Q_RANK + MLA_KV_RANK:]
    kr_ref[...] = _rope(kr, cos_ref[...], sin_ref[...]).astype(BF16)


def _mla_prep1(lat, q_norm, kv_norm, cos_t, sin_t):
    n, wdt = lat.shape
    tm = ROW_TILE
    row = lambda w: pl.BlockSpec((tm, w), lambda i: (i, 0))
    one = lambda w: pl.BlockSpec((1, w), lambda i: (0, 0))
    return pl.pallas_call(
        _mla_prep1_kernel,
        grid=(n // tm,),
        in_specs=[row(wdt), one(MLA_Q_RANK), one(MLA_KV_RANK), row(128), row(128)],
        out_specs=[row(MLA_Q_RANK), row(MLA_KV_RANK), row(128)],
        out_shape=[jax.ShapeDtypeStruct((n, MLA_Q_RANK), BF16),
                   jax.ShapeDtypeStruct((n, MLA_KV_RANK), BF16),
                   jax.ShapeDtypeStruct((n, 128), BF16)],
        compiler_params=_cparams(("arbitrary",), 8 << 20),
        name="mla_prep1",
    )(lat, q_norm.reshape(1, -1), kv_norm.reshape(1, -1), cos_t, sin_t)


def _mla_prep2_kernel(q_ref, kn_ref, kr_ref, cos_ref, sin_ref, qo_ref, ko_ref, *, scale):
    cos_e = cos_ref[...]
    sin_s = sin_ref[...]
    kr = kr_ref[...]
    for h in range(MLA_HEADS):
        lo = h * MLA_QK_PAD
        qo_ref[:, lo:lo + MLA_NOPE] = (q_ref[:, lo:lo + MLA_NOPE].astype(F32) * scale).astype(BF16)
        qr = q_ref[:, lo + MLA_NOPE:lo + MLA_QK_PAD].astype(F32)
        qo_ref[:, lo + MLA_NOPE:lo + MLA_QK_PAD] = (_rope(qr, cos_e, sin_s) * scale).astype(BF16)
        ko_ref[:, lo:lo + MLA_NOPE] = kn_ref[:, h * MLA_NOPE:(h + 1) * MLA_NOPE]
        ko_ref[:, lo + MLA_NOPE:lo + MLA_QK_PAD] = kr


def _mla_prep2(q_raw, kv_all, kr, cos_t, sin_t):
    n = q_raw.shape[0]
    tm = ROW_TILE
    wq = MLA_HEADS * MLA_QK_PAD
    row = lambda w: pl.BlockSpec((tm, w), lambda i: (i, 0))
    return pl.pallas_call(
        functools.partial(_mla_prep2_kernel, scale=(MLA_NOPE + MLA_ROPE) ** -0.5 * LOG2E),
        grid=(n // tm,),
        in_specs=[row(wq), row(MLA_HEADS * MLA_NOPE), row(128), row(128), row(128)],
        out_specs=[row(wq), row(wq)],
        out_shape=[jax.ShapeDtypeStruct((n, wq), BF16), jax.ShapeDtypeStruct((n, wq), BF16)],
        compiler_params=_cparams(("arbitrary",), 16 << 20),
        name="mla_prep2",
    )(q_raw, kv_all, kr, cos_t, sin_t)


def _gqa_prep_kernel(qkv_ref, qg_ref, kg_ref, cos_ref, sin_ref, qo_ref, ko_ref, *, scale):
    cos_e = cos_ref[...]
    sin_s = sin_ref[...]
    for h in range(GQA_HEADS):
        xh = qkv_ref[:, h * HEAD_DIM:(h + 1) * HEAD_DIM].astype(F32)
        qo_ref[:, h * HEAD_DIM:(h + 1) * HEAD_DIM] = (
            _rope(_rms(xh, qg_ref[...]), cos_e, sin_s) * scale).astype(BF16)
    for h in range(GQA_KV):
        lo = (GQA_HEADS + h) * HEAD_DIM
        xh = qkv_ref[:, lo:lo + HEAD_DIM].astype(F32)
        ko_ref[:, h * HEAD_DIM:(h + 1) * HEAD_DIM] = _rope(_rms(xh, kg_ref[...]), cos_e, sin_s).astype(BF16)


def _gqa_prep(qkv, q_norm, k_norm, cos_t, sin_t):
    n = qkv.shape[0]
    tm = ROW_TILE
    wqk = (GQA_HEADS + GQA_KV) * HEAD_DIM
    row = lambda w: pl.BlockSpec((tm, w), lambda i: (i, 0))
    one = pl.BlockSpec((1, HEAD_DIM), lambda i: (0, 0))
    return pl.pallas_call(
        functools.partial(_gqa_prep_kernel, scale=HEAD_DIM ** -0.5 * LOG2E),
        grid=(n // tm,),
        in_specs=[row(wqk), one, one, row(HEAD_DIM), row(HEAD_DIM)],
        out_specs=[row(GQA_HEADS * HEAD_DIM), row(GQA_KV * HEAD_DIM)],
        out_shape=[jax.ShapeDtypeStruct((n, GQA_HEADS * HEAD_DIM), BF16),
                   jax.ShapeDtypeStruct((n, GQA_KV * HEAD_DIM), BF16)],
        compiler_params=_cparams(("arbitrary",), 16 << 20),
        name="gqa_prep",
    )(qkv, q_norm.reshape(1, -1), k_norm.reshape(1, -1), cos_t, sin_t)


def _flash_kernel(q_ref, k_ref, v_ref, o_ref, sa_ref, sb_ref, *, group, dk, dv, tk, n_lat, q_scale):
    tq = q_ref.shape[0]
    q = jnp.concatenate([q_ref[:, g * dk:(g + 1) * dk] for g in range(group)], axis=0) if group > 1 else q_ref[...]
    if q_scale is not None:
        q = (q.astype(F32) * q_scale).astype(BF16)

    s = _dot_nt(q, k_ref[SEQ:SEQ + CTX_LEN, :])
    m = jnp.max(s, axis=-1, keepdims=True)
    p = jnp.exp2(s - m)
    l = jnp.sum(p, axis=-1, keepdims=True)
    acc = _dot(p.astype(BF16), v_ref[SEQ:SEQ + CTX_LEN, :])

    if n_lat:
        def rows(ref, j):
            return ref[pl.ds(pl.multiple_of(j * tk, tk), tk), :]

        def update(s, vb, m, l, acc):
            m_new = jnp.maximum(m, jnp.max(s, axis=-1, keepdims=True))
            alpha = jnp.exp2(m - m_new)
            p = jnp.exp2(s - m_new)
            l = alpha * l + jnp.sum(p, axis=-1, keepdims=True)
            return m_new, l, alpha * acc + _dot(p.astype(BF16), vb)

        def pair(jj, carry, last):
            j0 = 2 * jj
            sb_ref[...] = _dot_nt(q, rows(k_ref, j0 + 1))
            carry = update(sa_ref[...], rows(v_ref, j0), *carry)
            if not last:
                sa_ref[...] = _dot_nt(q, rows(k_ref, j0 + 2))
            return update(sb_ref[...], rows(v_ref, j0 + 1), *carry)

        sa_ref[...] = _dot_nt(q, rows(k_ref, 0))
        carry = lax.fori_loop(0, n_lat // 2 - 1, lambda jj, c: pair(jj, c, False), (m, l, acc))
        m, l, acc = pair(n_lat // 2 - 1, carry, True)

    o = (acc / l).astype(o_ref.dtype)
    for g in range(group):
        o_ref[:, g * dv:(g + 1) * dv] = o[g * tq:(g + 1) * tq]


def _flash(q_arr, k_arr, v_arr, *, n_kv_heads, group, dk, dv, q_col0, k_col0, v_col0, ctx_queries, tq, tk,
           q_scale=None):
    n_tok = k_arr.shape[0]
    if ctx_queries:
        n_q, q_blk0, n_lat = CTX_LEN, SEQ // tq, 0
    else:
        n_q, q_blk0, n_lat = SEQ, 0, SEQ // tk
        assert n_lat % 2 == 0
    m = group * tq
    s_shape = (m, tk) if n_lat else (8, 128)
    vmem = (2 * n_tok * (dk + dv) * 2 + 4 * tq * group * (dk + dv) * 2
            + 5 * m * max(tk, CTX_LEN) * 4 + 4 * m * dv * 4)
    return pl.pallas_call(
        functools.partial(_flash_kernel, group=group, dk=dk, dv=dv, tk=tk, n_lat=n_lat, q_scale=q_scale),
        grid=(n_kv_heads, n_q // tq),
        in_specs=[pl.BlockSpec((tq, group * dk), lambda h, i: (q_blk0 + i, q_col0 + h)),
                  pl.BlockSpec((n_tok, dk), lambda h, i: (0, k_col0 + h)),
                  pl.BlockSpec((n_tok, dv), lambda h, i: (0, v_col0 + h))],
        out_specs=pl.BlockSpec((tq, group * dv), lambda h, i: (i, h)),
        out_shape=jax.ShapeDtypeStruct((n_q, n_kv_heads * group * dv), BF16),
        scratch_shapes=[pltpu.VMEM(s_shape, F32), pltpu.VMEM(s_shape, F32)],
        compiler_params=_cparams(("arbitrary", "arbitrary"), vmem + (8 << 20)),
        name="flash_ctx" if ctx_queries else "flash",
    )(q_arr, k_arr, v_arr)


def _na_start_row(b):
    return jnp.clip(NA_RB * b - NA_KH // 2, 0, GRID_H - NA_U)


def _na_kernel(q_ref, k_ref, v_ref, bias_ref, o_ref, *, scale):
    b = pl.program_id(1)
    start = pl.multiple_of(_na_start_row(b) * GRID_W, GRID_W)
    nwin = NA_U * GRID_W
    q = (q_ref[...].astype(F32) * scale).astype(BF16)
    s_loc = _dot_nt(q, k_ref[pl.ds(start, nwin), :]) + bias_ref[...]
    s_ctx = _dot_nt(q, k_ref[SEQ:SEQ + CTX_LEN, :])
    m = jnp.maximum(jnp.max(s_loc, axis=-1, keepdims=True), jnp.max(s_ctx, axis=-1, keepdims=True))
    p_loc = jnp.exp(s_loc - m)
    p_ctx = jnp.exp(s_ctx - m)
    l = jnp.sum(p_loc, axis=-1, keepdims=True) + jnp.sum(p_ctx, axis=-1, keepdims=True)
    o = _dot(p_loc.astype(BF16), v_ref[pl.ds(start, nwin), :]) + _dot(p_ctx.astype(BF16), v_ref[SEQ:SEQ + CTX_LEN, :])
    o_ref[...] = (o / l).astype(o_ref.dtype)


def _na_bias_tables(rpb):
    h = rpb.shape[0]
    w = np.arange(GRID_W)
    c0 = np.clip(w - NA_KW // 2, 0, GRID_W - NA_KW)
    colmask = (w[None, :] >= c0[:, None]) & (w[None, :] < c0[:, None] + NA_KW)
    colidx = np.clip(w[None, :] - w[:, None] + NA_KW - 1, 0, 2 * NA_KW - 2)
    slabs = jnp.where(colmask[None, None], rpb[:, :, colidx], NEG)
    slabs = jnp.concatenate([slabs, jnp.full((h, 1, GRID_W, GRID_W), NEG, rpb.dtype)], axis=1)
    masked = 2 * NA_KH - 1
    nb = GRID_H // NA_RB
    tabs = []
    for b in (0, 1, nb - 1):
        start = int(np.clip(NA_RB * b - NA_KH // 2, 0, GRID_H - NA_U))
        a = np.full((NA_RB, NA_U), masked, np.int32)
        for j in range(NA_RB):
            r = NA_RB * b + j
            r0 = int(np.clip(r - NA_KH // 2, 0, GRID_H - NA_KH))
            for u in range(NA_U):
                kr = start + u
                if r0 <= kr < r0 + NA_KH:
                    a[j, u] = kr - r + NA_KH - 1
        t = slabs[:, a]
        tabs.append(t.transpose(0, 1, 3, 2, 4).reshape(h, NA_RB * GRID_W, NA_U * GRID_W))
    return jnp.stack(tabs, axis=1)


def _na_attention(qkv, bias_tabs):
    n_tok = qkv.shape[0]
    nb = GRID_H // NA_RB
    tq = NA_RB * GRID_W
    nwin = NA_U * GRID_W

    def cls(b):
        return jnp.where(b == 0, 0, jnp.where(b == nb - 1, 2, 1))

    vmem = 4 * n_tok * HEAD_DIM * 2 + 2 * tq * nwin * 4 + 4 * tq * (nwin + CTX_LEN) * 4
    return pl.pallas_call(
        functools.partial(_na_kernel, scale=HEAD_DIM ** -0.5),
        grid=(NA_HEADS, nb),
        in_specs=[pl.BlockSpec((tq, HEAD_DIM), lambda h, b: (b, h)),
                  pl.BlockSpec((n_tok, HEAD_DIM), lambda h, b: (0, NA_HEADS + h)),
                  pl.BlockSpec((n_tok, HEAD_DIM), lambda h, b: (0, 2 * NA_HEADS + h)),
                  pl.BlockSpec((None, None, tq, nwin), lambda h, b: (h, cls(b), 0, 0))],
        out_specs=pl.BlockSpec((tq, HEAD_DIM), lambda h, b: (b, h)),
        out_shape=jax.ShapeDtypeStruct((SEQ, NA_HEADS * HEAD_DIM), BF16),
        compiler_params=_cparams(("arbitrary", "arbitrary"), vmem + (8 << 20)),
        name="na_attention",
    )(qkv, qkv, qkv, bias_tabs)


def _dispatch_kernel(tail_ref, nu_ref, slot_ref, h_ref, xg_hbm, zbuf, sem, zsem, *, tb, n_blocks):
    te = EXPERT_TILE

    @pl.when(pl.program_id(0) == 0)
    def _():
        zbuf[...] = jnp.zeros_like(zbuf)

        def zero_tile(row0, wait):
            cp = pltpu.make_async_copy(zbuf, xg_hbm.at[pl.ds(pl.multiple_of(row0, te), te)], zsem)
            if wait:
                cp.wait()
            else:
                cp.start()

        def zero_tail(e, wait):
            @pl.when(tail_ref[e] >= 0)
            def _():
                zero_tile(tail_ref[e], wait)

        for wait in (False, True):
            lax.fori_loop(0, N_EXPERTS, lambda e, c: (zero_tail(e, wait), c)[1], 0)
            lax.fori_loop(nu_ref[0], n_blocks, lambda b, c: (zero_tile(b * te, wait), c)[1], 0)

    def body(t, carry):
        for k in range(MOE_TOP_K):
            pltpu.make_async_copy(h_ref.at[pl.ds(t, 1)], xg_hbm.at[pl.ds(slot_ref[k, t], 1)], sem).start(
                priority=k % 2)
        return carry

    lax.fori_loop(0, tb, body, 0, unroll=DMA_LOOP_UNROLL)
    for k in range(MOE_TOP_K):
        pltpu.make_async_copy(h_ref, xg_hbm.at[pl.ds(0, tb)], sem).wait()


def _dispatch(h32, slot_t, tail_start, n_used, n_rows, n_slots):
    d = h32.shape[1]
    tb = ROW_TILE
    grid_spec = pltpu.PrefetchScalarGridSpec(
        num_scalar_prefetch=2,
        grid=(n_rows // tb,),
        in_specs=[pl.BlockSpec((MOE_TOP_K, tb), lambda i, tl, nu: (0, i), memory_space=pltpu.SMEM),
                  pl.BlockSpec((tb, d), lambda i, tl, nu: (i, 0))],
        out_specs=pl.BlockSpec(memory_space=pl.ANY),
        scratch_shapes=[pltpu.VMEM((EXPERT_TILE, d), h32.dtype), pltpu.SemaphoreType.DMA(()),
                        pltpu.SemaphoreType.DMA(())],
    )
    return pl.pallas_call(
        functools.partial(_dispatch_kernel, tb=tb, n_blocks=n_slots // EXPERT_TILE),
        grid_spec=grid_spec,
        out_shape=jax.ShapeDtypeStruct((n_slots, d), h32.dtype),
        compiler_params=_cparams(("arbitrary",), 16 << 20),
        name="moe_dispatch",
    )(tail_start, n_used, slot_t, h32)


FFN_LEAD = 2


def _ffn_kernel(be_ref, ord_ref, nu_ref, x_ref, wg_ref, wu_ref, wd_ref, o_ref, wgb, wub, wdb, *, packed, nb):
    b = pl.program_id(0)
    c = b - FFN_LEAD

    def at(ref, i):
        return ref[jnp.clip(i, 0, nb - 1)]

    def starts_expert(i):
        return jnp.logical_or(b == 0, at(be_ref, i) != at(be_ref, i - 1))

    @pl.when(starts_expert(c))
    def _():
        wdb[at(ord_ref, c) % 2] = wd_ref[...].astype(BF16)

    @pl.when(jnp.logical_and(c >= 0, c < nu_ref[0]))
    def _():
        slot = at(ord_ref, c) % 2
        x = _unpack_bf16_pairs(x_ref[...]).astype(BF16) if packed else x_ref[...]
        g = _dot(x, wgb[slot])
        u = _dot(x, wub[slot])
        a = (g * jax.nn.sigmoid(g) * u).astype(BF16)
        y = _dot(a, wdb[slot])
        o_ref[...] = _pack_bf16_pairs(y) if packed else y.astype(o_ref.dtype)

    @pl.when(c >= nu_ref[0])
    def _():
        o_ref[...] = jnp.zeros_like(o_ref)

    @pl.when(starts_expert(b))
    def _():
        wgb[at(ord_ref, b) % 2] = wg_ref[...].astype(BF16)

    @pl.when(starts_expert(b - 1))
    def _():
        wub[at(ord_ref, b - 1) % 2] = wu_ref[...].astype(BF16)


def _expert_ffn(x_rows, block_expert, n_used, wg, wu, wd, layer, tb, packed):
    p, dx = x_rows.shape
    d, de = wg.shape[-2:]
    nb = p // tb
    out_dtype = x_rows.dtype
    vmem = 2 * 3 * d * de * 4 + 2 * 3 * d * de * 2 + 4 * tb * d * 2 + 4 * tb * de * 4 + 3 * tb * d * 4
    idx = jnp.arange(nb, dtype=jnp.int32)
    be = jnp.where(idx < n_used[0], block_expert, block_expert[jnp.maximum(n_used[0] - 1, 0)])
    ordinal = jnp.cumsum(jnp.concatenate([jnp.zeros((1,), jnp.int32),
                                          (be[1:] != be[:-1]).astype(jnp.int32)])).astype(jnp.int32)

    def blk(i):
        return jnp.clip(i, 0, nb - 1)

    def wspec(shape, lag):
        return pl.BlockSpec((None, None) + shape, lambda b, be, od, nu: (layer, be[blk(b - lag)], 0, 0))

    grid_spec = pltpu.PrefetchScalarGridSpec(
        num_scalar_prefetch=3,
        grid=(nb + FFN_LEAD,),
        in_specs=[pl.BlockSpec((tb, dx), lambda b, be, od, nu: (jnp.minimum(blk(b - FFN_LEAD), nu[0] - 1), 0)),
                  wspec((d, de), 0), wspec((d, de), 1), wspec((de, d), 2)],
        out_specs=pl.BlockSpec((tb, dx), lambda b, be, od, nu: (blk(b - FFN_LEAD), 0)),
        scratch_shapes=[pltpu.VMEM((2, d, de), BF16), pltpu.VMEM((2, d, de), BF16), pltpu.VMEM((2, de, d), BF16)],
    )
    return pl.pallas_call(
        functools.partial(_ffn_kernel, packed=packed, nb=nb),
        grid_spec=grid_spec,
        out_shape=jax.ShapeDtypeStruct((p, dx), out_dtype),
        compiler_params=_cparams(("arbitrary",), vmem + (4 << 20)),
        name="expert_ffn",
    )(be, ordinal, n_used, x_rows, wg, wu, wd)


def _combine_kernel(*refs, tc, nblk, modulated, emit_x):
    it = iter(refs)
    s_cur, s_nxt, y_hbm, ew_ref, ysh_ref, x_ref, gate_ref, g_ref = [next(it) for _ in range(8)]
    if modulated:
        shift_ref, scale_ref = next(it), next(it)
    if emit_x:
        o_ref = next(it)
    h_ref, buf, sem = next(it), next(it), next(it)
    i = pl.program_id(0)

    def gather(slot_ref, bslot):
        def body(t, carry):
            for k in range(MOE_TOP_K):
                pltpu.make_async_copy(y_hbm.at[pl.ds(slot_ref[k, t], 1)],
                                      buf.at[bslot, k, pl.ds(t, 1)], sem.at[bslot]).start()
            return carry

        lax.fori_loop(0, tc, body, 0, unroll=DMA_LOOP_UNROLL)

    @pl.when(i == 0)
    def _():
        gather(s_cur, 0)

    cur = i % 2

    @pl.when(i + 1 < nblk)
    def _():
        @pl.when(cur == 0)
        def _():
            gather(s_nxt, 1)

        @pl.when(cur == 1)
        def _():
            gather(s_nxt, 0)

    def finish(bslot):
        for k in range(MOE_TOP_K):
            pltpu.make_async_copy(y_hbm.at[pl.ds(0, tc)], buf.at[bslot, k], sem.at[bslot]).wait()
        ew = ew_ref[...]
        acc = ysh_ref[...].astype(F32)
        for k in range(MOE_TOP_K):
            acc = acc + ew[:, k:k + 1] * _unpack_bf16_pairs(buf[bslot, k])
        xn = x_ref[...] + gate_ref[...] * acc
        if emit_x:
            o_ref[...] = xn
        hn = _rms(xn, g_ref[...])
        if modulated:
            hn = hn * (1.0 + scale_ref[...]) + shift_ref[...]
        h_ref[...] = hn.astype(h_ref.dtype)

    @pl.when(cur == 0)
    def _():
        finish(0)

    @pl.when(cur == 1)
    def _():
        finish(1)


def _combine(y, slot_t, ew, ysh, x, gate, n_rows, gain, mod):
    d = x.shape[1]
    tc = COMBINE_TILE
    nblk = n_rows // tc
    lat_blocks = SEQ // tc
    row = pl.BlockSpec((tc, d), lambda i: (i, 0))
    vec = pl.BlockSpec((None, 1, d), lambda i: (jnp.where(i >= lat_blocks, 1, 0), 0, 0))
    vmem = 2 * MOE_TOP_K * tc * d * 2 + 12 * tc * d * 4
    args = [slot_t, slot_t, y, ew, ysh, x, gate.reshape(2, 1, d), gain.reshape(1, d)]
    in_specs = [pl.BlockSpec((MOE_TOP_K, tc), lambda i: (0, i), memory_space=pltpu.SMEM),
                pl.BlockSpec((MOE_TOP_K, tc), lambda i: (0, jnp.minimum(i + 1, nblk - 1)),
                             memory_space=pltpu.SMEM),
                pl.BlockSpec(memory_space=pl.ANY),
                pl.BlockSpec((tc, MOE_TOP_K), lambda i: (i, 0)),
                row, row, vec, pl.BlockSpec((1, d), lambda i: (0, 0))]
    if mod is not None:
        args += [mod[0].reshape(2, 1, d), mod[1].reshape(2, 1, d)]
        in_specs += [vec, vec]
        out_shape = [jax.ShapeDtypeStruct((n_rows, d), F32), jax.ShapeDtypeStruct((n_rows, d), BF16)]
    else:
        out_shape = [jax.ShapeDtypeStruct((n_rows, d), F32)]
    return pl.pallas_call(
        functools.partial(_combine_kernel, tc=tc, nblk=nblk, modulated=mod is not None, emit_x=mod is not None),
        grid=(nblk,),
        in_specs=in_specs,
        out_specs=[row] * len(out_shape),
        out_shape=out_shape,
        scratch_shapes=[pltpu.VMEM((2, MOE_TOP_K, tc, d // 2), jnp.uint32), pltpu.SemaphoreType.DMA((2,))],
        compiler_params=_cparams(("arbitrary",), vmem + (4 << 20)),
        name="moe_combine",
    )(*args)


def _routing_tables(counts, n_rows):
    tb = EXPERT_TILE
    n_blocks = -(-(n_rows * MOE_TOP_K) // tb) + N_EXPERTS
    padded = (counts + tb - 1) // tb * tb
    pad_end = jnp.cumsum(padded)
    pad_start = (pad_end - padded).astype(jnp.int32)
    tail_start = jnp.where(padded > 0, pad_end - tb, -1).astype(jnp.int32)
    blk_start = jnp.arange(n_blocks, dtype=jnp.int32) * tb
    block_expert = jnp.sum(blk_start[:, None] >= pad_end[None, :], axis=1)
    block_expert = jnp.minimum(block_expert, N_EXPERTS - 1).astype(jnp.int32)
    n_used = (pad_end[-1] // tb).astype(jnp.int32).reshape(1)
    return pad_start, tail_start, block_expert, n_used, n_blocks * tb


def _moe(layer, x_new, h, h32, eidx_t, ew_t, rank_t, counts, gate, n_rows, wg, wu, wd, sg, su, sd, next_gain, next_mod):
    pad_start, tail_start, block_expert, n_used, n_slots = _routing_tables(counts[:, 0], n_rows)
    e_iota = jnp.arange(N_EXPERTS, dtype=jnp.int32)
    slot_t = jnp.sum(jnp.where(eidx_t[:, :, None] == e_iota, pad_start, 0), axis=-1) + rank_t
    xg = _dispatch(h32, slot_t, tail_start, n_used, n_rows, n_slots)
    y = _expert_ffn(xg, block_expert, n_used, wg, wu, wd, layer, EXPERT_TILE, True)
    tm = MM_TM if n_rows % MM_TM == 0 else 512
    ysh = _expert_ffn(h, jnp.zeros((n_rows // tm,), jnp.int32), jnp.full((1,), n_rows // tm, jnp.int32),
                      sg[:, None], su[:, None], sd[:, None], layer, tm, False)
    return _combine(y, slot_t, ew_t.T, ysh, x_new, gate, n_rows, next_gain, next_mod)


def _rope_tables(d, width):
    t = jnp.arange(SEQ)
    row = (t // GRID_W).astype(F32)
    col = (t % GRID_W).astype(F32)
    n = d // 4
    inv = ROPE_THETA ** (-jnp.arange(n, dtype=F32) / n)
    ang = jnp.concatenate([row[:, None] * inv, col[:, None] * inv], axis=-1)
    ang = jnp.concatenate([ang, jnp.zeros((CTX_LEN, d // 2), F32)], axis=0)
    cos = jnp.repeat(jnp.cos(ang), 2, axis=1)
    sin = jnp.repeat(jnp.sin(ang), 2, axis=1) * jnp.tile(jnp.array([-1.0, 1.0], F32), d // 2)
    if width > d:
        cos = jnp.concatenate([cos, jnp.ones((N_TOK, width - d), F32)], axis=1)
        sin = jnp.concatenate([sin, jnp.zeros((N_TOK, width - d), F32)], axis=1)
    return cos, sin


def _mixer_even(h, w_in, rpb, q_norm, w_uq, kv_norm, w_ukv, w_out):
    n_na = 3 * NA_HEADS * HEAD_DIM
    qkv = _matmul(h, w_in[:, :n_na], N_TOK, BF16, MM_TM, 1024)
    w_tail = jnp.pad(w_in[:, n_na:], ((0, 0), (0, 64)))
    lat = _matmul(h, w_tail, N_TOK, F32, MM_TM, 384)
    cos_t, sin_t = _rope_tables(MLA_ROPE, 128)
    nq, nkv, kr = _mla_prep1(lat, q_norm, kv_norm, cos_t, sin_t)
    wq = w_uq.reshape(MLA_Q_RANK, MLA_HEADS, MLA_NOPE + MLA_ROPE)
    wq = jnp.pad(wq, ((0, 0), (0, 0), (0, MLA_QK_PAD - MLA_NOPE - MLA_ROPE))).reshape(MLA_Q_RANK, -1)
    wkv = w_ukv.reshape(MLA_KV_RANK, MLA_HEADS, MLA_NOPE + MLA_V)
    wkv = jnp.concatenate([wkv[:, :, :MLA_NOPE].reshape(MLA_KV_RANK, -1),
                           wkv[:, :, MLA_NOPE:].reshape(MLA_KV_RANK, -1)], axis=1)
    q_raw = _matmul(nq, wq, N_TOK, F32, MM_TM, 1024)
    kv_all = _matmul(nkv, wkv, N_TOK, BF16, MM_TM, 1024)
    q_mla, k_cat = _mla_prep2(q_raw, kv_all, kr, cos_t, sin_t)

    o_na = _na_attention(qkv, _na_bias_tables(rpb))
    mla = dict(n_kv_heads=MLA_HEADS, group=1, dk=MLA_QK_PAD, dv=MLA_V, q_col0=0, k_col0=0, v_col0=MLA_HEADS)
    o_mla = _flash(q_mla, k_cat, kv_all, ctx_queries=False, tq=1024, tk=1024, **mla)
    o_mla_c = _flash(q_mla, k_cat, kv_all, ctx_queries=True, tq=256, tk=512, **mla)
    na = dict(n_kv_heads=NA_HEADS, group=1, dk=HEAD_DIM, dv=HEAD_DIM, q_col0=0, k_col0=NA_HEADS, v_col0=2 * NA_HEADS)
    o_na_c = _flash(qkv, qkv, qkv, ctx_queries=True, tq=256, tk=512, q_scale=HEAD_DIM ** -0.5 * LOG2E, **na)
    o = jnp.concatenate([jnp.concatenate([o_na, o_mla], axis=1),
                         jnp.concatenate([o_na_c, o_mla_c], axis=1)], axis=0)
    return _matmul(o, w_out, N_TOK, F32, MM_TM, 1024)


def _mixer_odd_last(h, w_qkv, q_norm, k_norm, w_o):
    qkv = _matmul(h, w_qkv, N_TOK, BF16, MM_TM, 1024)
    cos_t, sin_t = _rope_tables(HEAD_DIM, HEAD_DIM)
    q_g, k_g = _gqa_prep(qkv, q_norm, k_norm, cos_t, sin_t)
    o = _flash(q_g, k_g, qkv, n_kv_heads=GQA_KV, group=GQA_GROUP, dk=HEAD_DIM, dv=HEAD_DIM,
               q_col0=0, k_col0=0, v_col0=GQA_HEADS + GQA_KV, ctx_queries=False, tq=256, tk=1024)
    return _matmul(o, w_o, SEQ, F32, 512, 1024)


def kernel(x, c, ctx, c_ctx, ada_w, ada_b, mix_norm, ffn_norm, ev_w_in, ev_rpb, ev_q_norm, ev_w_uq, ev_kv_norm, ev_w_ukv, ev_w_out, od_w_qkv, od_q_norm, od_k_norm, od_w_o, router_w, router_b, exp_w_gate, exp_w_up, exp_w_down, sh_w_gate, sh_w_up, sh_w_down, final_norm):
    assert DEPTH == 2 and x.shape == (1, SEQ, D_MODEL)
    mods = _adaln_mods(c, c_ctx, ada_w, ada_b)
    xt = jnp.concatenate([x[0], ctx[0]], axis=0)

    def parts(i):
        return [mods[i, :, k * D_MODEL:(k + 1) * D_MODEL] for k in range(6)]

    sh1, sc1, g1, sh2, sc2, g2 = parts(0)
    (h,) = _norm_call(xt, N_TOK, mix_norm[0], mod=(sh1, sc1))
    o = _mixer_even(h, ev_w_in[0], ev_rpb[0], ev_q_norm[0], ev_w_uq[0], ev_kv_norm[0], ev_w_ukv[0], ev_w_out[0])
    xt, h, h32, eidx_t, ew_t, rank_t, counts = _norm_call(xt, N_TOK, ffn_norm[0], resid=(o, g1), mod=(sh2, sc2),
                                                          router=(router_w[0], router_b[0]))
    n1_sh1, n1_sc1, n1_g1, n1_sh2, n1_sc2, n1_g2 = parts(1)
    xt, h = _moe(0, xt, h, h32, eidx_t, ew_t, rank_t, counts, g2, N_TOK, exp_w_gate, exp_w_up, exp_w_down,
                 sh_w_gate, sh_w_up, sh_w_down, mix_norm[1], (n1_sh1, n1_sc1))

    o = _mixer_odd_last(h, od_w_qkv[0], od_q_norm[0], od_k_norm[0], od_w_o[0])
    xl, h, h32, eidx_t, ew_t, rank_t, counts = _norm_call(xt, SEQ, ffn_norm[1], resid=(o, n1_g1),
                                                          mod=(n1_sh2, n1_sc2),
                                                          router=(router_w[1], router_b[1]))
    (out,) = _moe(1, xl, h, h32, eidx_t, ew_t, rank_t, counts, n1_g2, SEQ, exp_w_gate, exp_w_up, exp_w_down,
                  sh_w_gate, sh_w_up, sh_w_down, final_norm, None)
    return out[None]
```

```python
import functools

import numpy as np
import jax
import jax.numpy as jnp
from jax import lax
from jax.experimental import pallas as pl
from jax.experimental.pallas import tpu as pltpu

F32 = jnp.float32
BF16 = jnp.bfloat16

D_MODEL = 2048
SEQ = 8192
DEPTH = 2
GRID_W = 64
GRID_H = SEQ // GRID_W
CTX_LEN = 256
N_TOK = SEQ + CTX_LEN
EPS = 1e-6
ROPE_THETA = 10000.0
HEAD_DIM = 128
NA_HEADS = 8
NA_KH = 8
NA_KW = 16
MLA_HEADS = 8
MLA_Q_RANK = 512
MLA_KV_RANK = 512
MLA_NOPE = 128
MLA_ROPE = 64
MLA_V = 128
MLA_QK_PAD = 256
GQA_HEADS = 16
GQA_KV = 4
GQA_GROUP = GQA_HEADS // GQA_KV
N_EXPERTS = 64
N_GROUPS = 8
GROUP_SIZE = N_EXPERTS // N_GROUPS
TOPK_GROUPS = 4
MOE_TOP_K = 8
D_EXPERT = 512
ROUTED_SCALE = 2.5

NEG = -1e30
LOG2E = 1.4426950408889634

VMEM_LIMIT_CAP = 60 * 1024 * 1024

ROW_TILE = 256
MM_TM = 768
NA_RB = 4
NA_U = NA_RB + NA_KH - 1
EXPERT_TILE = 256
COMBINE_TILE = 128
DMA_LOOP_UNROLL = 128


def _cparams(sem, vmem_bytes):
    return pltpu.CompilerParams(
        dimension_semantics=sem,
        vmem_limit_bytes=int(min(max(vmem_bytes, 16 * 1024 * 1024), VMEM_LIMIT_CAP)),
    )


def _pack_bf16_pairs(x):
    w = x.shape[1] // 2

    def rounded(v):
        b = lax.bitcast_convert_type(v, jnp.uint32)
        return b + (((b >> 16) & jnp.uint32(1)) + jnp.uint32(0x7FFF))

    return (rounded(x[:, :w]) >> 16) | (rounded(x[:, w:]) & jnp.uint32(0xFFFF0000))


def _unpack_bf16_pairs(p):
    lo = lax.bitcast_convert_type(p << 16, F32)
    hi = lax.bitcast_convert_type(p & jnp.uint32(0xFFFF0000), F32)
    return jnp.concatenate([lo, hi], axis=1)


def _dot(a, b):
    return jnp.dot(a, b, preferred_element_type=F32)


def _dot_nt(a, b):
    return lax.dot_general(a, b, (((1,), (1,)), ((), ())), preferred_element_type=F32)


def _mods_kernel(cc_ref, w_ref, b_ref, o_ref, acc_ref, *, nk):
    k = pl.program_id(2)

    @pl.when(k == 0)
    def _():
        acc_ref[...] = jnp.zeros_like(acc_ref)

    a = cc_ref[...]
    a = a * jax.nn.sigmoid(a)
    w = w_ref[...]
    tk, tn = w.shape
    w3 = w.reshape(tk // 8, 8, tn)
    for v in range(2):
        col = a[:, v:v + 1].reshape(tk // 8, 8, 1)
        acc_ref[v] += jnp.sum(w3 * col, axis=0)

    @pl.when(k == nk - 1)
    def _():
        o_ref[...] = jnp.sum(acc_ref[...], axis=1) + b_ref[...]


def _adaln_mods(c, c_ctx, ada_w, ada_b):
    depth, d, n = ada_w.shape
    tk, tn = 1024, 1536
    nk = d // tk
    cc = jnp.stack([c[0], c_ctx], axis=1)
    return pl.pallas_call(
        functools.partial(_mods_kernel, nk=nk),
        grid=(depth, n // tn, nk),
        in_specs=[
            pl.BlockSpec((tk, 2), lambda i, j, k: (k, 0)),
            pl.BlockSpec((None, tk, tn), lambda i, j, k: (i, k, j)),
            pl.BlockSpec((None, 1, tn), lambda i, j, k: (i, 0, j)),
        ],
        out_specs=pl.BlockSpec((None, 2, tn), lambda i, j, k: (i, 0, j)),
        out_shape=jax.ShapeDtypeStruct((depth, 2, n), F32),
        scratch_shapes=[pltpu.VMEM((2, 8, tn), F32)],
        compiler_params=_cparams(("arbitrary", "arbitrary", "arbitrary"), 4 * tk * tn * 4),
        name="adaln_mods",
    )(cc, ada_w, ada_b.reshape(depth, 1, n))


def _router_select(hf, rwt_ref, rb_ref, base_ref):
    logits = lax.dot_general(rwt_ref[...], hf, (((1,), (1,)), ((), ())),
                             precision=lax.Precision.HIGHEST, preferred_element_type=F32)
    scores = jax.nn.sigmoid(logits)
    sel = scores + rb_ref[...]
    tm = sel.shape[1]
    ninf = jnp.float32(-jnp.inf)
    sub = lax.broadcasted_iota(jnp.int32, (GROUP_SIZE, tm), 0)
    gi = lax.broadcasted_iota(jnp.int32, (N_GROUPS, tm), 0)
    gs = jnp.zeros((N_GROUPS, tm), F32)
    for g in range(N_GROUPS):
        blk = sel[g * GROUP_SIZE:(g + 1) * GROUP_SIZE, :]
        m1 = jnp.max(blk, axis=0, keepdims=True)
        first = jnp.min(jnp.where(blk == m1, sub, GROUP_SIZE), axis=0, keepdims=True)
        m2 = jnp.max(jnp.where(sub == first, ninf, blk), axis=0, keepdims=True)
        gs = jnp.where(gi == g, m1 + m2, gs)
    gsel = jnp.zeros((N_GROUPS, tm), F32)
    cur = gs
    for _ in range(TOPK_GROUPS):
        mx = jnp.max(cur, axis=0, keepdims=True)
        idx = jnp.min(jnp.where(cur == mx, gi, N_GROUPS), axis=0, keepdims=True)
        pick = gi == idx
        gsel = jnp.where(pick, 1.0, gsel)
        cur = jnp.where(pick, ninf, cur)
    ei = lax.broadcasted_iota(jnp.int32, (N_EXPERTS, tm), 0)
    eg = ei // GROUP_SIZE
    emask = jnp.zeros((N_EXPERTS, tm), F32)
    for g in range(N_GROUPS):
        emask = jnp.where(eg == g, gsel[g:g + 1, :], emask)
    cur = jnp.where(emask > 0.0, sel, ninf)
    idxs, ws, picks = [], [], []
    for _ in range(MOE_TOP_K):
        mx = jnp.max(cur, axis=0, keepdims=True)
        idx = jnp.min(jnp.where(cur == mx, ei, N_EXPERTS), axis=0, keepdims=True)
        pick = ei == idx
        ws.append(jnp.sum(jnp.where(pick, scores, 0.0), axis=0, keepdims=True))
        idxs.append(idx)
        picks.append(pick)
        cur = jnp.where(pick, ninf, cur)
    total = ws[0]
    for w in ws[1:]:
        total = total + w
    ws = [w / total * ROUTED_SCALE for w in ws]
    onehot = jnp.zeros((N_EXPERTS, tm), F32)
    for pick in picks:
        onehot = jnp.where(pick, 1.0, onehot)
    tri = (lax.broadcasted_iota(jnp.int32, (tm, tm), 0) < lax.broadcasted_iota(jnp.int32, (tm, tm), 1))
    before = _dot(onehot.astype(BF16), jnp.where(tri, 1.0, 0.0).astype(BF16)) + base_ref[...]
    ranks = [jnp.sum(jnp.where(pick, before, 0.0), axis=0, keepdims=True).astype(jnp.int32) for pick in picks]
    base_ref[...] += jnp.sum(onehot, axis=1, keepdims=True)
    return idxs, ws, ranks


def _norm_kernel(*refs, has_resid, modulated, with_router, out_dtype):
    it = iter(refs)
    x_ref = next(it)
    if has_resid:
        o_ref = next(it)
        gate_ref = next(it)
    g_ref = next(it)
    if modulated:
        shift_ref = next(it)
        scale_ref = next(it)
    if with_router:
        rwt_ref = next(it)
        rb_ref = next(it)
    if has_resid:
        xn_ref = next(it)
    h_ref = next(it)
    if with_router:
        hp_ref = next(it)
        eidx_ref = next(it)
        ew_ref = next(it)
        rank_ref = next(it)
        cnt_ref = next(it)
        base_ref = next(it)

        @pl.when(pl.program_id(0) == 0)
        def _():
            base_ref[...] = jnp.zeros_like(base_ref)

    x = x_ref[...]
    if has_resid:
        x = x + gate_ref[...] * o_ref[...].astype(F32)
        xn_ref[...] = x
    y = x * lax.rsqrt(jnp.mean(x * x, axis=-1, keepdims=True) + EPS)
    y = y * g_ref[...]
    if modulated:
        y = y * (1.0 + scale_ref[...]) + shift_ref[...]
    h_ref[...] = y.astype(out_dtype)
    if with_router:
        hp_ref[...] = _pack_bf16_pairs(y)
        idxs, ws, ranks = _router_select(y, rwt_ref, rb_ref, base_ref)
        for k in range(MOE_TOP_K):
            eidx_ref[k:k + 1, :] = idxs[k]
            ew_ref[k:k + 1, :] = ws[k]
            rank_ref[k:k + 1, :] = ranks[k]
        cnt_ref[...] = base_ref[...].astype(jnp.int32)


def _norm_call(x, n_rows, gain, *, resid=None, mod=None, router=None, out_dtype=BF16):
    d = x.shape[1]
    tm = ROW_TILE
    nblk = n_rows // tm
    lat_blocks = SEQ // tm

    def cls_map(i):
        return (jnp.where(i >= lat_blocks, 1, 0), 0, 0)

    row = pl.BlockSpec((tm, d), lambda i: (i, 0))
    vec = pl.BlockSpec((None, 1, d), cls_map)
    args, in_specs = [x], [row]
    if resid is not None:
        o, gate = resid
        args += [o, gate.reshape(2, 1, d)]
        in_specs += [row, vec]
    args.append(gain.reshape(1, d))
    in_specs.append(pl.BlockSpec((1, d), lambda i: (0, 0)))
    if mod is not None:
        shift, scale = mod
        args += [shift.reshape(2, 1, d), scale.reshape(2, 1, d)]
        in_specs += [vec, vec]
    if router is not None:
        rw, rb = router
        args += [rw.T, rb.reshape(N_EXPERTS, 1)]
        in_specs += [pl.BlockSpec((N_EXPERTS, d), lambda i: (0, 0)),
                     pl.BlockSpec((N_EXPERTS, 1), lambda i: (0, 0))]
    out_shape, out_specs = [], []
    if resid is not None:
        out_shape.append(jax.ShapeDtypeStruct((n_rows, d), F32))
        out_specs.append(row)
    out_shape.append(jax.ShapeDtypeStruct((n_rows, d), out_dtype))
    out_specs.append(row)
    if router is not None:
        out_shape += [jax.ShapeDtypeStruct((n_rows, d // 2), jnp.uint32),
                      jax.ShapeDtypeStruct((MOE_TOP_K, n_rows), jnp.int32),
                      jax.ShapeDtypeStruct((MOE_TOP_K, n_rows), F32),
                      jax.ShapeDtypeStruct((MOE_TOP_K, n_rows), jnp.int32),
                      jax.ShapeDtypeStruct((N_EXPERTS, 1), jnp.int32)]
        topk = pl.BlockSpec((MOE_TOP_K, tm), lambda i: (0, i))
        out_specs += [pl.BlockSpec((tm, d // 2), lambda i: (i, 0)), topk, topk, topk,
                      pl.BlockSpec((N_EXPERTS, 1), lambda i: (0, 0))]
    return pl.pallas_call(
        functools.partial(_norm_kernel, has_resid=resid is not None, modulated=mod is not None,
                          with_router=router is not None, out_dtype=out_dtype),
        grid=(nblk,),
        in_specs=in_specs,
        out_specs=out_specs,
        out_shape=out_shape,
        scratch_shapes=[pltpu.VMEM((N_EXPERTS, 1), F32)] if router is not None else [],
        compiler_params=_cparams(("arbitrary",), 16 * tm * d * 4),
        name="norm_router" if router is not None else "norm",
    )(*args)


def _mm_kernel(a_ref, w_ref, o_ref, wb_ref):
    @pl.when(pl.program_id(1) == 0)
    def _():
        wb_ref[...] = w_ref[...].astype(BF16)

    o_ref[...] = _dot(a_ref[...], wb_ref[...]).astype(o_ref.dtype)


def _matmul(a, w, m_rows, out_dtype, tm, tn):
    k, n = w.shape
    assert m_rows % tm == 0 and n % tn == 0 and a.shape[1] == k
    osz = jnp.dtype(out_dtype).itemsize
    vmem = 2 * k * tn * 4 + k * tn * 2 + 2 * tm * k * 2 + 2 * tm * tn * osz + tm * tn * 4
    return pl.pallas_call(
        _mm_kernel,
        grid=(n // tn, m_rows // tm),
        in_specs=[pl.BlockSpec((tm, k), lambda j, i: (i, 0)),
                  pl.BlockSpec((k, tn), lambda j, i: (0, j))],
        out_specs=pl.BlockSpec((tm, tn), lambda j, i: (i, j)),
        out_shape=jax.ShapeDtypeStruct((m_rows, n), out_dtype),
        scratch_shapes=[pltpu.VMEM((k, tn), BF16)],
        compiler_params=_cparams(("arbitrary", "arbitrary"), vmem + (4 << 20)),
        name="matmul",
    )(a, w)


def _rope(x, cos_e, sin_s):
    w = x.shape[-1]
    lane = lax.broadcasted_iota(jnp.int32, x.shape, 1)
    nxt = pltpu.roll(x, w - 1, 1)
    prv = pltpu.roll(x, 1, 1)
    swapped = jnp.where(lane % 2 == 0, nxt, prv)
    return x * cos_e + swapped * sin_s


def _rms(x, g):
    return x * lax.rsqrt(jnp.mean(x * x, axis=-1, keepdims=True) + EPS) * g


def _mla_prep1_kernel(lat_ref, qg_ref, kvg_ref, cos_ref, sin_ref, nq_ref, nkv_ref, kr_ref):
    lat = lat_ref[...]
    nq_ref[...] = _rms(lat[:, :MLA_Q_RANK], qg_ref[...]).astype(BF16)
    nkv_ref[...] = _rms(lat[:, MLA_Q_RANK:MLA_Q_RANK + MLA_KV_RANK], kvg_ref[...]).astype(BF16)
    kr = lat[:, MLA_Q_RANK + MLA_KV_RANK:]
    kr_ref[...] = _rope(kr, cos_ref[...], sin_ref[...]).astype(BF16)


def _mla_prep1(lat, q_norm, kv_norm, cos_t, sin_t):
    n, wdt = lat.shape
    tm = ROW_TILE
    row = lambda w: pl.BlockSpec((tm, w), lambda i: (i, 0))
    one = lambda w: pl.BlockSpec((1, w), lambda i: (0, 0))
    return pl.pallas_call(
        _mla_prep1_kernel,
        grid=(n // tm,),
        in_specs=[row(wdt), one(MLA_Q_RANK), one(MLA_KV_RANK), row(128), row(128)],
        out_specs=[row(MLA_Q_RANK), row(MLA_KV_RANK), row(128)],
        out_shape=[jax.ShapeDtypeStruct((n, MLA_Q_RANK), BF16),
                   jax.ShapeDtypeStruct((n, MLA_KV_RANK), BF16),
                   jax.ShapeDtypeStruct((n, 128), BF16)],
        compiler_params=_cparams(("arbitrary",), 8 << 20),
        name="mla_prep1",
    )(lat, q_norm.reshape(1, -1), kv_norm.reshape(1, -1), cos_t, sin_t)


def _mla_prep2_kernel(q_ref, kn_ref, kr_ref, cos_ref, sin_ref, qo_ref, ko_ref, *, scale):
    cos_e = cos_ref[...]
    sin_s = sin_ref[...]
    kr = kr_ref[...]
    for h in range(MLA_HEADS):
        lo = h * MLA_QK_PAD
        qo_ref[:, lo:lo + MLA_NOPE] = (q_ref[:, lo:lo + MLA_NOPE].astype(F32) * scale).astype(BF16)
        qr = q_ref[:, lo + MLA_NOPE:lo + MLA_QK_PAD].astype(F32)
        qo_ref[:, lo + MLA_NOPE:lo + MLA_QK_PAD] = (_rope(qr, cos_e, sin_s) * scale).astype(BF16)
        ko_ref[:, lo:lo + MLA_NOPE] = kn_ref[:, h * MLA_NOPE:(h + 1) * MLA_NOPE]
        ko_ref[:, lo + MLA_NOPE:lo + MLA_QK_PAD] = kr


def _mla_prep2(q_raw, kv_all, kr, cos_t, sin_t):
    n = q_raw.shape[0]
    tm = ROW_TILE
    wq = MLA_HEADS * MLA_QK_PAD
    row = lambda w: pl.BlockSpec((tm, w), lambda i: (i, 0))
    return pl.pallas_call(
        functools.partial(_mla_prep2_kernel, scale=(MLA_NOPE + MLA_ROPE) ** -0.5 * LOG2E),
        grid=(n // tm,),
        in_specs=[row(wq), row(MLA_HEADS * MLA_NOPE), row(128), row(128), row(128)],
        out_specs=[row(wq), row(wq)],
        out_shape=[jax.ShapeDtypeStruct((n, wq), BF16), jax.ShapeDtypeStruct((n, wq), BF16)],
        compiler_params=_cparams(("arbitrary",), 16 << 20),
        name="mla_prep2",
    )(q_raw, kv_all, kr, cos_t, sin_t)


def _gqa_prep_kernel(qkv_ref, qg_ref, kg_ref, cos_ref, sin_ref, qo_ref, ko_ref, *, scale):
    cos_e = cos_ref[...]
    sin_s = sin_ref[...]
    for h in range(GQA_HEADS):
        xh = qkv_ref[:, h * HEAD_DIM:(h + 1) * HEAD_DIM].astype(F32)
        qo_ref[:, h * HEAD_DIM:(h + 1) * HEAD_DIM] = (
            _rope(_rms(xh, qg_ref[...]), cos_e, sin_s) * scale).astype(BF16)
    for h in range(GQA_KV):
        lo = (GQA_HEADS + h) * HEAD_DIM
        xh = qkv_ref[:, lo:lo + HEAD_DIM].astype(F32)
        ko_ref[:, h * HEAD_DIM:(h + 1) * HEAD_DIM] = _rope(_rms(xh, kg_ref[...]), cos_e, sin_s).astype(BF16)


def _gqa_prep(qkv, q_norm, k_norm, cos_t, sin_t):
    n = qkv.shape[0]
    tm = ROW_TILE
    wqk = (GQA_HEADS + GQA_KV) * HEAD_DIM
    row = lambda w: pl.BlockSpec((tm, w), lambda i: (i, 0))
    one = pl.BlockSpec((1, HEAD_DIM), lambda i: (0, 0))
    return pl.pallas_call(
        functools.partial(_gqa_prep_kernel, scale=HEAD_DIM ** -0.5 * LOG2E),
        grid=(n // tm,),
        in_specs=[row(wqk), one, one, row(HEAD_DIM), row(HEAD_DIM)],
        out_specs=[row(GQA_HEADS * HEAD_DIM), row(GQA_KV * HEAD_DIM)],
        out_shape=[jax.ShapeDtypeStruct((n, GQA_HEADS * HEAD_DIM), BF16),
                   jax.ShapeDtypeStruct((n, GQA_KV * HEAD_DIM), BF16)],
        compiler_params=_cparams(("arbitrary",), 16 << 20),
        name="gqa_prep",
    )(qkv, q_norm.reshape(1, -1), k_norm.reshape(1, -1), cos_t, sin_t)


def _flash_kernel(q_ref, k_ref, v_ref, o_ref, sa_ref, sb_ref, *, group, dk, dv, tk, n_lat, q_scale):
    tq = q_ref.shape[0]
    q = jnp.concatenate([q_ref[:, g * dk:(g + 1) * dk] for g in range(group)], axis=0) if group > 1 else q_ref[...]
    if q_scale is not None:
        q = (q.astype(F32) * q_scale).astype(BF16)

    s = _dot_nt(q, k_ref[SEQ:SEQ + CTX_LEN, :])
    m = jnp.max(s, axis=-1, keepdims=True)
    p = jnp.exp2(s - m)
    l = jnp.sum(p, axis=-1, keepdims=True)
    acc = _dot(p.astype(BF16), v_ref[SEQ:SEQ + CTX_LEN, :])

    if n_lat:
        def rows(ref, j):
            return ref[pl.ds(pl.multiple_of(j * tk, tk), tk), :]

        def update(s, vb, m, l, acc):
            m_new = jnp.maximum(m, jnp.max(s, axis=-1, keepdims=True))
            alpha = jnp.exp2(m - m_new)
            p = jnp.exp2(s - m_new)
            l = alpha * l + jnp.sum(p, axis=-1, keepdims=True)
            return m_new, l, alpha * acc + _dot(p.astype(BF16), vb)

        def pair(jj, carry, last):
            j0 = 2 * jj
            sb_ref[...] = _dot_nt(q, rows(k_ref, j0 + 1))
            carry = update(sa_ref[...], rows(v_ref, j0), *carry)
            if not last:
                sa_ref[...] = _dot_nt(q, rows(k_ref, j0 + 2))
            return update(sb_ref[...], rows(v_ref, j0 + 1), *carry)

        sa_ref[...] = _dot_nt(q, rows(k_ref, 0))
        carry = lax.fori_loop(0, n_lat // 2 - 1, lambda jj, c: pair(jj, c, False), (m, l, acc))
        m, l, acc = pair(n_lat // 2 - 1, carry, True)

    o = (acc / l).astype(o_ref.dtype)
    for g in range(group):
        o_ref[:, g * dv:(g + 1) * dv] = o[g * tq:(g + 1) * tq]


def _flash(q_arr, k_arr, v_arr, *, n_kv_heads, group, dk, dv, q_col0, k_col0, v_col0, ctx_queries, tq, tk,
           q_scale=None):
    n_tok = k_arr.shape[0]
    if ctx_queries:
        n_q, q_blk0, n_lat = CTX_LEN, SEQ // tq, 0
    else:
        n_q, q_blk0, n_lat = SEQ, 0, SEQ // tk
        assert n_lat % 2 == 0
    m = group * tq
    s_shape = (m, tk) if n_lat else (8, 128)
    vmem = (2 * n_tok * (dk + dv) * 2 + 4 * tq * group * (dk + dv) * 2
            + 5 * m * max(tk, CTX_LEN) * 4 + 4 * m * dv * 4)
    return pl.pallas_call(
        functools.partial(_flash_kernel, group=group, dk=dk, dv=dv, tk=tk, n_lat=n_lat, q_scale=q_scale),
        grid=(n_kv_heads, n_q // tq),
        in_specs=[pl.BlockSpec((tq, group * dk), lambda h, i: (q_blk0 + i, q_col0 + h)),
                  pl.BlockSpec((n_tok, dk), lambda h, i: (0, k_col0 + h), pipeline_mode=pl.Buffered(1)),
                  pl.BlockSpec((n_tok, dv), lambda h, i: (0, v_col0 + h), pipeline_mode=pl.Buffered(1))],
        out_specs=pl.BlockSpec((tq, group * dv), lambda h, i: (i, h)),
        out_shape=jax.ShapeDtypeStruct((n_q, n_kv_heads * group * dv), BF16),
        scratch_shapes=[pltpu.VMEM(s_shape, F32), pltpu.VMEM(s_shape, F32)],
        compiler_params=_cparams(("arbitrary", "arbitrary"), vmem + (8 << 20)),
        name="flash_ctx" if ctx_queries else "flash",
    )(q_arr, k_arr, v_arr)


def _na_start_row(b):
    return jnp.clip(NA_RB * b - NA_KH // 2, 0, GRID_H - NA_U)


def _na_kernel(q_ref, k_ref, v_ref, bias_ref, o_ref, *, scale):
    b = pl.program_id(1)
    start = pl.multiple_of(_na_start_row(b) * GRID_W, GRID_W)
    nwin = NA_U * GRID_W
    q = (q_ref[...].astype(F32) * scale).astype(BF16)
    s_loc = _dot_nt(q, k_ref[pl.ds(start, nwin), :]) + bias_ref[...]
    s_ctx = _dot_nt(q, k_ref[SEQ:SEQ + CTX_LEN, :])
    m = jnp.maximum(jnp.max(s_loc, axis=-1, keepdims=True), jnp.max(s_ctx, axis=-1, keepdims=True))
    p_loc = jnp.exp(s_loc - m)
    p_ctx = jnp.exp(s_ctx - m)
    l = jnp.sum(p_loc, axis=-1, keepdims=True) + jnp.sum(p_ctx, axis=-1, keepdims=True)
    o = _dot(p_loc.astype(BF16), v_ref[pl.ds(start, nwin), :]) + _dot(p_ctx.astype(BF16), v_ref[SEQ:SEQ + CTX_LEN, :])
    o_ref[...] = (o / l).astype(o_ref.dtype)


def _na_bias_tables(rpb):
    h = rpb.shape[0]
    w = np.arange(GRID_W)
    c0 = np.clip(w - NA_KW // 2, 0, GRID_W - NA_KW)
    colmask = (w[None, :] >= c0[:, None]) & (w[None, :] < c0[:, None] + NA_KW)
    colidx = np.clip(w[None, :] - w[:, None] + NA_KW - 1, 0, 2 * NA_KW - 2)
    slabs = jnp.where(colmask[None, None], rpb[:, :, colidx], NEG)
    slabs = jnp.concatenate([slabs, jnp.full((h, 1, GRID_W, GRID_W), NEG, rpb.dtype)], axis=1)
    masked = 2 * NA_KH - 1
    nb = GRID_H // NA_RB
    tabs = []
    for b in (0, 1, nb - 1):
        start = int(np.clip(NA_RB * b - NA_KH // 2, 0, GRID_H - NA_U))
        a = np.full((NA_RB, NA_U), masked, np.int32)
        for j in range(NA_RB):
            r = NA_RB * b + j
            r0 = int(np.clip(r - NA_KH // 2, 0, GRID_H - NA_KH))
            for u in range(NA_U):
                kr = start + u
                if r0 <= kr < r0 + NA_KH:
                    a[j, u] = kr - r + NA_KH - 1
        t = slabs[:, a]
        tabs.append(t.transpose(0, 1, 3, 2, 4).reshape(h, NA_RB * GRID_W, NA_U * GRID_W))
    return jnp.stack(tabs, axis=1)


def _na_attention(qkv, bias_tabs):
    n_tok = qkv.shape[0]
    nb = GRID_H // NA_RB
    tq = NA_RB * GRID_W
    nwin = NA_U * GRID_W

    def cls(b):
        return jnp.where(b == 0, 0, jnp.where(b == nb - 1, 2, 1))

    vmem = 4 * n_tok * HEAD_DIM * 2 + 2 * tq * nwin * 4 + 4 * tq * (nwin + CTX_LEN) * 4
    return pl.pallas_call(
        functools.partial(_na_kernel, scale=HEAD_DIM ** -0.5),
        grid=(NA_HEADS, nb),
        in_specs=[pl.BlockSpec((tq, HEAD_DIM), lambda h, b: (b, h)),
                  pl.BlockSpec((n_tok, HEAD_DIM), lambda h, b: (0, NA_HEADS + h)),
                  pl.BlockSpec((n_tok, HEAD_DIM), lambda h, b: (0, 2 * NA_HEADS + h)),
                  pl.BlockSpec((None, None, tq, nwin), lambda h, b: (h, cls(b), 0, 0))],
        out_specs=pl.BlockSpec((tq, HEAD_DIM), lambda h, b: (b, h)),
        out_shape=jax.ShapeDtypeStruct((SEQ, NA_HEADS * HEAD_DIM), BF16),
        compiler_params=_cparams(("arbitrary", "arbitrary"), vmem + (8 << 20)),
        name="na_attention",
    )(qkv, qkv, qkv, bias_tabs)


def _dispatch_kernel(tail_ref, nu_ref, slot_ref, h_ref, xg_hbm, zbuf, sem, zsem, *, tb, n_blocks):
    te = EXPERT_TILE

    @pl.when(pl.program_id(0) == 0)
    def _():
        zbuf[...] = jnp.zeros_like(zbuf)

        def zero_tile(row0, wait):
            cp = pltpu.make_async_copy(zbuf, xg_hbm.at[pl.ds(pl.multiple_of(row0, te), te)], zsem)
            if wait:
                cp.wait()
            else:
                cp.start()

        def zero_tail(e, wait):
            @pl.when(tail_ref[e] >= 0)
            def _():
                zero_tile(tail_ref[e], wait)

        for wait in (False, True):
            lax.fori_loop(0, N_EXPERTS, lambda e, c: (zero_tail(e, wait), c)[1], 0)
            lax.fori_loop(nu_ref[0], n_blocks, lambda b, c: (zero_tile(b * te, wait), c)[1], 0)

    def body(t, carry):
        for k in range(MOE_TOP_K):
            pltpu.make_async_copy(h_ref.at[pl.ds(t, 1)], xg_hbm.at[pl.ds(slot_ref[k, t], 1)], sem).start(
                priority=k % 2)
        return carry

    lax.fori_loop(0, tb, body, 0, unroll=DMA_LOOP_UNROLL)
    for k in range(MOE_TOP_K):
        pltpu.make_async_copy(h_ref, xg_hbm.at[pl.ds(0, tb)], sem).wait()


def _dispatch(h32, slot_t, tail_start, n_used, n_rows, n_slots):
    d = h32.shape[1]
    tb = ROW_TILE
    grid_spec = pltpu.PrefetchScalarGridSpec(
        num_scalar_prefetch=2,
        grid=(n_rows // tb,),
        in_specs=[pl.BlockSpec((MOE_TOP_K, tb), lambda i, tl, nu: (0, i), memory_space=pltpu.SMEM),
                  pl.BlockSpec((tb, d), lambda i, tl, nu: (i, 0))],
        out_specs=pl.BlockSpec(memory_space=pl.ANY),
        scratch_shapes=[pltpu.VMEM((EXPERT_TILE, d), h32.dtype), pltpu.SemaphoreType.DMA(()),
                        pltpu.SemaphoreType.DMA(())],
    )
    return pl.pallas_call(
        functools.partial(_dispatch_kernel, tb=tb, n_blocks=n_slots // EXPERT_TILE),
        grid_spec=grid_spec,
        out_shape=jax.ShapeDtypeStruct((n_slots, d), h32.dtype),
        compiler_params=_cparams(("arbitrary",), 16 << 20),
        name="moe_dispatch",
    )(tail_start, n_used, slot_t, h32)


FFN_LEAD = 2


def _ffn_kernel(be_ref, ord_ref, nu_ref, x_ref, wg_ref, wu_ref, wd_ref, o_ref, wgb, wub, wdb, *, packed, nb):
    b = pl.program_id(0)
    c = b - FFN_LEAD

    def at(ref, i):
        return ref[jnp.clip(i, 0, nb - 1)]

    def starts_expert(i):
        return jnp.logical_or(b == 0, at(be_ref, i) != at(be_ref, i - 1))

    @pl.when(starts_expert(c))
    def _():
        wdb[at(ord_ref, c) % 2] = wd_ref[...].astype(BF16)

    @pl.when(jnp.logical_and(c >= 0, c < nu_ref[0]))
    def _():
        slot = at(ord_ref, c) % 2
        x = _unpack_bf16_pairs(x_ref[...]).astype(BF16) if packed else x_ref[...]
        g = _dot(x, wgb[slot])
        u = _dot(x, wub[slot])
        a = (g * jax.nn.sigmoid(g) * u).astype(BF16)
        y = _dot(a, wdb[slot])
        o_ref[...] = _pack_bf16_pairs(y) if packed else y.astype(o_ref.dtype)

    @pl.when(c >= nu_ref[0])
    def _():
        o_ref[...] = jnp.zeros_like(o_ref)

    @pl.when(starts_expert(b))
    def _():
        wgb[at(ord_ref, b) % 2] = wg_ref[...].astype(BF16)

    @pl.when(starts_expert(b - 1))
    def _():
        wub[at(ord_ref, b - 1) % 2] = wu_ref[...].astype(BF16)


def _expert_ffn(x_rows, block_expert, n_used, wg, wu, wd, layer, tb, packed):
    p, dx = x_rows.shape
    d, de = wg.shape[-2:]
    nb = p // tb
    out_dtype = x_rows.dtype
    vmem = 2 * 3 * d * de * 4 + 2 * 3 * d * de * 2 + 4 * tb * d * 2 + 4 * tb * de * 4 + 3 * tb * d * 4
    idx = jnp.arange(nb, dtype=jnp.int32)
    be = jnp.where(idx < n_used[0], block_expert, block_expert[jnp.maximum(n_used[0] - 1, 0)])
    ordinal = jnp.cumsum(jnp.concatenate([jnp.zeros((1,), jnp.int32),
                                          (be[1:] != be[:-1]).astype(jnp.int32)])).astype(jnp.int32)

    def blk(i):
        return jnp.clip(i, 0, nb - 1)

    def wspec(shape, lag):
        return pl.BlockSpec((None, None) + shape, lambda b, be, od, nu: (layer, be[blk(b - lag)], 0, 0))

    grid_spec = pltpu.PrefetchScalarGridSpec(
        num_scalar_prefetch=3,
        grid=(nb + FFN_LEAD,),
        in_specs=[pl.BlockSpec((tb, dx), lambda b, be, od, nu: (jnp.minimum(blk(b - FFN_LEAD), nu[0] - 1), 0)),
                  wspec((d, de), 0), wspec((d, de), 1), wspec((de, d), 2)],
        out_specs=pl.BlockSpec((tb, dx), lambda b, be, od, nu: (blk(b - FFN_LEAD), 0)),
        scratch_shapes=[pltpu.VMEM((2, d, de), BF16), pltpu.VMEM((2, d, de), BF16), pltpu.VMEM((2, de, d), BF16)],
    )
    return pl.pallas_call(
        functools.partial(_ffn_kernel, packed=packed, nb=nb),
        grid_spec=grid_spec,
        out_shape=jax.ShapeDtypeStruct((p, dx), out_dtype),
        compiler_params=_cparams(("arbitrary",), vmem + (4 << 20)),
        name="expert_ffn",
    )(be, ordinal, n_used, x_rows, wg, wu, wd)


def _combine_kernel(*refs, tc, nblk, modulated, emit_x):
    it = iter(refs)
    s_cur, s_nxt, y_hbm, ew_ref, ysh_ref, x_ref, gate_ref, g_ref = [next(it) for _ in range(8)]
    if modulated:
        shift_ref, scale_ref = next(it), next(it)
    if emit_x:
        o_ref = next(it)
    h_ref, buf, sem = next(it), next(it), next(it)
    i = pl.program_id(0)

    def gather(slot_ref, bslot):
        def body(t, carry):
            for k in range(MOE_TOP_K):
                pltpu.make_async_copy(y_hbm.at[pl.ds(slot_ref[k, t], 1)],
                                      buf.at[bslot, k, pl.ds(t, 1)], sem.at[bslot]).start()
            return carry

        lax.fori_loop(0, tc, body, 0, unroll=DMA_LOOP_UNROLL)

    @pl.when(i == 0)
    def _():
        gather(s_cur, 0)

    cur = i % 2

    @pl.when(i + 1 < nblk)
    def _():
        @pl.when(cur == 0)
        def _():
            gather(s_nxt, 1)

        @pl.when(cur == 1)
        def _():
            gather(s_nxt, 0)

    def finish(bslot):
        for k in range(MOE_TOP_K):
            pltpu.make_async_copy(y_hbm.at[pl.ds(0, tc)], buf.at[bslot, k], sem.at[bslot]).wait()
        ew = ew_ref[...]
        acc = ysh_ref[...].astype(F32)
        for k in range(MOE_TOP_K):
            acc = acc + ew[:, k:k + 1] * _unpack_bf16_pairs(buf[bslot, k])
        xn = x_ref[...] + gate_ref[...] * acc
        if emit_x:
            o_ref[...] = xn
        hn = _rms(xn, g_ref[...])
        if modulated:
            hn = hn * (1.0 + scale_ref[...]) + shift_ref[...]
        h_ref[...] = hn.astype(h_ref.dtype)

    @pl.when(cur == 0)
    def _():
        finish(0)

    @pl.when(cur == 1)
    def _():
        finish(1)


def _combine(y, slot_t, ew, ysh, x, gate, n_rows, gain, mod):
    d = x.shape[1]
    tc = COMBINE_TILE
    nblk = n_rows // tc
    lat_blocks = SEQ // tc
    row = pl.BlockSpec((tc, d), lambda i: (i, 0))
    vec = pl.BlockSpec((None, 1, d), lambda i: (jnp.where(i >= lat_blocks, 1, 0), 0, 0))
    vmem = 2 * MOE_TOP_K * tc * d * 2 + 12 * tc * d * 4
    args = [slot_t, slot_t, y, ew, ysh, x, gate.reshape(2, 1, d), gain.reshape(1, d)]
    in_specs = [pl.BlockSpec((MOE_TOP_K, tc), lambda i: (0, i), memory_space=pltpu.SMEM),
                pl.BlockSpec((MOE_TOP_K, tc), lambda i: (0, jnp.minimum(i + 1, nblk - 1)),
                             memory_space=pltpu.SMEM),
                pl.BlockSpec(memory_space=pl.ANY),
                pl.BlockSpec((tc, MOE_TOP_K), lambda i: (i, 0)),
                row, row, vec, pl.BlockSpec((1, d), lambda i: (0, 0))]
    if mod is not None:
        args += [mod[0].reshape(2, 1, d), mod[1].reshape(2, 1, d)]
        in_specs += [vec, vec]
        out_shape = [jax.ShapeDtypeStruct((n_rows, d), F32), jax.ShapeDtypeStruct((n_rows, d), BF16)]
    else:
        out_shape = [jax.ShapeDtypeStruct((n_rows, d), F32)]
    return pl.pallas_call(
        functools.partial(_combine_kernel, tc=tc, nblk=nblk, modulated=mod is not None, emit_x=mod is not None),
        grid=(nblk,),
        in_specs=in_specs,
        out_specs=[row] * len(out_shape),
        out_shape=out_shape,
        scratch_shapes=[pltpu.VMEM((2, MOE_TOP_K, tc, d // 2), jnp.uint32), pltpu.SemaphoreType.DMA((2,))],
        compiler_params=_cparams(("arbitrary",), vmem + (4 << 20)),
        name="moe_combine",
    )(*args)


def _routing_tables(counts, n_rows):
    tb = EXPERT_TILE
    n_blocks = -(-(n_rows * MOE_TOP_K) // tb) + N_EXPERTS
    padded = (counts + tb - 1) // tb * tb
    pad_end = jnp.cumsum(padded)
    pad_start = (pad_end - padded).astype(jnp.int32)
    tail_start = jnp.where(padded > 0, pad_end - tb, -1).astype(jnp.int32)
    blk_start = jnp.arange(n_blocks, dtype=jnp.int32) * tb
    block_expert = jnp.sum(blk_start[:, None] >= pad_end[None, :], axis=1)
    block_expert = jnp.minimum(block_expert, N_EXPERTS - 1).astype(jnp.int32)
    n_used = (pad_end[-1] // tb).astype(jnp.int32).reshape(1)
    return pad_start, tail_start, block_expert, n_used, n_blocks * tb


def _moe(layer, x_new, h, h32, eidx_t, ew_t, rank_t, counts, gate, n_rows, wg, wu, wd, sg, su, sd, next_gain, next_mod):
    pad_start, tail_start, block_expert, n_used, n_slots = _routing_tables(counts[:, 0], n_rows)
    e_iota = jnp.arange(N_EXPERTS, dtype=jnp.int32)
    slot_t = jnp.sum(jnp.where(eidx_t[:, :, None] == e_iota, pad_start, 0), axis=-1) + rank_t
    xg = _dispatch(h32, slot_t, tail_start, n_used, n_rows, n_slots)
    y = _expert_ffn(xg, block_expert, n_used, wg, wu, wd, layer, EXPERT_TILE, True)
    tm = MM_TM if n_rows % MM_TM == 0 else 512
    ysh = _expert_ffn(h, jnp.zeros((n_rows // tm,), jnp.int32), jnp.full((1,), n_rows // tm, jnp.int32),
                      sg[:, None], su[:, None], sd[:, None], layer, tm, False)
    return _combine(y, slot_t, ew_t.T, ysh, x_new, gate, n_rows, next_gain, next_mod)


def _rope_tables(d, width):
    t = jnp.arange(SEQ)
    row = (t // GRID_W).astype(F32)
    col = (t % GRID_W).astype(F32)
    n = d // 4
    inv = ROPE_THETA ** (-jnp.arange(n, dtype=F32) / n)
    ang = jnp.concatenate([row[:, None] * inv, col[:, None] * inv], axis=-1)
    ang = jnp.concatenate([ang, jnp.zeros((CTX_LEN, d // 2), F32)], axis=0)
    cos = jnp.repeat(jnp.cos(ang), 2, axis=1)
    sin = jnp.repeat(jnp.sin(ang), 2, axis=1) * jnp.tile(jnp.array([-1.0, 1.0], F32), d // 2)
    if width > d:
        cos = jnp.concatenate([cos, jnp.ones((N_TOK, width - d), F32)], axis=1)
        sin = jnp.concatenate([sin, jnp.zeros((N_TOK, width - d), F32)], axis=1)
    return cos, sin


def _mixer_even(h, w_in, rpb, q_norm, w_uq, kv_norm, w_ukv, w_out):
    n_na = 3 * NA_HEADS * HEAD_DIM
    qkv = _matmul(h, w_in[:, :n_na], N_TOK, BF16, MM_TM, 1024)
    w_tail = jnp.pad(w_in[:, n_na:], ((0, 0), (0, 64)))
    lat = _matmul(h, w_tail, N_TOK, F32, MM_TM, 384)
    cos_t, sin_t = _rope_tables(MLA_ROPE, 128)
    nq, nkv, kr = _mla_prep1(lat, q_norm, kv_norm, cos_t, sin_t)
    wq = w_uq.reshape(MLA_Q_RANK, MLA_HEADS, MLA_NOPE + MLA_ROPE)
    wq = jnp.pad(wq, ((0, 0), (0, 0), (0, MLA_QK_PAD - MLA_NOPE - MLA_ROPE))).reshape(MLA_Q_RANK, -1)
    wkv = w_ukv.reshape(MLA_KV_RANK, MLA_HEADS, MLA_NOPE + MLA_V)
    wkv = jnp.concatenate([wkv[:, :, :MLA_NOPE].reshape(MLA_KV_RANK, -1),
                           wkv[:, :, MLA_NOPE:].reshape(MLA_KV_RANK, -1)], axis=1)
    q_raw = _matmul(nq, wq, N_TOK, F32, MM_TM, 1024)
    kv_all = _matmul(nkv, wkv, N_TOK, BF16, MM_TM, 1024)
    q_mla, k_cat = _mla_prep2(q_raw, kv_all, kr, cos_t, sin_t)

    o_na = _na_attention(qkv, _na_bias_tables(rpb))
    mla = dict(n_kv_heads=MLA_HEADS, group=1, dk=MLA_QK_PAD, dv=MLA_V, q_col0=0, k_col0=0, v_col0=MLA_HEADS)
    o_mla = _flash(q_mla, k_cat, kv_all, ctx_queries=False, tq=1024, tk=1024, **mla)
    o_mla_c = _flash(q_mla, k_cat, kv_all, ctx_queries=True, tq=256, tk=512, **mla)
    na = dict(n_kv_heads=NA_HEADS, group=1, dk=HEAD_DIM, dv=HEAD_DIM, q_col0=0, k_col0=NA_HEADS, v_col0=2 * NA_HEADS)
    o_na_c = _flash(qkv, qkv, qkv, ctx_queries=True, tq=256, tk=512, q_scale=HEAD_DIM ** -0.5 * LOG2E, **na)
    o = jnp.concatenate([jnp.concatenate([o_na, o_mla], axis=1),
                         jnp.concatenate([o_na_c, o_mla_c], axis=1)], axis=0)
    return _matmul(o, w_out, N_TOK, F32, MM_TM, 1024)


def _mixer_odd_last(h, w_qkv, q_norm, k_norm, w_o):
    qkv = _matmul(h, w_qkv, N_TOK, BF16, MM_TM, 1024)
    cos_t, sin_t = _rope_tables(HEAD_DIM, HEAD_DIM)
    q_g, k_g = _gqa_prep(qkv, q_norm, k_norm, cos_t, sin_t)
    o = _flash(q_g, k_g, qkv, n_kv_heads=GQA_KV, group=GQA_GROUP, dk=HEAD_DIM, dv=HEAD_DIM,
               q_col0=0, k_col0=0, v_col0=GQA_HEADS + GQA_KV, ctx_queries=False, tq=512, tk=1024)
    return _matmul(o, w_o, SEQ, F32, 512, 1024)


def kernel(x, c, ctx, c_ctx, ada_w, ada_b, mix_norm, ffn_norm, ev_w_in, ev_rpb, ev_q_norm, ev_w_uq, ev_kv_norm, ev_w_ukv, ev_w_out, od_w_qkv, od_q_norm, od_k_norm, od_w_o, router_w, router_b, exp_w_gate, exp_w_up, exp_w_down, sh_w_gate, sh_w_up, sh_w_down, final_norm):
    assert DEPTH == 2 and x.shape == (1, SEQ, D_MODEL)
    mods = _adaln_mods(c, c_ctx, ada_w, ada_b)
    xt = jnp.concatenate([x[0], ctx[0]], axis=0)

    def parts(i):
        return [mods[i, :, k * D_MODEL:(k + 1) * D_MODEL] for k in range(6)]

    sh1, sc1, g1, sh2, sc2, g2 = parts(0)
    (h,) = _norm_call(xt, N_TOK, mix_norm[0], mod=(sh1, sc1))
    o = _mixer_even(h, ev_w_in[0], ev_rpb[0], ev_q_norm[0], ev_w_uq[0], ev_kv_norm[0], ev_w_ukv[0], ev_w_out[0])
    xt, h, h32, eidx_t, ew_t, rank_t, counts = _norm_call(xt, N_TOK, ffn_norm[0], resid=(o, g1), mod=(sh2, sc2),
                                                          router=(router_w[0], router_b[0]))
    n1_sh1, n1_sc1, n1_g1, n1_sh2, n1_sc2, n1_g2 = parts(1)
    xt, h = _moe(0, xt, h, h32, eidx_t, ew_t, rank_t, counts, g2, N_TOK, exp_w_gate, exp_w_up, exp_w_down,
                 sh_w_gate, sh_w_up, sh_w_down, mix_norm[1], (n1_sh1, n1_sc1))

    o = _mixer_odd_last(h, od_w_qkv[0], od_q_norm[0], od_k_norm[0], od_w_o[0])
    xl, h, h32, eidx_t, ew_t, rank_t, counts = _norm_call(xt, SEQ, ffn_norm[1], resid=(o, n1_g1),
                                                          mod=(n1_sh2, n1_sc2),
                                                          router=(router_w[1], router_b[1]))
    (out,) = _moe(1, xl, h, h32, eidx_t, ew_t, rank_t, counts, n1_g2, SEQ, exp_w_gate, exp_w_up, exp_w_down,
                  sh_w_gate, sh_w_up, sh_w_down, final_norm, None)
    return out[None]
```

```python
import functools

import numpy as np
import jax
import jax.numpy as jnp
from jax import lax
from jax.experimental import pallas as pl
from jax.experimental.pallas import tpu as pltpu

F32 = jnp.float32
BF16 = jnp.bfloat16

D_MODEL = 2048
SEQ = 8192
DEPTH = 2
GRID_W = 64
GRID_H = SEQ // GRID_W
CTX_LEN = 256
N_TOK = SEQ + CTX_LEN
EPS = 1e-6
ROPE_THETA = 10000.0
HEAD_DIM = 128
NA_HEADS = 8
NA_KH = 8
NA_KW = 16
MLA_HEADS = 8
MLA_Q_RANK = 512
MLA_KV_RANK = 512
MLA_NOPE = 128
MLA_ROPE = 64
MLA_V = 128
MLA_QK_PAD = 256
GQA_HEADS = 16
GQA_KV = 4
GQA_GROUP = GQA_HEADS // GQA_KV
N_EXPERTS = 64
N_GROUPS = 8
GROUP_SIZE = N_EXPERTS // N_GROUPS
TOPK_GROUPS = 4
MOE_TOP_K = 8
D_EXPERT = 512
ROUTED_SCALE = 2.5

NEG = -1e30
LOG2E = 1.4426950408889634

VMEM_LIMIT_CAP = 60 * 1024 * 1024

ROW_TILE = 256
MM_TM = 768
NA_RB = 4
NA_U = NA_RB + NA_KH - 1
EXPERT_TILE = 256
COMBINE_TILE = 128
DMA_LOOP_UNROLL = 128


def _cparams(sem, vmem_bytes):
    return pltpu.CompilerParams(
        dimension_semantics=sem,
        vmem_limit_bytes=int(min(max(vmem_bytes, 16 * 1024 * 1024), VMEM_LIMIT_CAP)),
    )


def _pack_bf16_pairs(x):
    w = x.shape[1] // 2

    def rounded(v):
        b = lax.bitcast_convert_type(v, jnp.uint32)
        return b + (((b >> 16) & jnp.uint32(1)) + jnp.uint32(0x7FFF))

    return (rounded(x[:, :w]) >> 16) | (rounded(x[:, w:]) & jnp.uint32(0xFFFF0000))


def _unpack_bf16_pairs(p):
    lo = lax.bitcast_convert_type(p << 16, F32)
    hi = lax.bitcast_convert_type(p & jnp.uint32(0xFFFF0000), F32)
    return jnp.concatenate([lo, hi], axis=1)


def _dot(a, b):
    return jnp.dot(a, b, preferred_element_type=F32)


def _dot_nt(a, b):
    return lax.dot_general(a, b, (((1,), (1,)), ((), ())), preferred_element_type=F32)


def _mods_kernel(cc_ref, w_ref, b_ref, o_ref, acc_ref, *, nk):
    k = pl.program_id(2)

    @pl.when(k == 0)
    def _():
        acc_ref[...] = jnp.zeros_like(acc_ref)

    a = cc_ref[...]
    a = a * jax.nn.sigmoid(a)
    w = w_ref[...]
    tk, tn = w.shape
    w3 = w.reshape(tk // 8, 8, tn)
    for v in range(2):
        col = a[:, v:v + 1].reshape(tk // 8, 8, 1)
        acc_ref[v] += jnp.sum(w3 * col, axis=0)

    @pl.when(k == nk - 1)
    def _():
        o_ref[...] = jnp.sum(acc_ref[...], axis=1) + b_ref[...]


def _adaln_mods(c, c_ctx, ada_w, ada_b):
    depth, d, n = ada_w.shape
    tk, tn = 1024, 1536
    nk = d // tk
    cc = jnp.stack([c[0], c_ctx], axis=1)
    return pl.pallas_call(
        functools.partial(_mods_kernel, nk=nk),
        grid=(depth, n // tn, nk),
        in_specs=[
            pl.BlockSpec((tk, 2), lambda i, j, k: (k, 0)),
            pl.BlockSpec((None, tk, tn), lambda i, j, k: (i, k, j)),
            pl.BlockSpec((None, 1, tn), lambda i, j, k: (i, 0, j)),
        ],
        out_specs=pl.BlockSpec((None, 2, tn), lambda i, j, k: (i, 0, j)),
        out_shape=jax.ShapeDtypeStruct((depth, 2, n), F32),
        scratch_shapes=[pltpu.VMEM((2, 8, tn), F32)],
        compiler_params=_cparams(("arbitrary", "arbitrary", "arbitrary"), 4 * tk * tn * 4),
        name="adaln_mods",
    )(cc, ada_w, ada_b.reshape(depth, 1, n))


def _router_select(hf, rwt_ref, rb_ref, base_ref):
    logits = lax.dot_general(rwt_ref[...], hf, (((1,), (1,)), ((), ())),
                             precision=lax.Precision.HIGHEST, preferred_element_type=F32)
    scores = jax.nn.sigmoid(logits)
    sel = scores + rb_ref[...]
    tm = sel.shape[1]
    ninf = jnp.float32(-jnp.inf)
    sub = lax.broadcasted_iota(jnp.int32, (GROUP_SIZE, tm), 0)
    gi = lax.broadcasted_iota(jnp.int32, (N_GROUPS, tm), 0)
    gs = jnp.zeros((N_GROUPS, tm), F32)
    for g in range(N_GROUPS):
        blk = sel[g * GROUP_SIZE:(g + 1) * GROUP_SIZE, :]
        m1 = jnp.max(blk, axis=0, keepdims=True)
        first = jnp.min(jnp.where(blk == m1, sub, GROUP_SIZE), axis=0, keepdims=True)
        m2 = jnp.max(jnp.where(sub == first, ninf, blk), axis=0, keepdims=True)
        gs = jnp.where(gi == g, m1 + m2, gs)
    gsel = jnp.zeros((N_GROUPS, tm), F32)
    cur = gs
    for _ in range(TOPK_GROUPS):
        mx = jnp.max(cur, axis=0, keepdims=True)
        idx = jnp.min(jnp.where(cur == mx, gi, N_GROUPS), axis=0, keepdims=True)
        pick = gi == idx
        gsel = jnp.where(pick, 1.0, gsel)
        cur = jnp.where(pick, ninf, cur)
    ei = lax.broadcasted_iota(jnp.int32, (N_EXPERTS, tm), 0)
    eg = ei // GROUP_SIZE
    emask = jnp.zeros((N_EXPERTS, tm), F32)
    for g in range(N_GROUPS):
        emask = jnp.where(eg == g, gsel[g:g + 1, :], emask)
    cur = jnp.where(emask > 0.0, sel, ninf)
    idxs, ws, picks = [], [], []
    for _ in range(MOE_TOP_K):
        mx = jnp.max(cur, axis=0, keepdims=True)
        idx = jnp.min(jnp.where(cur == mx, ei, N_EXPERTS), axis=0, keepdims=True)
        pick = ei == idx
        ws.append(jnp.sum(jnp.where(pick, scores, 0.0), axis=0, keepdims=True))
        idxs.append(idx)
        picks.append(pick)
        cur = jnp.where(pick, ninf, cur)
    total = ws[0]
    for w in ws[1:]:
        total = total + w
    ws = [w / total * ROUTED_SCALE for w in ws]
    onehot = jnp.zeros((N_EXPERTS, tm), F32)
    for pick in picks:
        onehot = jnp.where(pick, 1.0, onehot)
    tri = (lax.broadcasted_iota(jnp.int32, (tm, tm), 0) < lax.broadcasted_iota(jnp.int32, (tm, tm), 1))
    before = _dot(onehot.astype(BF16), jnp.where(tri, 1.0, 0.0).astype(BF16)) + base_ref[...]
    ranks = [jnp.sum(jnp.where(pick, before, 0.0), axis=0, keepdims=True).astype(jnp.int32) for pick in picks]
    base_ref[...] += jnp.sum(onehot, axis=1, keepdims=True)
    return idxs, ws, ranks


def _norm_kernel(*refs, has_resid, modulated, with_router, out_dtype):
    it = iter(refs)
    x_ref = next(it)
    if has_resid:
        o_ref = next(it)
        gate_ref = next(it)
    g_ref = next(it)
    if modulated:
        shift_ref = next(it)
        scale_ref = next(it)
    if with_router:
        rwt_ref = next(it)
        rb_ref = next(it)
    if has_resid:
        xn_ref = next(it)
    h_ref = next(it)
    if with_router:
        hp_ref = next(it)
        eidx_ref = next(it)
        ew_ref = next(it)
        rank_ref = next(it)
        cnt_ref = next(it)
        base_ref = next(it)

        @pl.when(pl.program_id(0) == 0)
        def _():
            base_ref[...] = jnp.zeros_like(base_ref)

    x = x_ref[...]
    if has_resid:
        x = x + gate_ref[...] * o_ref[...].astype(F32)
        xn_ref[...] = x
    y = x * lax.rsqrt(jnp.mean(x * x, axis=-1, keepdims=True) + EPS)
    y = y * g_ref[...]
    if modulated:
        y = y * (1.0 + scale_ref[...]) + shift_ref[...]
    h_ref[...] = y.astype(out_dtype)
    if with_router:
        hp_ref[...] = _pack_bf16_pairs(y)
        idxs, ws, ranks = _router_select(y, rwt_ref, rb_ref, base_ref)
        for k in range(MOE_TOP_K):
            eidx_ref[k:k + 1, :] = idxs[k]
            ew_ref[k:k + 1, :] = ws[k]
            rank_ref[k:k + 1, :] = ranks[k]
        cnt_ref[...] = base_ref[...].astype(jnp.int32)


def _norm_call(x, n_rows, gain, *, resid=None, mod=None, router=None, out_dtype=BF16):
    d = x.shape[1]
    tm = ROW_TILE
    nblk = n_rows // tm
    lat_blocks = SEQ // tm

    def cls_map(i):
        return (jnp.where(i >= lat_blocks, 1, 0), 0, 0)

    row = pl.BlockSpec((tm, d), lambda i: (i, 0))
    vec = pl.BlockSpec((None, 1, d), cls_map)
    args, in_specs = [x], [row]
    if resid is not None:
        o, gate = resid
        args += [o, gate.reshape(2, 1, d)]
        in_specs += [row, vec]
    args.append(gain.reshape(1, d))
    in_specs.append(pl.BlockSpec((1, d), lambda i: (0, 0)))
    if mod is not None:
        shift, scale = mod
        args += [shift.reshape(2, 1, d), scale.reshape(2, 1, d)]
        in_specs += [vec, vec]
    if router is not None:
        rw, rb = router
        args += [rw.T, rb.reshape(N_EXPERTS, 1)]
        in_specs += [pl.BlockSpec((N_EXPERTS, d), lambda i: (0, 0)),
                     pl.BlockSpec((N_EXPERTS, 1), lambda i: (0, 0))]
    out_shape, out_specs = [], []
    if resid is not None:
        out_shape.append(jax.ShapeDtypeStruct((n_rows, d), F32))
        out_specs.append(row)
    out_shape.append(jax.ShapeDtypeStruct((n_rows, d), out_dtype))
    out_specs.append(row)
    if router is not None:
        out_shape += [jax.ShapeDtypeStruct((n_rows, d // 2), jnp.uint32),
                      jax.ShapeDtypeStruct((MOE_TOP_K, n_rows), jnp.int32),
                      jax.ShapeDtypeStruct((MOE_TOP_K, n_rows), F32),
                      jax.ShapeDtypeStruct((MOE_TOP_K, n_rows), jnp.int32),
                      jax.ShapeDtypeStruct((N_EXPERTS, 1), jnp.int32)]
        topk = pl.BlockSpec((MOE_TOP_K, tm), lambda i: (0, i))
        out_specs += [pl.BlockSpec((tm, d // 2), lambda i: (i, 0)), topk, topk, topk,
                      pl.BlockSpec((N_EXPERTS, 1), lambda i: (0, 0))]
    return pl.pallas_call(
        functools.partial(_norm_kernel, has_resid=resid is not None, modulated=mod is not None,
                          with_router=router is not None, out_dtype=out_dtype),
        grid=(nblk,),
        in_specs=in_specs,
        out_specs=out_specs,
        out_shape=out_shape,
        scratch_shapes=[pltpu.VMEM((N_EXPERTS, 1), F32)] if router is not None else [],
        compiler_params=_cparams(("arbitrary",), 16 * tm * d * 4),
        name="norm_router" if router is not None else "norm",
    )(*args)


def _mm_kernel(a_ref, w_ref, o_ref, wb_ref):
    @pl.when(pl.program_id(1) == 0)
    def _():
        wb_ref[...] = w_ref[...].astype(BF16)

    o_ref[...] = _dot(a_ref[...], wb_ref[...]).astype(o_ref.dtype)


def _matmul(a, w, m_rows, out_dtype, tm, tn):
    k, n = w.shape
    assert m_rows % tm == 0 and n % tn == 0 and a.shape[1] == k
    osz = jnp.dtype(out_dtype).itemsize
    vmem = 2 * k * tn * 4 + k * tn * 2 + 2 * tm * k * 2 + 2 * tm * tn * osz + tm * tn * 4
    return pl.pallas_call(
        _mm_kernel,
        grid=(n // tn, m_rows // tm),
        in_specs=[pl.BlockSpec((tm, k), lambda j, i: (i, 0)),
                  pl.BlockSpec((k, tn), lambda j, i: (0, j))],
        out_specs=pl.BlockSpec((tm, tn), lambda j, i: (i, j)),
        out_shape=jax.ShapeDtypeStruct((m_rows, n), out_dtype),
        scratch_shapes=[pltpu.VMEM((k, tn), BF16)],
        compiler_params=_cparams(("arbitrary", "arbitrary"), vmem + (4 << 20)),
        name="matmul",
    )(a, w)


def _pair_swap_matrix(w):
    r = lax.broadcasted_iota(jnp.int32, (w, w), 0)
    c = lax.broadcasted_iota(jnp.int32, (w, w), 1)
    return jnp.where(r == jnp.bitwise_xor(c, 1), 1.0, 0.0).astype(BF16)


def _rope(x, cos_e, sin_s, swap):
    swapped = _dot(x.astype(BF16), swap)
    return x * cos_e + swapped * sin_s


def _rms(x, g):
    return x * lax.rsqrt(jnp.mean(x * x, axis=-1, keepdims=True) + EPS) * g


def _mla_prep1_kernel(lat_ref, qg_ref, kvg_ref, cos_ref, sin_ref, nq_ref, nkv_ref, kr_ref):
    lat = lat_ref[...]
    nq_ref[...] = _rms(lat[:, :MLA_Q_RANK], qg_ref[...]).astype(BF16)
    nkv_ref[...] = _rms(lat[:, MLA_Q_RANK:MLA_Q_RANK + MLA_KV_RANK], kvg_ref[...]).astype(BF16)
    kr = lat[:, MLA_Q_RANK + MLA_KV_RANK:]
    kr_ref[...] = _rope(kr, cos_ref[...], sin_ref[...], _pair_swap_matrix(kr.shape[1])).astype(BF16)


def _mla_prep1(lat, q_norm, kv_norm, cos_t, sin_t):
    n, wdt = lat.shape
    tm = ROW_TILE
    row = lambda w: pl.BlockSpec((tm, w), lambda i: (i, 0))
    one = lambda w: pl.BlockSpec((1, w), lambda i: (0, 0))
    return pl.pallas_call(
        _mla_prep1_kernel,
        grid=(n // tm,),
        in_specs=[row(wdt), one(MLA_Q_RANK), one(MLA_KV_RANK), row(128), row(128)],
        out_specs=[row(MLA_Q_RANK), row(MLA_KV_RANK), row(128)],
        out_shape=[jax.ShapeDtypeStruct((n, MLA_Q_RANK), BF16),
                   jax.ShapeDtypeStruct((n, MLA_KV_RANK), BF16),
                   jax.ShapeDtypeStruct((n, 128), BF16)],
        compiler_params=_cparams(("arbitrary",), 8 << 20),
        name="mla_prep1",
    )(lat, q_norm.reshape(1, -1), kv_norm.reshape(1, -1), cos_t, sin_t)


def _mla_prep2_kernel(q_ref, kn_ref, kr_ref, cos_ref, sin_ref, qo_ref, ko_ref, *, scale):
    cos_e = cos_ref[...]
    sin_s = sin_ref[...]
    kr = kr_ref[...]
    swap = _pair_swap_matrix(MLA_QK_PAD - MLA_NOPE)
    for h in range(MLA_HEADS):
        lo = h * MLA_QK_PAD
        qo_ref[:, lo:lo + MLA_NOPE] = (q_ref[:, lo:lo + MLA_NOPE].astype(F32) * scale).astype(BF16)
        qr = q_ref[:, lo + MLA_NOPE:lo + MLA_QK_PAD].astype(F32)
        qo_ref[:, lo + MLA_NOPE:lo + MLA_QK_PAD] = (_rope(qr, cos_e, sin_s, swap) * scale).astype(BF16)
        ko_ref[:, lo:lo + MLA_NOPE] = kn_ref[:, h * MLA_NOPE:(h + 1) * MLA_NOPE]
        ko_ref[:, lo + MLA_NOPE:lo + MLA_QK_PAD] = kr


def _mla_prep2(q_raw, kv_all, kr, cos_t, sin_t):
    n = q_raw.shape[0]
    tm = ROW_TILE
    wq = MLA_HEADS * MLA_QK_PAD
    row = lambda w: pl.BlockSpec((tm, w), lambda i: (i, 0))
    return pl.pallas_call(
        functools.partial(_mla_prep2_kernel, scale=(MLA_NOPE + MLA_ROPE) ** -0.5 * LOG2E),
        grid=(n // tm,),
        in_specs=[row(wq), row(MLA_HEADS * MLA_NOPE), row(128), row(128), row(128)],
        out_specs=[row(wq), row(wq)],
        out_shape=[jax.ShapeDtypeStruct((n, wq), BF16), jax.ShapeDtypeStruct((n, wq), BF16)],
        compiler_params=_cparams(("arbitrary",), 16 << 20),
        name="mla_prep2",
    )(q_raw, kv_all, kr, cos_t, sin_t)


def _gqa_prep_kernel(qkv_ref, qg_ref, kg_ref, cos_ref, sin_ref, qo_ref, ko_ref, *, scale):
    cos_e = cos_ref[...]
    sin_s = sin_ref[...]
    swap = _pair_swap_matrix(HEAD_DIM)
    for h in range(GQA_HEADS):
        xh = qkv_ref[:, h * HEAD_DIM:(h + 1) * HEAD_DIM].astype(F32)
        qo_ref[:, h * HEAD_DIM:(h + 1) * HEAD_DIM] = (
            _rope(_rms(xh, qg_ref[...]), cos_e, sin_s, swap) * scale).astype(BF16)
    for h in range(GQA_KV):
        lo = (GQA_HEADS + h) * HEAD_DIM
        xh = qkv_ref[:, lo:lo + HEAD_DIM].astype(F32)
        ko_ref[:, h * HEAD_DIM:(h + 1) * HEAD_DIM] = _rope(_rms(xh, kg_ref[...]), cos_e, sin_s, swap).astype(BF16)


def _gqa_prep(qkv, q_norm, k_norm, cos_t, sin_t):
    n = qkv.shape[0]
    tm = ROW_TILE
    wqk = (GQA_HEADS + GQA_KV) * HEAD_DIM
    row = lambda w: pl.BlockSpec((tm, w), lambda i: (i, 0))
    one = pl.BlockSpec((1, HEAD_DIM), lambda i: (0, 0))
    return pl.pallas_call(
        functools.partial(_gqa_prep_kernel, scale=HEAD_DIM ** -0.5 * LOG2E),
        grid=(n // tm,),
        in_specs=[row(wqk), one, one, row(HEAD_DIM), row(HEAD_DIM)],
        out_specs=[row(GQA_HEADS * HEAD_DIM), row(GQA_KV * HEAD_DIM)],
        out_shape=[jax.ShapeDtypeStruct((n, GQA_HEADS * HEAD_DIM), BF16),
                   jax.ShapeDtypeStruct((n, GQA_KV * HEAD_DIM), BF16)],
        compiler_params=_cparams(("arbitrary",), 16 << 20),
        name="gqa_prep",
    )(qkv, q_norm.reshape(1, -1), k_norm.reshape(1, -1), cos_t, sin_t)


def _flash_kernel(q_ref, k_ref, v_ref, o_ref, sa_ref, sb_ref, *, group, dk, dv, tk, n_lat, q_scale):
    tq = q_ref.shape[0]
    q = jnp.concatenate([q_ref[:, g * dk:(g + 1) * dk] for g in range(group)], axis=0) if group > 1 else q_ref[...]
    if q_scale is not None:
        q = (q.astype(F32) * q_scale).astype(BF16)

    s = _dot_nt(q, k_ref[SEQ:SEQ + CTX_LEN, :])
    m = jnp.max(s, axis=-1, keepdims=True)
    p = jnp.exp2(s - m)
    l = jnp.sum(p, axis=-1, keepdims=True)
    acc = _dot(p.astype(BF16), v_ref[SEQ:SEQ + CTX_LEN, :])

    if n_lat:
        def rows(ref, j):
            return ref[pl.ds(pl.multiple_of(j * tk, tk), tk), :]

        def update(s, vb, m, l, acc):
            m_new = jnp.maximum(m, jnp.max(s, axis=-1, keepdims=True))
            alpha = jnp.exp2(m - m_new)
            p = jnp.exp2(s - m_new)
            l = alpha * l + jnp.sum(p, axis=-1, keepdims=True)
            return m_new, l, alpha * acc + _dot(p.astype(BF16), vb)

        def pair(jj, carry, last):
            j0 = 2 * jj
            sb_ref[...] = _dot_nt(q, rows(k_ref, j0 + 1))
            carry = update(sa_ref[...], rows(v_ref, j0), *carry)
            if not last:
                sa_ref[...] = _dot_nt(q, rows(k_ref, j0 + 2))
            return update(sb_ref[...], rows(v_ref, j0 + 1), *carry)

        sa_ref[...] = _dot_nt(q, rows(k_ref, 0))
        carry = lax.fori_loop(0, n_lat // 2 - 1, lambda jj, c: pair(jj, c, False), (m, l, acc))
        m, l, acc = pair(n_lat // 2 - 1, carry, True)

    o = (acc / l).astype(o_ref.dtype)
    for g in range(group):
        o_ref[:, g * dv:(g + 1) * dv] = o[g * tq:(g + 1) * tq]


def _flash(q_arr, k_arr, v_arr, *, n_kv_heads, group, dk, dv, q_col0, k_col0, v_col0, ctx_queries, tq, tk,
           q_scale=None):
    n_tok = k_arr.shape[0]
    if ctx_queries:
        n_q, q_blk0, n_lat = CTX_LEN, SEQ // tq, 0
    else:
        n_q, q_blk0, n_lat = SEQ, 0, SEQ // tk
        assert n_lat % 2 == 0
    m = group * tq
    s_shape = (m, tk) if n_lat else (8, 128)
    kv_mode = pl.Buffered(1) if m * tk * 4 >= (8 << 20) else None
    vmem = (2 * n_tok * (dk + dv) * 2 + 4 * tq * group * (dk + dv) * 2
            + 5 * m * max(tk, CTX_LEN) * 4 + 4 * m * dv * 4)
    return pl.pallas_call(
        functools.partial(_flash_kernel, group=group, dk=dk, dv=dv, tk=tk, n_lat=n_lat, q_scale=q_scale),
        grid=(n_kv_heads, n_q // tq),
        in_specs=[pl.BlockSpec((tq, group * dk), lambda h, i: (q_blk0 + i, q_col0 + h)),
                  pl.BlockSpec((n_tok, dk), lambda h, i: (0, k_col0 + h), pipeline_mode=kv_mode),
                  pl.BlockSpec((n_tok, dv), lambda h, i: (0, v_col0 + h), pipeline_mode=kv_mode)],
        out_specs=pl.BlockSpec((tq, group * dv), lambda h, i: (i, h)),
        out_shape=jax.ShapeDtypeStruct((n_q, n_kv_heads * group * dv), BF16),
        scratch_shapes=[pltpu.VMEM(s_shape, F32), pltpu.VMEM(s_shape, F32)],
        compiler_params=_cparams(("arbitrary", "arbitrary"), vmem + (8 << 20)),
        name="flash_ctx" if ctx_queries else "flash",
    )(q_arr, k_arr, v_arr)


def _na_start_row(b):
    return jnp.clip(NA_RB * b - NA_KH // 2, 0, GRID_H - NA_U)


def _na_kernel(q_ref, k_ref, v_ref, bias_ref, o_ref, *, scale):
    b = pl.program_id(1)
    start = pl.multiple_of(_na_start_row(b) * GRID_W, GRID_W)
    nwin = NA_U * GRID_W
    q = (q_ref[...].astype(F32) * scale).astype(BF16)
    s_loc = _dot_nt(q, k_ref[pl.ds(start, nwin), :]) + bias_ref[...]
    s_ctx = _dot_nt(q, k_ref[SEQ:SEQ + CTX_LEN, :])
    m = jnp.maximum(jnp.max(s_loc, axis=-1, keepdims=True), jnp.max(s_ctx, axis=-1, keepdims=True))
    p_loc = jnp.exp(s_loc - m)
    p_ctx = jnp.exp(s_ctx - m)
    l = jnp.sum(p_loc, axis=-1, keepdims=True) + jnp.sum(p_ctx, axis=-1, keepdims=True)
    o = _dot(p_loc.astype(BF16), v_ref[pl.ds(start, nwin), :]) + _dot(p_ctx.astype(BF16), v_ref[SEQ:SEQ + CTX_LEN, :])
    o_ref[...] = (o / l).astype(o_ref.dtype)


def _na_bias_tables(rpb):
    h = rpb.shape[0]
    w = np.arange(GRID_W)
    c0 = np.clip(w - NA_KW // 2, 0, GRID_W - NA_KW)
    colmask = (w[None, :] >= c0[:, None]) & (w[None, :] < c0[:, None] + NA_KW)
    colidx = np.clip(w[None, :] - w[:, None] + NA_KW - 1, 0, 2 * NA_KW - 2)
    slabs = jnp.where(colmask[None, None], rpb[:, :, colidx], NEG)
    slabs = jnp.concatenate([slabs, jnp.full((h, 1, GRID_W, GRID_W), NEG, rpb.dtype)], axis=1)
    masked = 2 * NA_KH - 1
    nb = GRID_H // NA_RB
    tabs = []
    for b in (0, 1, nb - 1):
        start = int(np.clip(NA_RB * b - NA_KH // 2, 0, GRID_H - NA_U))
        a = np.full((NA_RB, NA_U), masked, np.int32)
        for j in range(NA_RB):
            r = NA_RB * b + j
            r0 = int(np.clip(r - NA_KH // 2, 0, GRID_H - NA_KH))
            for u in range(NA_U):
                kr = start + u
                if r0 <= kr < r0 + NA_KH:
                    a[j, u] = kr - r + NA_KH - 1
        t = slabs[:, a]
        tabs.append(t.transpose(0, 1, 3, 2, 4).reshape(h, NA_RB * GRID_W, NA_U * GRID_W))
    return jnp.stack(tabs, axis=1)


def _na_attention(qkv, bias_tabs):
    n_tok = qkv.shape[0]
    nb = GRID_H // NA_RB
    tq = NA_RB * GRID_W
    nwin = NA_U * GRID_W

    def cls(b):
        return jnp.where(b == 0, 0, jnp.where(b == nb - 1, 2, 1))

    vmem = 4 * n_tok * HEAD_DIM * 2 + 2 * tq * nwin * 4 + 4 * tq * (nwin + CTX_LEN) * 4
    return pl.pallas_call(
        functools.partial(_na_kernel, scale=HEAD_DIM ** -0.5),
        grid=(NA_HEADS, nb),
        in_specs=[pl.BlockSpec((tq, HEAD_DIM), lambda h, b: (b, h)),
                  pl.BlockSpec((n_tok, HEAD_DIM), lambda h, b: (0, NA_HEADS + h)),
                  pl.BlockSpec((n_tok, HEAD_DIM), lambda h, b: (0, 2 * NA_HEADS + h)),
                  pl.BlockSpec((None, None, tq, nwin), lambda h, b: (h, cls(b), 0, 0))],
        out_specs=pl.BlockSpec((tq, HEAD_DIM), lambda h, b: (b, h)),
        out_shape=jax.ShapeDtypeStruct((SEQ, NA_HEADS * HEAD_DIM), BF16),
        compiler_params=_cparams(("arbitrary", "arbitrary"), vmem + (8 << 20)),
        name="na_attention",
    )(qkv, qkv, qkv, bias_tabs)


def _dispatch_kernel(tail_ref, nu_ref, slot_ref, h_ref, xg_hbm, zbuf, sem, zsem, *, tb, n_blocks):
    te = EXPERT_TILE

    @pl.when(pl.program_id(0) == 0)
    def _():
        zbuf[...] = jnp.zeros_like(zbuf)

        def zero_tile(row0, wait):
            cp = pltpu.make_async_copy(zbuf, xg_hbm.at[pl.ds(pl.multiple_of(row0, te), te)], zsem)
            if wait:
                cp.wait()
            else:
                cp.start()

        def zero_tail(e, wait):
            @pl.when(tail_ref[e] >= 0)
            def _():
                zero_tile(tail_ref[e], wait)

        for wait in (False, True):
            lax.fori_loop(0, N_EXPERTS, lambda e, c: (zero_tail(e, wait), c)[1], 0)
            lax.fori_loop(nu_ref[0], n_blocks, lambda b, c: (zero_tile(b * te, wait), c)[1], 0)

    def body(t, carry):
        for k in range(MOE_TOP_K):
            pltpu.make_async_copy(h_ref.at[pl.ds(t, 1)], xg_hbm.at[pl.ds(slot_ref[k, t], 1)], sem).start(
                priority=k % 2)
        return carry

    lax.fori_loop(0, tb, body, 0, unroll=DMA_LOOP_UNROLL)
    for k in range(MOE_TOP_K):
        pltpu.make_async_copy(h_ref, xg_hbm.at[pl.ds(0, tb)], sem).wait()


def _dispatch(h32, slot_t, tail_start, n_used, n_rows, n_slots):
    d = h32.shape[1]
    tb = ROW_TILE
    grid_spec = pltpu.PrefetchScalarGridSpec(
        num_scalar_prefetch=2,
        grid=(n_rows // tb,),
        in_specs=[pl.BlockSpec((MOE_TOP_K, tb), lambda i, tl, nu: (0, i), memory_space=pltpu.SMEM),
                  pl.BlockSpec((tb, d), lambda i, tl, nu: (i, 0))],
        out_specs=pl.BlockSpec(memory_space=pl.ANY),
        scratch_shapes=[pltpu.VMEM((EXPERT_TILE, d), h32.dtype), pltpu.SemaphoreType.DMA(()),
                        pltpu.SemaphoreType.DMA(())],
    )
    return pl.pallas_call(
        functools.partial(_dispatch_kernel, tb=tb, n_blocks=n_slots // EXPERT_TILE),
        grid_spec=grid_spec,
        out_shape=jax.ShapeDtypeStruct((n_slots, d), h32.dtype),
        compiler_params=_cparams(("arbitrary",), 16 << 20),
        name="moe_dispatch",
    )(tail_start, n_used, slot_t, h32)


FFN_LEAD = 2


def _ffn_kernel(be_ref, ord_ref, nu_ref, x_ref, wg_ref, wu_ref, wd_ref, o_ref, wgb, wub, wdb, *, packed, nb):
    b = pl.program_id(0)
    c = b - FFN_LEAD

    def at(ref, i):
        return ref[jnp.clip(i, 0, nb - 1)]

    def starts_expert(i):
        return jnp.logical_or(b == 0, at(be_ref, i) != at(be_ref, i - 1))

    @pl.when(starts_expert(c))
    def _():
        wdb[at(ord_ref, c) % 2] = wd_ref[...].astype(BF16)

    @pl.when(jnp.logical_and(c >= 0, c < nu_ref[0]))
    def _():
        slot = at(ord_ref, c) % 2
        x = _unpack_bf16_pairs(x_ref[...]).astype(BF16) if packed else x_ref[...]
        g = _dot(x, wgb[slot])
        u = _dot(x, wub[slot])
        a = (g * jax.nn.sigmoid(g) * u).astype(BF16)
        y = _dot(a, wdb[slot])
        o_ref[...] = _pack_bf16_pairs(y) if packed else y.astype(o_ref.dtype)

    @pl.when(c >= nu_ref[0])
    def _():
        o_ref[...] = jnp.zeros_like(o_ref)

    @pl.when(starts_expert(b))
    def _():
        wgb[at(ord_ref, b) % 2] = wg_ref[...].astype(BF16)

    @pl.when(starts_expert(b - 1))
    def _():
        wub[at(ord_ref, b - 1) % 2] = wu_ref[...].astype(BF16)


def _expert_ffn(x_rows, block_expert, n_used, wg, wu, wd, layer, tb, packed):
    p, dx = x_rows.shape
    d, de = wg.shape[-2:]
    nb = p // tb
    out_dtype = x_rows.dtype
    vmem = 2 * 3 * d * de * 4 + 2 * 3 * d * de * 2 + 4 * tb * d * 2 + 4 * tb * de * 4 + 3 * tb * d * 4
    idx = jnp.arange(nb, dtype=jnp.int32)
    be = jnp.where(idx < n_used[0], block_expert, block_expert[jnp.maximum(n_used[0] - 1, 0)])
    ordinal = jnp.cumsum(jnp.concatenate([jnp.zeros((1,), jnp.int32),
                                          (be[1:] != be[:-1]).astype(jnp.int32)])).astype(jnp.int32)

    def blk(i):
        return jnp.clip(i, 0, nb - 1)

    def wspec(shape, lag):
        return pl.BlockSpec((None, None) + shape, lambda b, be, od, nu: (layer, be[blk(b - lag)], 0, 0))

    grid_spec = pltpu.PrefetchScalarGridSpec(
        num_scalar_prefetch=3,
        grid=(nb + FFN_LEAD,),
        in_specs=[pl.BlockSpec((tb, dx), lambda b, be, od, nu: (jnp.minimum(blk(b - FFN_LEAD), nu[0] - 1), 0)),
                  wspec((d, de), 0), wspec((d, de), 1), wspec((de, d), 2)],
        out_specs=pl.BlockSpec((tb, dx), lambda b, be, od, nu: (blk(b - FFN_LEAD), 0)),
        scratch_shapes=[pltpu.VMEM((2, d, de), BF16), pltpu.VMEM((2, d, de), BF16), pltpu.VMEM((2, de, d), BF16)],
    )
    return pl.pallas_call(
        functools.partial(_ffn_kernel, packed=packed, nb=nb),
        grid_spec=grid_spec,
        out_shape=jax.ShapeDtypeStruct((p, dx), out_dtype),
        compiler_params=_cparams(("arbitrary",), vmem + (4 << 20)),
        name="expert_ffn",
    )(be, ordinal, n_used, x_rows, wg, wu, wd)


def _combine_kernel(*refs, tc, nblk, modulated, emit_x):
    it = iter(refs)
    s_cur, s_nxt, y_hbm, ew_ref, ysh_ref, x_ref, gate_ref, g_ref = [next(it) for _ in range(8)]
    if modulated:
        shift_ref, scale_ref = next(it), next(it)
    if emit_x:
        o_ref = next(it)
    h_ref, buf, sem = next(it), next(it), next(it)
    i = pl.program_id(0)

    def gather(slot_ref, bslot):
        def body(t, carry):
            for k in range(MOE_TOP_K):
                pltpu.make_async_copy(y_hbm.at[pl.ds(slot_ref[k, t], 1)],
                                      buf.at[bslot, k, pl.ds(t, 1)], sem.at[bslot]).start()
            return carry

        lax.fori_loop(0, tc, body, 0, unroll=DMA_LOOP_UNROLL)

    @pl.when(i == 0)
    def _():
        gather(s_cur, 0)

    cur = i % 2

    @pl.when(i + 1 < nblk)
    def _():
        @pl.when(cur == 0)
        def _():
            gather(s_nxt, 1)

        @pl.when(cur == 1)
        def _():
            gather(s_nxt, 0)

    def finish(bslot):
        for k in range(MOE_TOP_K):
            pltpu.make_async_copy(y_hbm.at[pl.ds(0, tc)], buf.at[bslot, k], sem.at[bslot]).wait()
        ew = ew_ref[...]
        acc = ysh_ref[...].astype(F32)
        for k in range(MOE_TOP_K):
            acc = acc + ew[:, k:k + 1] * _unpack_bf16_pairs(buf[bslot, k])
        xn = x_ref[...] + gate_ref[...] * acc
        if emit_x:
            o_ref[...] = xn
        hn = _rms(xn, g_ref[...])
        if modulated:
            hn = hn * (1.0 + scale_ref[...]) + shift_ref[...]
        h_ref[...] = hn.astype(h_ref.dtype)

    @pl.when(cur == 0)
    def _():
        finish(0)

    @pl.when(cur == 1)
    def _():
        finish(1)


def _combine(y, slot_t, ew, ysh, x, gate, n_rows, gain, mod):
    d = x.shape[1]
    tc = COMBINE_TILE
    nblk = n_rows // tc
    lat_blocks = SEQ // tc
    row = pl.BlockSpec((tc, d), lambda i: (i, 0))
    vec = pl.BlockSpec((None, 1, d), lambda i: (jnp.where(i >= lat_blocks, 1, 0), 0, 0))
    vmem = 2 * MOE_TOP_K * tc * d * 2 + 12 * tc * d * 4
    args = [slot_t, slot_t, y, ew, ysh, x, gate.reshape(2, 1, d), gain.reshape(1, d)]
    in_specs = [pl.BlockSpec((MOE_TOP_K, tc), lambda i: (0, i), memory_space=pltpu.SMEM),
                pl.BlockSpec((MOE_TOP_K, tc), lambda i: (0, jnp.minimum(i + 1, nblk - 1)),
                             memory_space=pltpu.SMEM),
                pl.BlockSpec(memory_space=pl.ANY),
                pl.BlockSpec((tc, MOE_TOP_K), lambda i: (i, 0)),
                row, row, vec, pl.BlockSpec((1, d), lambda i: (0, 0))]
    if mod is not None:
        args += [mod[0].reshape(2, 1, d), mod[1].reshape(2, 1, d)]
        in_specs += [vec, vec]
        out_shape = [jax.ShapeDtypeStruct((n_rows, d), F32), jax.ShapeDtypeStruct((n_rows, d), BF16)]
    else:
        out_shape = [jax.ShapeDtypeStruct((n_rows, d), F32)]
    return pl.pallas_call(
        functools.partial(_combine_kernel, tc=tc, nblk=nblk, modulated=mod is not None, emit_x=mod is not None),
        grid=(nblk,),
        in_specs=in_specs,
        out_specs=[row] * len(out_shape),
        out_shape=out_shape,
        scratch_shapes=[pltpu.VMEM((2, MOE_TOP_K, tc, d // 2), jnp.uint32), pltpu.SemaphoreType.DMA((2,))],
        compiler_params=_cparams(("arbitrary",), vmem + (4 << 20)),
        name="moe_combine",
    )(*args)


def _routing_tables(counts, n_rows):
    tb = EXPERT_TILE
    n_blocks = -(-(n_rows * MOE_TOP_K) // tb) + N_EXPERTS
    padded = (counts + tb - 1) // tb * tb
    pad_end = jnp.cumsum(padded)
    pad_start = (pad_end - padded).astype(jnp.int32)
    tail_start = jnp.where(padded > 0, pad_end - tb, -1).astype(jnp.int32)
    blk_start = jnp.arange(n_blocks, dtype=jnp.int32) * tb
    block_expert = jnp.sum(blk_start[:, None] >= pad_end[None, :], axis=1)
    block_expert = jnp.minimum(block_expert, N_EXPERTS - 1).astype(jnp.int32)
    n_used = (pad_end[-1] // tb).astype(jnp.int32).reshape(1)
    return pad_start, tail_start, block_expert, n_used, n_blocks * tb


def _moe(layer, x_new, h, h32, eidx_t, ew_t, rank_t, counts, gate, n_rows, wg, wu, wd, sg, su, sd, next_gain, next_mod):
    pad_start, tail_start, block_expert, n_used, n_slots = _routing_tables(counts[:, 0], n_rows)
    e_iota = jnp.arange(N_EXPERTS, dtype=jnp.int32)
    slot_t = jnp.sum(jnp.where(eidx_t[:, :, None] == e_iota, pad_start, 0), axis=-1) + rank_t
    xg = _dispatch(h32, slot_t, tail_start, n_used, n_rows, n_slots)
    y = _expert_ffn(xg, block_expert, n_used, wg, wu, wd, layer, EXPERT_TILE, True)
    tm = MM_TM if n_rows % MM_TM == 0 else 512
    ysh = _expert_ffn(h, jnp.zeros((n_rows // tm,), jnp.int32), jnp.full((1,), n_rows // tm, jnp.int32),
                      sg[:, None], su[:, None], sd[:, None], layer, tm, False)
    return _combine(y, slot_t, ew_t.T, ysh, x_new, gate, n_rows, next_gain, next_mod)


def _rope_tables(d, width):
    t = jnp.arange(SEQ)
    row = (t // GRID_W).astype(F32)
    col = (t % GRID_W).astype(F32)
    n = d // 4
    inv = ROPE_THETA ** (-jnp.arange(n, dtype=F32) / n)
    ang = jnp.concatenate([row[:, None] * inv, col[:, None] * inv], axis=-1)
    ang = jnp.concatenate([ang, jnp.zeros((CTX_LEN, d // 2), F32)], axis=0)
    cos = jnp.repeat(jnp.cos(ang), 2, axis=1)
    sin = jnp.repeat(jnp.sin(ang), 2, axis=1) * jnp.tile(jnp.array([-1.0, 1.0], F32), d // 2)
    if width > d:
        cos = jnp.concatenate([cos, jnp.ones((N_TOK, width - d), F32)], axis=1)
        sin = jnp.concatenate([sin, jnp.zeros((N_TOK, width - d), F32)], axis=1)
    return cos, sin


def _mixer_even(h, w_in, rpb, q_norm, w_uq, kv_norm, w_ukv, w_out):
    n_na = 3 * NA_HEADS * HEAD_DIM
    qkv = _matmul(h, w_in[:, :n_na], N_TOK, BF16, MM_TM, 1024)
    w_tail = jnp.pad(w_in[:, n_na:], ((0, 0), (0, 64)))
    lat = _matmul(h, w_tail, N_TOK, F32, MM_TM, 384)
    cos_t, sin_t = _rope_tables(MLA_ROPE, 128)
    nq, nkv, kr = _mla_prep1(lat, q_norm, kv_norm, cos_t, sin_t)
    wq = w_uq.reshape(MLA_Q_RANK, MLA_HEADS, MLA_NOPE + MLA_ROPE)
    wq = jnp.pad(wq, ((0, 0), (0, 0), (0, MLA_QK_PAD - MLA_NOPE - MLA_ROPE))).reshape(MLA_Q_RANK, -1)
    wkv = w_ukv.reshape(MLA_KV_RANK, MLA_HEADS, MLA_NOPE + MLA_V)
    wkv = jnp.concatenate([wkv[:, :, :MLA_NOPE].reshape(MLA_KV_RANK, -1),
                           wkv[:, :, MLA_NOPE:].reshape(MLA_KV_RANK, -1)], axis=1)
    q_raw = _matmul(nq, wq, N_TOK, F32, MM_TM, 1024)
    kv_all = _matmul(nkv, wkv, N_TOK, BF16, MM_TM, 1024)
    q_mla, k_cat = _mla_prep2(q_raw, kv_all, kr, cos_t, sin_t)

    o_na = _na_attention(qkv, _na_bias_tables(rpb))
    mla = dict(n_kv_heads=MLA_HEADS, group=1, dk=MLA_QK_PAD, dv=MLA_V, q_col0=0, k_col0=0, v_col0=MLA_HEADS)
    o_mla = _flash(q_mla, k_cat, kv_all, ctx_queries=False, tq=1024, tk=1024, **mla)
    o_mla_c = _flash(q_mla, k_cat, kv_all, ctx_queries=True, tq=256, tk=512, **mla)
    na = dict(n_kv_heads=NA_HEADS, group=1, dk=HEAD_DIM, dv=HEAD_DIM, q_col0=0, k_col0=NA_HEADS, v_col0=2 * NA_HEADS)
    o_na_c = _flash(qkv, qkv, qkv, ctx_queries=True, tq=256, tk=512, q_scale=HEAD_DIM ** -0.5 * LOG2E, **na)
    o = jnp.concatenate([jnp.concatenate([o_na, o_mla], axis=1),
                         jnp.concatenate([o_na_c, o_mla_c], axis=1)], axis=0)
    return _matmul(o, w_out, N_TOK, F32, MM_TM, 1024)


def _mixer_odd_last(h, w_qkv, q_norm, k_norm, w_o):
    qkv = _matmul(h, w_qkv, N_TOK, BF16, MM_TM, 1024)
    cos_t, sin_t = _rope_tables(HEAD_DIM, HEAD_DIM)
    q_g, k_g = _gqa_prep(qkv, q_norm, k_norm, cos_t, sin_t)
    o = _flash(q_g, k_g, qkv, n_kv_heads=GQA_KV, group=GQA_GROUP, dk=HEAD_DIM, dv=HEAD_DIM,
               q_col0=0, k_col0=0, v_col0=GQA_HEADS + GQA_KV, ctx_queries=False, tq=512, tk=1024)
    return _matmul(o, w_o, SEQ, F32, 512, 1024)


def kernel(x, c, ctx, c_ctx, ada_w, ada_b, mix_norm, ffn_norm, ev_w_in, ev_rpb, ev_q_norm, ev_w_uq, ev_kv_norm, ev_w_ukv, ev_w_out, od_w_qkv, od_q_norm, od_k_norm, od_w_o, router_w, router_b, exp_w_gate, exp_w_up, exp_w_down, sh_w_gate, sh_w_up, sh_w_down, final_norm):
    assert DEPTH == 2 and x.shape == (1, SEQ, D_MODEL)
    mods = _adaln_mods(c, c_ctx, ada_w, ada_b)
    xt = jnp.concatenate([x[0], ctx[0]], axis=0)

    def parts(i):
        return [mods[i, :, k * D_MODEL:(k + 1) * D_MODEL] for k in range(6)]

    sh1, sc1, g1, sh2, sc2, g2 = parts(0)
    (h,) = _norm_call(xt, N_TOK, mix_norm[0], mod=(sh1, sc1))
    o = _mixer_even(h, ev_w_in[0], ev_rpb[0], ev_q_norm[0], ev_w_uq[0], ev_kv_norm[0], ev_w_ukv[0], ev_w_out[0])
    xt, h, h32, eidx_t, ew_t, rank_t, counts = _norm_call(xt, N_TOK, ffn_norm[0], resid=(o, g1), mod=(sh2, sc2),
                                                          router=(router_w[0], router_b[0]))
    n1_sh1, n1_sc1, n1_g1, n1_sh2, n1_sc2, n1_g2 = parts(1)
    xt, h = _moe(0, xt, h, h32, eidx_t, ew_t, rank_t, counts, g2, N_TOK, exp_w_gate, exp_w_up, exp_w_down,
                 sh_w_gate, sh_w_up, sh_w_down, mix_norm[1], (n1_sh1, n1_sc1))

    o = _mixer_odd_last(h, od_w_qkv[0], od_q_norm[0], od_k_norm[0], od_w_o[0])
    xl, h, h32, eidx_t, ew_t, rank_t, counts = _norm_call(xt, SEQ, ffn_norm[1], resid=(o, n1_g1),
                                                          mod=(n1_sh2, n1_sc2),
                                                          router=(router_w[1], router_b[1]))
    (out,) = _moe(1, xl, h, h32, eidx_t, ew_t, rank_t, counts, n1_g2, SEQ, exp_w_gate, exp_w_up, exp_w_down,
                  sh_w_gate, sh_w_up, sh_w_down, final_norm, None)
    return out[None]
```

```python
import functools

import numpy as np
import jax
import jax.numpy as jnp
from jax import lax
from jax.experimental import pallas as pl
from jax.experimental.pallas import tpu as pltpu

F32 = jnp.float32
BF16 = jnp.bfloat16

D_MODEL = 2048
SEQ = 8192
DEPTH = 2
GRID_W = 64
GRID_H = SEQ // GRID_W
CTX_LEN = 256
N_TOK = SEQ + CTX_LEN
EPS = 1e-6
ROPE_THETA = 10000.0
HEAD_DIM = 128
NA_HEADS = 8
NA_KH = 8
NA_KW = 16
MLA_HEADS = 8
MLA_Q_RANK = 512
MLA_KV_RANK = 512
MLA_NOPE = 128
MLA_ROPE = 64
MLA_V = 128
MLA_QK_PAD = 256
GQA_HEADS = 16
GQA_KV = 4
GQA_GROUP = GQA_HEADS // GQA_KV
N_EXPERTS = 64
N_GROUPS = 8
GROUP_SIZE = N_EXPERTS // N_GROUPS
TOPK_GROUPS = 4
MOE_TOP_K = 8
D_EXPERT = 512
ROUTED_SCALE = 2.5

NEG = -1e30
LOG2E = 1.4426950408889634

VMEM_LIMIT_CAP = 60 * 1024 * 1024

ROW_TILE = 256
MM_TM = 768
NA_RB = 4
NA_U = NA_RB + NA_KH - 1
EXPERT_TILE = 256
COMBINE_TILE = 128
DMA_LOOP_UNROLL = 128


def _cparams(sem, vmem_bytes):
    return pltpu.CompilerParams(
        dimension_semantics=sem,
        vmem_limit_bytes=int(min(max(vmem_bytes, 16 * 1024 * 1024), VMEM_LIMIT_CAP)),
    )


def _pack_bf16_pairs(x):
    w = x.shape[1] // 2

    def rounded(v):
        b = lax.bitcast_convert_type(v, jnp.uint32)
        return b + (((b >> 16) & jnp.uint32(1)) + jnp.uint32(0x7FFF))

    return (rounded(x[:, :w]) >> 16) | (rounded(x[:, w:]) & jnp.uint32(0xFFFF0000))


def _unpack_bf16_pairs(p):
    lo = lax.bitcast_convert_type(p << 16, F32)
    hi = lax.bitcast_convert_type(p & jnp.uint32(0xFFFF0000), F32)
    return jnp.concatenate([lo, hi], axis=1)


def _dot(a, b):
    return jnp.dot(a, b, preferred_element_type=F32)


def _dot_nt(a, b):
    return lax.dot_general(a, b, (((1,), (1,)), ((), ())), preferred_element_type=F32)


def _mods_kernel(cc_ref, w_ref, b_ref, o_ref, acc_ref, *, nk):
    k = pl.program_id(2)

    @pl.when(k == 0)
    def _():
        acc_ref[...] = jnp.zeros_like(acc_ref)

    a = cc_ref[...]
    a = a * jax.nn.sigmoid(a)
    w = w_ref[...]
    tk, tn = w.shape
    w3 = w.reshape(tk // 8, 8, tn)
    for v in range(2):
        col = a[:, v:v + 1].reshape(tk // 8, 8, 1)
        acc_ref[v] += jnp.sum(w3 * col, axis=0)

    @pl.when(k == nk - 1)
    def _():
        o_ref[...] = jnp.sum(acc_ref[...], axis=1) + b_ref[...]


def _adaln_mods(c, c_ctx, ada_w, ada_b):
    depth, d, n = ada_w.shape
    tk, tn = 1024, 1536
    nk = d // tk
    cc = jnp.stack([c[0], c_ctx], axis=1)
    return pl.pallas_call(
        functools.partial(_mods_kernel, nk=nk),
        grid=(depth, n // tn, nk),
        in_specs=[
            pl.BlockSpec((tk, 2), lambda i, j, k: (k, 0)),
            pl.BlockSpec((None, tk, tn), lambda i, j, k: (i, k, j)),
            pl.BlockSpec((None, 1, tn), lambda i, j, k: (i, 0, j)),
        ],
        out_specs=pl.BlockSpec((None, 2, tn), lambda i, j, k: (i, 0, j)),
        out_shape=jax.ShapeDtypeStruct((depth, 2, n), F32),
        scratch_shapes=[pltpu.VMEM((2, 8, tn), F32)],
        compiler_params=_cparams(("arbitrary", "arbitrary", "arbitrary"), 4 * tk * tn * 4),
        name="adaln_mods",
    )(cc, ada_w, ada_b.reshape(depth, 1, n))


def _router_select(hf, rwt_ref, rb_ref, base_ref):
    logits = lax.dot_general(rwt_ref[...], hf, (((1,), (1,)), ((), ())),
                             precision=lax.Precision.HIGHEST, preferred_element_type=F32)
    scores = jax.nn.sigmoid(logits)
    sel = scores + rb_ref[...]
    tm = sel.shape[1]
    ninf = jnp.float32(-jnp.inf)
    sub = lax.broadcasted_iota(jnp.int32, (GROUP_SIZE, tm), 0)
    gi = lax.broadcasted_iota(jnp.int32, (N_GROUPS, tm), 0)
    gs = jnp.zeros((N_GROUPS, tm), F32)
    for g in range(N_GROUPS):
        blk = sel[g * GROUP_SIZE:(g + 1) * GROUP_SIZE, :]
        m1 = jnp.max(blk, axis=0, keepdims=True)
        first = jnp.min(jnp.where(blk == m1, sub, GROUP_SIZE), axis=0, keepdims=True)
        m2 = jnp.max(jnp.where(sub == first, ninf, blk), axis=0, keepdims=True)
        gs = jnp.where(gi == g, m1 + m2, gs)
    gsel = jnp.zeros((N_GROUPS, tm), F32)
    cur = gs
    for _ in range(TOPK_GROUPS):
        mx = jnp.max(cur, axis=0, keepdims=True)
        idx = jnp.min(jnp.where(cur == mx, gi, N_GROUPS), axis=0, keepdims=True)
        pick = gi == idx
        gsel = jnp.where(pick, 1.0, gsel)
        cur = jnp.where(pick, ninf, cur)
    ei = lax.broadcasted_iota(jnp.int32, (N_EXPERTS, tm), 0)
    eg = ei // GROUP_SIZE
    emask = jnp.zeros((N_EXPERTS, tm), F32)
    for g in range(N_GROUPS):
        emask = jnp.where(eg == g, gsel[g:g + 1, :], emask)
    cur = jnp.where(emask > 0.0, sel, ninf)
    idxs, ws, picks = [], [], []
    for _ in range(MOE_TOP_K):
        mx = jnp.max(cur, axis=0, keepdims=True)
        idx = jnp.min(jnp.where(cur == mx, ei, N_EXPERTS), axis=0, keepdims=True)
        pick = ei == idx
        ws.append(jnp.sum(jnp.where(pick, scores, 0.0), axis=0, keepdims=True))
        idxs.append(idx)
        picks.append(pick)
        cur = jnp.where(pick, ninf, cur)
    total = ws[0]
    for w in ws[1:]:
        total = total + w
    ws = [w / total * ROUTED_SCALE for w in ws]
    onehot = jnp.zeros((N_EXPERTS, tm), F32)
    for pick in picks:
        onehot = jnp.where(pick, 1.0, onehot)
    tri = (lax.broadcasted_iota(jnp.int32, (tm, tm), 0) < lax.broadcasted_iota(jnp.int32, (tm, tm), 1))
    before = _dot(onehot.astype(BF16), jnp.where(tri, 1.0, 0.0).astype(BF16)) + base_ref[...]
    ranks = [jnp.sum(jnp.where(pick, before, 0.0), axis=0, keepdims=True).astype(jnp.int32) for pick in picks]
    base_ref[...] += jnp.sum(onehot, axis=1, keepdims=True)
    return idxs, ws, ranks


def _norm_kernel(*refs, has_resid, modulated, with_router, out_dtype):
    it = iter(refs)
    x_ref = next(it)
    if has_resid:
        o_ref = next(it)
        gate_ref = next(it)
    g_ref = next(it)
    if modulated:
        shift_ref = next(it)
        scale_ref = next(it)
    if with_router:
        rwt_ref = next(it)
        rb_ref = next(it)
    if has_resid:
        xn_ref = next(it)
    h_ref = next(it)
    if with_router:
        hp_ref = next(it)
        eidx_ref = next(it)
        ew_ref = next(it)
        rank_ref = next(it)
        cnt_ref = next(it)
        base_ref = next(it)

        @pl.when(pl.program_id(0) == 0)
        def _():
            base_ref[...] = jnp.zeros_like(base_ref)

    x = x_ref[...]
    if has_resid:
        x = x + gate_ref[...] * o_ref[...].astype(F32)
        xn_ref[...] = x
    y = x * lax.rsqrt(jnp.mean(x * x, axis=-1, keepdims=True) + EPS)
    y = y * g_ref[...]
    if modulated:
        y = y * (1.0 + scale_ref[...]) + shift_ref[...]
    h_ref[...] = y.astype(out_dtype)
    if with_router:
        hp_ref[...] = _pack_bf16_pairs(y)
        idxs, ws, ranks = _router_select(y, rwt_ref, rb_ref, base_ref)
        for k in range(MOE_TOP_K):
            eidx_ref[k:k + 1, :] = idxs[k]
            ew_ref[k:k + 1, :] = ws[k]
            rank_ref[k:k + 1, :] = ranks[k]
        cnt_ref[...] = base_ref[...].astype(jnp.int32)


def _norm_call(x, n_rows, gain, *, resid=None, mod=None, router=None, out_dtype=BF16):
    d = x.shape[1]
    tm = ROW_TILE
    nblk = n_rows // tm
    lat_blocks = SEQ // tm

    def cls_map(i):
        return (jnp.where(i >= lat_blocks, 1, 0), 0, 0)

    row = pl.BlockSpec((tm, d), lambda i: (i, 0))
    vec = pl.BlockSpec((None, 1, d), cls_map)
    args, in_specs = [x], [row]
    if resid is not None:
        o, gate = resid
        args += [o, gate.reshape(2, 1, d)]
        in_specs += [row, vec]
    args.append(gain.reshape(1, d))
    in_specs.append(pl.BlockSpec((1, d), lambda i: (0, 0)))
    if mod is not None:
        shift, scale = mod
        args += [shift.reshape(2, 1, d), scale.reshape(2, 1, d)]
        in_specs += [vec, vec]
    if router is not None:
        rw, rb = router
        args += [rw.T, rb.reshape(N_EXPERTS, 1)]
        in_specs += [pl.BlockSpec((N_EXPERTS, d), lambda i: (0, 0)),
                     pl.BlockSpec((N_EXPERTS, 1), lambda i: (0, 0))]
    out_shape, out_specs = [], []
    if resid is not None:
        out_shape.append(jax.ShapeDtypeStruct((n_rows, d), F32))
        out_specs.append(row)
    out_shape.append(jax.ShapeDtypeStruct((n_rows, d), out_dtype))
    out_specs.append(row)
    if router is not None:
        out_shape += [jax.ShapeDtypeStruct((n_rows, d // 2), jnp.uint32),
                      jax.ShapeDtypeStruct((MOE_TOP_K, n_rows), jnp.int32),
                      jax.ShapeDtypeStruct((MOE_TOP_K, n_rows), F32),
                      jax.ShapeDtypeStruct((MOE_TOP_K, n_rows), jnp.int32),
                      jax.ShapeDtypeStruct((N_EXPERTS, 1), jnp.int32)]
        topk = pl.BlockSpec((MOE_TOP_K, tm), lambda i: (0, i))
        out_specs += [pl.BlockSpec((tm, d // 2), lambda i: (i, 0)), topk, topk, topk,
                      pl.BlockSpec((N_EXPERTS, 1), lambda i: (0, 0))]
    return pl.pallas_call(
        functools.partial(_norm_kernel, has_resid=resid is not None, modulated=mod is not None,
                          with_router=router is not None, out_dtype=out_dtype),
        grid=(nblk,),
        in_specs=in_specs,
        out_specs=out_specs,
        out_shape=out_shape,
        scratch_shapes=[pltpu.VMEM((N_EXPERTS, 1), F32)] if router is not None else [],
        compiler_params=_cparams(("arbitrary",), 16 * tm * d * 4),
        name="norm_router" if router is not None else "norm",
    )(*args)


def _mm_kernel(a_ref, w_ref, o_ref, wb_ref):
    @pl.when(pl.program_id(1) == 0)
    def _():
        wb_ref[...] = w_ref[...].astype(BF16)

    o_ref[...] = _dot(a_ref[...], wb_ref[...]).astype(o_ref.dtype)


def _matmul(a, w, m_rows, out_dtype, tm, tn):
    k, n = w.shape
    assert m_rows % tm == 0 and n % tn == 0 and a.shape[1] == k
    osz = jnp.dtype(out_dtype).itemsize
    vmem = 2 * k * tn * 4 + k * tn * 2 + 2 * tm * k * 2 + 2 * tm * tn * osz + tm * tn * 4
    return pl.pallas_call(
        _mm_kernel,
        grid=(n // tn, m_rows // tm),
        in_specs=[pl.BlockSpec((tm, k), lambda j, i: (i, 0)),
                  pl.BlockSpec((k, tn), lambda j, i: (0, j))],
        out_specs=pl.BlockSpec((tm, tn), lambda j, i: (i, j)),
        out_shape=jax.ShapeDtypeStruct((m_rows, n), out_dtype),
        scratch_shapes=[pltpu.VMEM((k, tn), BF16)],
        compiler_params=_cparams(("arbitrary", "arbitrary"), vmem + (4 << 20)),
        name="matmul",
    )(a, w)


def _pair_swap_matrix(w):
    r = lax.broadcasted_iota(jnp.int32, (w, w), 0)
    c = lax.broadcasted_iota(jnp.int32, (w, w), 1)
    return jnp.where(r == jnp.bitwise_xor(c, 1), 1.0, 0.0).astype(BF16)


def _rope(x, cos_e, sin_s, swap):
    swapped = _dot(x.astype(BF16), swap)
    return x * cos_e + swapped * sin_s


def _rms(x, g):
    return x * lax.rsqrt(jnp.mean(x * x, axis=-1, keepdims=True) + EPS) * g


def _mla_prep1_kernel(lat_ref, qg_ref, kvg_ref, cos_ref, sin_ref, nq_ref, nkv_ref, kr_ref):
    lat = lat_ref[...]
    nq_ref[...] = _rms(lat[:, :MLA_Q_RANK], qg_ref[...]).astype(BF16)
    nkv_ref[...] = _rms(lat[:, MLA_Q_RANK:MLA_Q_RANK + MLA_KV_RANK], kvg_ref[...]).astype(BF16)
    kr = lat[:, MLA_Q_RANK + MLA_KV_RANK:]
    kr_ref[...] = _rope(kr, cos_ref[...], sin_ref[...], _pair_swap_matrix(kr.shape[1])).astype(BF16)


def _mla_prep1(lat, q_norm, kv_norm, cos_t, sin_t):
    n, wdt = lat.shape
    tm = ROW_TILE
    row = lambda w: pl.BlockSpec((tm, w), lambda i: (i, 0))
    one = lambda w: pl.BlockSpec((1, w), lambda i: (0, 0))
    return pl.pallas_call(
        _mla_prep1_kernel,
        grid=(n // tm,),
        in_specs=[row(wdt), one(MLA_Q_RANK), one(MLA_KV_RANK), row(128), row(128)],
        out_specs=[row(MLA_Q_RANK), row(MLA_KV_RANK), row(128)],
        out_shape=[jax.ShapeDtypeStruct((n, MLA_Q_RANK), BF16),
                   jax.ShapeDtypeStruct((n, MLA_KV_RANK), BF16),
                   jax.ShapeDtypeStruct((n, 128), BF16)],
        compiler_params=_cparams(("arbitrary",), 8 << 20),
        name="mla_prep1",
    )(lat, q_norm.reshape(1, -1), kv_norm.reshape(1, -1), cos_t, sin_t)


def _mla_prep2_kernel(q_ref, kn_ref, kr_ref, cos_ref, sin_ref, qo_ref, ko_ref, *, scale):
    cos_e = cos_ref[...]
    sin_s = sin_ref[...]
    kr = kr_ref[...]
    swap = _pair_swap_matrix(MLA_QK_PAD - MLA_NOPE)
    for h in range(MLA_HEADS):
        lo = h * MLA_QK_PAD
        qo_ref[:, lo:lo + MLA_NOPE] = (q_ref[:, lo:lo + MLA_NOPE].astype(F32) * scale).astype(BF16)
        qr = q_ref[:, lo + MLA_NOPE:lo + MLA_QK_PAD].astype(F32)
        qo_ref[:, lo + MLA_NOPE:lo + MLA_QK_PAD] = (_rope(qr, cos_e, sin_s, swap) * scale).astype(BF16)
        ko_ref[:, lo:lo + MLA_NOPE] = kn_ref[:, h * MLA_NOPE:(h + 1) * MLA_NOPE]
        ko_ref[:, lo + MLA_NOPE:lo + MLA_QK_PAD] = kr


def _mla_prep2(q_raw, kv_all, kr, cos_t, sin_t):
    n = q_raw.shape[0]
    tm = ROW_TILE
    wq = MLA_HEADS * MLA_QK_PAD
    row = lambda w: pl.BlockSpec((tm, w), lambda i: (i, 0))
    return pl.pallas_call(
        functools.partial(_mla_prep2_kernel, scale=(MLA_NOPE + MLA_ROPE) ** -0.5 * LOG2E),
        grid=(n // tm,),
        in_specs=[row(wq), row(MLA_HEADS * MLA_NOPE), row(128), row(128), row(128)],
        out_specs=[row(wq), row(wq)],
        out_shape=[jax.ShapeDtypeStruct((n, wq), BF16), jax.ShapeDtypeStruct((n, wq), BF16)],
        compiler_params=_cparams(("arbitrary",), 16 << 20),
        name="mla_prep2",
    )(q_raw, kv_all, kr, cos_t, sin_t)


def _gqa_prep_kernel(qkv_ref, qg_ref, kg_ref, cos_ref, sin_ref, qo_ref, ko_ref, *, scale):
    cos_e = cos_ref[...]
    sin_s = sin_ref[...]
    swap = _pair_swap_matrix(HEAD_DIM)
    for h in range(GQA_HEADS):
        xh = qkv_ref[:, h * HEAD_DIM:(h + 1) * HEAD_DIM].astype(F32)
        qo_ref[:, h * HEAD_DIM:(h + 1) * HEAD_DIM] = (
            _rope(_rms(xh, qg_ref[...]), cos_e, sin_s, swap) * scale).astype(BF16)
    for h in range(GQA_KV):
        lo = (GQA_HEADS + h) * HEAD_DIM
        xh = qkv_ref[:, lo:lo + HEAD_DIM].astype(F32)
        ko_ref[:, h * HEAD_DIM:(h + 1) * HEAD_DIM] = _rope(_rms(xh, kg_ref[...]), cos_e, sin_s, swap).astype(BF16)


def _gqa_prep(qkv, q_norm, k_norm, cos_t, sin_t):
    n = qkv.shape[0]
    tm = ROW_TILE
    wqk = (GQA_HEADS + GQA_KV) * HEAD_DIM
    row = lambda w: pl.BlockSpec((tm, w), lambda i: (i, 0))
    one = pl.BlockSpec((1, HEAD_DIM), lambda i: (0, 0))
    return pl.pallas_call(
        functools.partial(_gqa_prep_kernel, scale=HEAD_DIM ** -0.5 * LOG2E),
        grid=(n // tm,),
        in_specs=[row(wqk), one, one, row(HEAD_DIM), row(HEAD_DIM)],
        out_specs=[row(GQA_HEADS * HEAD_DIM), row(GQA_KV * HEAD_DIM)],
        out_shape=[jax.ShapeDtypeStruct((n, GQA_HEADS * HEAD_DIM), BF16),
                   jax.ShapeDtypeStruct((n, GQA_KV * HEAD_DIM), BF16)],
        compiler_params=_cparams(("arbitrary",), 16 << 20),
        name="gqa_prep",
    )(qkv, q_norm.reshape(1, -1), k_norm.reshape(1, -1), cos_t, sin_t)


def _flash_kernel(q_ref, k_ref, v_ref, o_ref, sa_ref, sb_ref, *, group, dk, dv, tk, n_lat, q_scale):
    tq = q_ref.shape[0]
    q = jnp.concatenate([q_ref[:, g * dk:(g + 1) * dk] for g in range(group)], axis=0) if group > 1 else q_ref[...]
    if q_scale is not None:
        q = (q.astype(F32) * q_scale).astype(BF16)

    s = _dot_nt(q, k_ref[SEQ:SEQ + CTX_LEN, :])
    m = jnp.max(s, axis=-1, keepdims=True)
    p = jnp.exp2(s - m)
    l = jnp.sum(p, axis=-1, keepdims=True)
    acc = _dot(p.astype(BF16), v_ref[SEQ:SEQ + CTX_LEN, :])

    if n_lat:
        def rows(ref, j):
            return ref[pl.ds(pl.multiple_of(j * tk, tk), tk), :]

        def update(s, vb, m, l, acc):
            m_new = jnp.maximum(m, jnp.max(s, axis=-1, keepdims=True))
            alpha = jnp.exp2(m - m_new)
            p = jnp.exp2(s - m_new)
            l = alpha * l + jnp.sum(p, axis=-1, keepdims=True)
            return m_new, l, alpha * acc + _dot(p.astype(BF16), vb)

        def pair(jj, carry, last):
            j0 = 2 * jj
            sb_ref[...] = _dot_nt(q, rows(k_ref, j0 + 1))
            carry = update(sa_ref[...], rows(v_ref, j0), *carry)
            if not last:
                sa_ref[...] = _dot_nt(q, rows(k_ref, j0 + 2))
            return update(sb_ref[...], rows(v_ref, j0 + 1), *carry)

        sa_ref[...] = _dot_nt(q, rows(k_ref, 0))
        carry = lax.fori_loop(0, n_lat // 2 - 1, lambda jj, c: pair(jj, c, False), (m, l, acc))
        m, l, acc = pair(n_lat // 2 - 1, carry, True)

    o = (acc / l).astype(o_ref.dtype)
    for g in range(group):
        o_ref[:, g * dv:(g + 1) * dv] = o[g * tq:(g + 1) * tq]


def _flash(q_arr, k_arr, v_arr, *, n_kv_heads, group, dk, dv, q_col0, k_col0, v_col0, ctx_queries, tq, tk,
           q_scale=None):
    n_tok = k_arr.shape[0]
    if ctx_queries:
        n_q, q_blk0, n_lat = CTX_LEN, SEQ // tq, 0
    else:
        n_q, q_blk0, n_lat = SEQ, 0, SEQ // tk
        assert n_lat % 2 == 0
    m = group * tq
    s_shape = (m, tk) if n_lat else (8, 128)
    kv_mode = pl.Buffered(1) if m * tk * 4 >= (8 << 20) else None
    vmem = (2 * n_tok * (dk + dv) * 2 + 4 * tq * group * (dk + dv) * 2
            + 5 * m * max(tk, CTX_LEN) * 4 + 4 * m * dv * 4)
    return pl.pallas_call(
        functools.partial(_flash_kernel, group=group, dk=dk, dv=dv, tk=tk, n_lat=n_lat, q_scale=q_scale),
        grid=(n_kv_heads, n_q // tq),
        in_specs=[pl.BlockSpec((tq, group * dk), lambda h, i: (q_blk0 + i, q_col0 + h)),
                  pl.BlockSpec((n_tok, dk), lambda h, i: (0, k_col0 + h), pipeline_mode=kv_mode),
                  pl.BlockSpec((n_tok, dv), lambda h, i: (0, v_col0 + h), pipeline_mode=kv_mode)],
        out_specs=pl.BlockSpec((tq, group * dv), lambda h, i: (i, h)),
        out_shape=jax.ShapeDtypeStruct((n_q, n_kv_heads * group * dv), BF16),
        scratch_shapes=[pltpu.VMEM(s_shape, F32), pltpu.VMEM(s_shape, F32)],
        compiler_params=_cparams(("arbitrary", "arbitrary"), vmem + (8 << 20)),
        name="flash_ctx" if ctx_queries else "flash",
    )(q_arr, k_arr, v_arr)


def _na_start_row(b):
    return jnp.clip(NA_RB * b - NA_KH // 2, 0, GRID_H - NA_U)


def _na_kernel(q_ref, k_ref, v_ref, bias_ref, o_ref, *, scale):
    b = pl.program_id(1)
    start = pl.multiple_of(_na_start_row(b) * GRID_W, GRID_W)
    nwin = NA_U * GRID_W
    q = (q_ref[...].astype(F32) * scale).astype(BF16)
    s_loc = _dot_nt(q, k_ref[pl.ds(start, nwin), :]) + bias_ref[...]
    s_ctx = _dot_nt(q, k_ref[SEQ:SEQ + CTX_LEN, :])
    m = jnp.maximum(jnp.max(s_loc, axis=-1, keepdims=True), jnp.max(s_ctx, axis=-1, keepdims=True))
    p_loc = jnp.exp(s_loc - m)
    p_ctx = jnp.exp(s_ctx - m)
    l = jnp.sum(p_loc, axis=-1, keepdims=True) + jnp.sum(p_ctx, axis=-1, keepdims=True)
    o = _dot(p_loc.astype(BF16), v_ref[pl.ds(start, nwin), :]) + _dot(p_ctx.astype(BF16), v_ref[SEQ:SEQ + CTX_LEN, :])
    o_ref[...] = (o / l).astype(o_ref.dtype)


def _na_bias_tables(rpb):
    h = rpb.shape[0]
    w = np.arange(GRID_W)
    c0 = np.clip(w - NA_KW // 2, 0, GRID_W - NA_KW)
    colmask = (w[None, :] >= c0[:, None]) & (w[None, :] < c0[:, None] + NA_KW)
    colidx = np.clip(w[None, :] - w[:, None] + NA_KW - 1, 0, 2 * NA_KW - 2)
    slabs = jnp.where(colmask[None, None], rpb[:, :, colidx], NEG)
    slabs = jnp.concatenate([slabs, jnp.full((h, 1, GRID_W, GRID_W), NEG, rpb.dtype)], axis=1)
    masked = 2 * NA_KH - 1
    nb = GRID_H // NA_RB
    tabs = []
    for b in (0, 1, nb - 1):
        start = int(np.clip(NA_RB * b - NA_KH // 2, 0, GRID_H - NA_U))
        a = np.full((NA_RB, NA_U), masked, np.int32)
        for j in range(NA_RB):
            r = NA_RB * b + j
            r0 = int(np.clip(r - NA_KH // 2, 0, GRID_H - NA_KH))
            for u in range(NA_U):
                kr = start + u
                if r0 <= kr < r0 + NA_KH:
                    a[j, u] = kr - r + NA_KH - 1
        t = slabs[:, a]
        tabs.append(t.transpose(0, 1, 3, 2, 4).reshape(h, NA_RB * GRID_W, NA_U * GRID_W))
    return jnp.stack(tabs, axis=1)


def _na_attention(qkv, bias_tabs):
    n_tok = qkv.shape[0]
    nb = GRID_H // NA_RB
    tq = NA_RB * GRID_W
    nwin = NA_U * GRID_W

    def cls(b):
        return jnp.where(b == 0, 0, jnp.where(b == nb - 1, 2, 1))

    vmem = 4 * n_tok * HEAD_DIM * 2 + 2 * tq * nwin * 4 + 4 * tq * (nwin + CTX_LEN) * 4
    return pl.pallas_call(
        functools.partial(_na_kernel, scale=HEAD_DIM ** -0.5),
        grid=(NA_HEADS, nb),
        in_specs=[pl.BlockSpec((tq, HEAD_DIM), lambda h, b: (b, h)),
                  pl.BlockSpec((n_tok, HEAD_DIM), lambda h, b: (0, NA_HEADS + h)),
                  pl.BlockSpec((n_tok, HEAD_DIM), lambda h, b: (0, 2 * NA_HEADS + h)),
                  pl.BlockSpec((None, None, tq, nwin), lambda h, b: (h, cls(b), 0, 0))],
        out_specs=pl.BlockSpec((tq, HEAD_DIM), lambda h, b: (b, h)),
        out_shape=jax.ShapeDtypeStruct((SEQ, NA_HEADS * HEAD_DIM), BF16),
        compiler_params=_cparams(("arbitrary", "arbitrary"), vmem + (8 << 20)),
        name="na_attention",
    )(qkv, qkv, qkv, bias_tabs)


def _dispatch_kernel(tail_ref, nu_ref, slot_ref, h_ref, xg_hbm, zbuf, sem, zsem, *, tb, n_blocks):
    te = EXPERT_TILE

    @pl.when(pl.program_id(0) == 0)
    def _():
        zbuf[...] = jnp.zeros_like(zbuf)

        def zero_tile(row0, wait):
            cp = pltpu.make_async_copy(zbuf, xg_hbm.at[pl.ds(pl.multiple_of(row0, te), te)], zsem)
            if wait:
                cp.wait()
            else:
                cp.start()

        def zero_tail(e, wait):
            @pl.when(tail_ref[e] >= 0)
            def _():
                zero_tile(tail_ref[e], wait)

        for wait in (False, True):
            lax.fori_loop(0, N_EXPERTS, lambda e, c: (zero_tail(e, wait), c)[1], 0)
            lax.fori_loop(nu_ref[0], n_blocks, lambda b, c: (zero_tile(b * te, wait), c)[1], 0)

    def body(t, carry):
        for k in range(MOE_TOP_K):
            pltpu.make_async_copy(h_ref.at[pl.ds(t, 1)], xg_hbm.at[pl.ds(slot_ref[k, t], 1)], sem).start(
                priority=k % 2)
        return carry

    lax.fori_loop(0, tb, body, 0, unroll=DMA_LOOP_UNROLL)
    for k in range(MOE_TOP_K):
        pltpu.make_async_copy(h_ref, xg_hbm.at[pl.ds(0, tb)], sem).wait()


def _dispatch(h32, slot_t, tail_start, n_used, n_rows, n_slots):
    d = h32.shape[1]
    tb = ROW_TILE
    grid_spec = pltpu.PrefetchScalarGridSpec(
        num_scalar_prefetch=2,
        grid=(n_rows // tb,),
        in_specs=[pl.BlockSpec((MOE_TOP_K, tb), lambda i, tl, nu: (0, i), memory_space=pltpu.SMEM),
                  pl.BlockSpec((tb, d), lambda i, tl, nu: (i, 0))],
        out_specs=pl.BlockSpec(memory_space=pl.ANY),
        scratch_shapes=[pltpu.VMEM((EXPERT_TILE, d), h32.dtype), pltpu.SemaphoreType.DMA(()),
                        pltpu.SemaphoreType.DMA(())],
    )
    return pl.pallas_call(
        functools.partial(_dispatch_kernel, tb=tb, n_blocks=n_slots // EXPERT_TILE),
        grid_spec=grid_spec,
        out_shape=jax.ShapeDtypeStruct((n_slots, d), h32.dtype),
        compiler_params=_cparams(("arbitrary",), 16 << 20),
        name="moe_dispatch",
    )(tail_start, n_used, slot_t, h32)


FFN_LEAD = 2


def _ffn_kernel(be_ref, ord_ref, nu_ref, x_ref, wg_ref, wu_ref, wd_ref, o_ref, wgb, wub, wdb, *, packed, nb):
    b = pl.program_id(0)
    c = b - FFN_LEAD

    def at(ref, i):
        return ref[jnp.clip(i, 0, nb - 1)]

    def starts_expert(i):
        return jnp.logical_or(b == 0, at(be_ref, i) != at(be_ref, i - 1))

    @pl.when(starts_expert(c))
    def _():
        wdb[at(ord_ref, c) % 2] = wd_ref[...].astype(BF16)

    @pl.when(jnp.logical_and(c >= 0, c < nu_ref[0]))
    def _():
        slot = at(ord_ref, c) % 2
        x = _unpack_bf16_pairs(x_ref[...]).astype(BF16) if packed else x_ref[...]
        g = _dot(x, wgb[slot])
        u = _dot(x, wub[slot])
        a = (g * jax.nn.sigmoid(g) * u).astype(BF16)
        y = _dot(a, wdb[slot])
        o_ref[...] = _pack_bf16_pairs(y) if packed else y.astype(o_ref.dtype)

    @pl.when(c >= nu_ref[0])
    def _():
        o_ref[...] = jnp.zeros_like(o_ref)

    @pl.when(starts_expert(b))
    def _():
        wgb[at(ord_ref, b) % 2] = wg_ref[...].astype(BF16)

    @pl.when(starts_expert(b - 1))
    def _():
        wub[at(ord_ref, b - 1) % 2] = wu_ref[...].astype(BF16)


def _expert_ffn(x_rows, block_expert, n_used, wg, wu, wd, layer, tb, packed):
    p, dx = x_rows.shape
    d, de = wg.shape[-2:]
    nb = p // tb
    out_dtype = x_rows.dtype
    vmem = 2 * 3 * d * de * 4 + 2 * 3 * d * de * 2 + 4 * tb * d * 2 + 4 * tb * de * 4 + 3 * tb * d * 4
    idx = jnp.arange(nb, dtype=jnp.int32)
    be = jnp.where(idx < n_used[0], block_expert, block_expert[jnp.maximum(n_used[0] - 1, 0)])
    ordinal = jnp.cumsum(jnp.concatenate([jnp.zeros((1,), jnp.int32),
                                          (be[1:] != be[:-1]).astype(jnp.int32)])).astype(jnp.int32)

    def blk(i):
        return jnp.clip(i, 0, nb - 1)

    def wspec(shape, lag):
        return pl.BlockSpec((None, None) + shape, lambda b, be, od, nu: (layer, be[blk(b - lag)], 0, 0))

    grid_spec = pltpu.PrefetchScalarGridSpec(
        num_scalar_prefetch=3,
        grid=(nb + FFN_LEAD,),
        in_specs=[pl.BlockSpec((tb, dx), lambda b, be, od, nu: (jnp.minimum(blk(b - FFN_LEAD), nu[0] - 1), 0)),
                  wspec((d, de), 0), wspec((d, de), 1), wspec((de, d), 2)],
        out_specs=pl.BlockSpec((tb, dx), lambda b, be, od, nu: (blk(b - FFN_LEAD), 0)),
        scratch_shapes=[pltpu.VMEM((2, d, de), BF16), pltpu.VMEM((2, d, de), BF16), pltpu.VMEM((2, de, d), BF16)],
    )
    return pl.pallas_call(
        functools.partial(_ffn_kernel, packed=packed, nb=nb),
        grid_spec=grid_spec,
        out_shape=jax.ShapeDtypeStruct((p, dx), out_dtype),
        compiler_params=_cparams(("arbitrary",), vmem + (4 << 20)),
        name="expert_ffn",
    )(be, ordinal, n_used, x_rows, wg, wu, wd)


def _combine_kernel(*refs, tc, nblk, modulated, emit_x):
    it = iter(refs)
    s_cur, s_nxt, y_hbm, ew_ref, ysh_ref, x_ref, gate_ref, g_ref = [next(it) for _ in range(8)]
    if modulated:
        shift_ref, scale_ref = next(it), next(it)
    if emit_x:
        o_ref = next(it)
    h_ref, buf, sem = next(it), next(it), next(it)
    i = pl.program_id(0)

    def gather(slot_ref, bslot):
        def body(t, carry):
            for k in range(MOE_TOP_K):
                pltpu.make_async_copy(y_hbm.at[pl.ds(slot_ref[k, t], 1)],
                                      buf.at[bslot, k, pl.ds(t, 1)], sem.at[bslot]).start()
            return carry

        lax.fori_loop(0, tc, body, 0, unroll=DMA_LOOP_UNROLL)

    @pl.when(i == 0)
    def _():
        gather(s_cur, 0)

    cur = i % 2

    @pl.when(i + 1 < nblk)
    def _():
        @pl.when(cur == 0)
        def _():
            gather(s_nxt, 1)

        @pl.when(cur == 1)
        def _():
            gather(s_nxt, 0)

    def finish(bslot):
        for k in range(MOE_TOP_K):
            pltpu.make_async_copy(y_hbm.at[pl.ds(0, tc)], buf.at[bslot, k], sem.at[bslot]).wait()
        ew = ew_ref[...]
        acc = ysh_ref[...].astype(F32)
        for k in range(MOE_TOP_K):
            acc = acc + ew[:, k:k + 1] * _unpack_bf16_pairs(buf[bslot, k])
        xn = x_ref[...] + gate_ref[...] * acc
        if emit_x:
            o_ref[...] = xn
        hn = _rms(xn, g_ref[...])
        if modulated:
            hn = hn * (1.0 + scale_ref[...]) + shift_ref[...]
        h_ref[...] = hn.astype(h_ref.dtype)

    @pl.when(cur == 0)
    def _():
        finish(0)

    @pl.when(cur == 1)
    def _():
        finish(1)


def _combine(y, slot_t, ew, ysh, x, gate, n_rows, gain, mod):
    d = x.shape[1]
    tc = COMBINE_TILE
    nblk = n_rows // tc
    lat_blocks = SEQ // tc
    row = pl.BlockSpec((tc, d), lambda i: (i, 0))
    vec = pl.BlockSpec((None, 1, d), lambda i: (jnp.where(i >= lat_blocks, 1, 0), 0, 0))
    vmem = 2 * MOE_TOP_K * tc * d * 2 + 12 * tc * d * 4
    args = [slot_t, slot_t, y, ew, ysh, x, gate.reshape(2, 1, d), gain.reshape(1, d)]
    in_specs = [pl.BlockSpec((MOE_TOP_K, tc), lambda i: (0, i), memory_space=pltpu.SMEM),
                pl.BlockSpec((MOE_TOP_K, tc), lambda i: (0, jnp.minimum(i + 1, nblk - 1)),
                             memory_space=pltpu.SMEM),
                pl.BlockSpec(memory_space=pl.ANY),
                pl.BlockSpec((tc, MOE_TOP_K), lambda i: (i, 0)),
                row, row, vec, pl.BlockSpec((1, d), lambda i: (0, 0))]
    if mod is not None:
        args += [mod[0].reshape(2, 1, d), mod[1].reshape(2, 1, d)]
        in_specs += [vec, vec]
        out_shape = [jax.ShapeDtypeStruct((n_rows, d), F32), jax.ShapeDtypeStruct((n_rows, d), BF16)]
    else:
        out_shape = [jax.ShapeDtypeStruct((n_rows, d), F32)]
    return pl.pallas_call(
        functools.partial(_combine_kernel, tc=tc, nblk=nblk, modulated=mod is not None, emit_x=mod is not None),
        grid=(nblk,),
        in_specs=in_specs,
        out_specs=[row] * len(out_shape),
        out_shape=out_shape,
        scratch_shapes=[pltpu.VMEM((2, MOE_TOP_K, tc, d // 2), jnp.uint32), pltpu.SemaphoreType.DMA((2,))],
        compiler_params=_cparams(("arbitrary",), vmem + (4 << 20)),
        name="moe_combine",
    )(*args)


def _routing_tables(counts, n_rows):
    tb = EXPERT_TILE
    n_blocks = -(-(n_rows * MOE_TOP_K) // tb) + N_EXPERTS
    padded = (counts + tb - 1) // tb * tb
    pad_end = jnp.cumsum(padded)
    pad_start = (pad_end - padded).astype(jnp.int32)
    tail_start = jnp.where(padded > 0, pad_end - tb, -1).astype(jnp.int32)
    blk_start = jnp.arange(n_blocks, dtype=jnp.int32) * tb
    block_expert = jnp.sum(blk_start[:, None] >= pad_end[None, :], axis=1)
    block_expert = jnp.minimum(block_expert, N_EXPERTS - 1).astype(jnp.int32)
    n_used = (pad_end[-1] // tb).astype(jnp.int32).reshape(1)
    return pad_start, tail_start, block_expert, n_used, n_blocks * tb


def _moe(layer, x_new, h, h32, eidx_t, ew_t, rank_t, counts, gate, n_rows, wg, wu, wd, sg, su, sd, next_gain, next_mod):
    pad_start, tail_start, block_expert, n_used, n_slots = _routing_tables(counts[:, 0], n_rows)
    e_iota = jnp.arange(N_EXPERTS, dtype=jnp.int32)
    slot_t = jnp.sum(jnp.where(eidx_t[:, :, None] == e_iota, pad_start, 0), axis=-1) + rank_t
    xg = _dispatch(h32, slot_t, tail_start, n_used, n_rows, n_slots)
    y = _expert_ffn(xg, block_expert, n_used, wg, wu, wd, layer, EXPERT_TILE, True)
    tm = MM_TM if n_rows % MM_TM == 0 else 512
    ysh = _expert_ffn(h, jnp.zeros((n_rows // tm,), jnp.int32), jnp.full((1,), n_rows // tm, jnp.int32),
                      sg[:, None], su[:, None], sd[:, None], layer, tm, False)
    return _combine(y, slot_t, ew_t.T, ysh, x_new, gate, n_rows, next_gain, next_mod)


def _rope_tables(d, width):
    t = jnp.arange(SEQ)
    row = (t // GRID_W).astype(F32)
    col = (t % GRID_W).astype(F32)
    n = d // 4
    inv = ROPE_THETA ** (-jnp.arange(n, dtype=F32) / n)
    ang = jnp.concatenate([row[:, None] * inv, col[:, None] * inv], axis=-1)
    ang = jnp.concatenate([ang, jnp.zeros((CTX_LEN, d // 2), F32)], axis=0)
    cos = jnp.repeat(jnp.cos(ang), 2, axis=1)
    sin = jnp.repeat(jnp.sin(ang), 2, axis=1) * jnp.tile(jnp.array([-1.0, 1.0], F32), d // 2)
    if width > d:
        cos = jnp.concatenate([cos, jnp.ones((N_TOK, width - d), F32)], axis=1)
        sin = jnp.concatenate([sin, jnp.zeros((N_TOK, width - d), F32)], axis=1)
    return cos, sin


def _mixer_even(h, w_in, rpb, q_norm, w_uq, kv_norm, w_ukv, w_out):
    n_na = 3 * NA_HEADS * HEAD_DIM
    qkv = _matmul(h, w_in[:, :n_na], N_TOK, BF16, MM_TM, 1024)
    w_tail = jnp.pad(w_in[:, n_na:], ((0, 0), (0, 64)))
    lat = _matmul(h, w_tail, N_TOK, F32, MM_TM, 384)
    cos_t, sin_t = _rope_tables(MLA_ROPE, 128)
    nq, nkv, kr = _mla_prep1(lat, q_norm, kv_norm, cos_t, sin_t)
    wq = w_uq.reshape(MLA_Q_RANK, MLA_HEADS, MLA_NOPE + MLA_ROPE)
    wq = jnp.pad(wq, ((0, 0), (0, 0), (0, MLA_QK_PAD - MLA_NOPE - MLA_ROPE))).reshape(MLA_Q_RANK, -1)
    wkv = w_ukv.reshape(MLA_KV_RANK, MLA_HEADS, MLA_NOPE + MLA_V)
    wkv = jnp.concatenate([wkv[:, :, :MLA_NOPE].reshape(MLA_KV_RANK, -1),
                           wkv[:, :, MLA_NOPE:].reshape(MLA_KV_RANK, -1)], axis=1)
    q_raw = _matmul(nq, wq, N_TOK, F32, MM_TM, 1024)
    kv_all = _matmul(nkv, wkv, N_TOK, BF16, MM_TM, 1024)
    q_mla, k_cat = _mla_prep2(q_raw, kv_all, kr, cos_t, sin_t)

    o_na = _na_attention(qkv, _na_bias_tables(rpb))
    mla = dict(n_kv_heads=MLA_HEADS, group=1, dk=MLA_QK_PAD, dv=MLA_V, q_col0=0, k_col0=0, v_col0=MLA_HEADS)
    o_mla = _flash(q_mla, k_cat, kv_all, ctx_queries=False, tq=1024, tk=2048, **mla)
    o_mla_c = _flash(q_mla, k_cat, kv_all, ctx_queries=True, tq=256, tk=512, **mla)
    na = dict(n_kv_heads=NA_HEADS, group=1, dk=HEAD_DIM, dv=HEAD_DIM, q_col0=0, k_col0=NA_HEADS, v_col0=2 * NA_HEADS)
    o_na_c = _flash(qkv, qkv, qkv, ctx_queries=True, tq=256, tk=512, q_scale=HEAD_DIM ** -0.5 * LOG2E, **na)
    o = jnp.concatenate([jnp.concatenate([o_na, o_mla], axis=1),
                         jnp.concatenate([o_na_c, o_mla_c], axis=1)], axis=0)
    return _matmul(o, w_out, N_TOK, F32, MM_TM, 1024)


def _mixer_odd_last(h, w_qkv, q_norm, k_norm, w_o):
    qkv = _matmul(h, w_qkv, N_TOK, BF16, MM_TM, 1024)
    cos_t, sin_t = _rope_tables(HEAD_DIM, HEAD_DIM)
    q_g, k_g = _gqa_prep(qkv, q_norm, k_norm, cos_t, sin_t)
    o = _flash(q_g, k_g, qkv, n_kv_heads=GQA_KV, group=GQA_GROUP, dk=HEAD_DIM, dv=HEAD_DIM,
               q_col0=0, k_col0=0, v_col0=GQA_HEADS + GQA_KV, ctx_queries=False, tq=256, tk=2048)
    return _matmul(o, w_o, SEQ, F32, 512, 1024)


def kernel(x, c, ctx, c_ctx, ada_w, ada_b, mix_norm, ffn_norm, ev_w_in, ev_rpb, ev_q_norm, ev_w_uq, ev_kv_norm, ev_w_ukv, ev_w_out, od_w_qkv, od_q_norm, od_k_norm, od_w_o, router_w, router_b, exp_w_gate, exp_w_up, exp_w_down, sh_w_gate, sh_w_up, sh_w_down, final_norm):
    assert DEPTH == 2 and x.shape == (1, SEQ, D_MODEL)
    mods = _adaln_mods(c, c_ctx, ada_w, ada_b)
    xt = jnp.concatenate([x[0], ctx[0]], axis=0)

    def parts(i):
        return [mods[i, :, k * D_MODEL:(k + 1) * D_MODEL] for k in range(6)]

    sh1, sc1, g1, sh2, sc2, g2 = parts(0)
    (h,) = _norm_call(xt, N_TOK, mix_norm[0], mod=(sh1, sc1))
    o = _mixer_even(h, ev_w_in[0], ev_rpb[0], ev_q_norm[0], ev_w_uq[0], ev_kv_norm[0], ev_w_ukv[0], ev_w_out[0])
    xt, h, h32, eidx_t, ew_t, rank_t, counts = _norm_call(xt, N_TOK, ffn_norm[0], resid=(o, g1), mod=(sh2, sc2),
                                                          router=(router_w[0], router_b[0]))
    n1_sh1, n1_sc1, n1_g1, n1_sh2, n1_sc2, n1_g2 = parts(1)
    xt, h = _moe(0, xt, h, h32, eidx_t, ew_t, rank_t, counts, g2, N_TOK, exp_w_gate, exp_w_up, exp_w_down,
                 sh_w_gate, sh_w_up, sh_w_down, mix_norm[1], (n1_sh1, n1_sc1))

    o = _mixer_odd_last(h, od_w_qkv[0], od_q_norm[0], od_k_norm[0], od_w_o[0])
    xl, h, h32, eidx_t, ew_t, rank_t, counts = _norm_call(xt, SEQ, ffn_norm[1], resid=(o, n1_g1),
                                                          mod=(n1_sh2, n1_sc2),
                                                          router=(router_w[1], router_b[1]))
    (out,) = _moe(1, xl, h, h32, eidx_t, ew_t, rank_t, counts, n1_g2, SEQ, exp_w_gate, exp_w_up, exp_w_down,
                  sh_w_gate, sh_w_up, sh_w_down, final_norm, None)
    return out[None]
```

```python
import functools

import numpy as np
import jax
import jax.numpy as jnp
from jax import lax
from jax.experimental import pallas as pl
from jax.experimental.pallas import tpu as pltpu

F32 = jnp.float32
BF16 = jnp.bfloat16

D_MODEL = 2048
SEQ = 8192
DEPTH = 2
GRID_W = 64
GRID_H = SEQ // GRID_W
CTX_LEN = 256
N_TOK = SEQ + CTX_LEN
EPS = 1e-6
ROPE_THETA = 10000.0
HEAD_DIM = 128
NA_HEADS = 8
NA_KH = 8
NA_KW = 16
MLA_HEADS = 8
MLA_Q_RANK = 512
MLA_KV_RANK = 512
MLA_NOPE = 128
MLA_ROPE = 64
MLA_V = 128
MLA_QK_PAD = 256
GQA_HEADS = 16
GQA_KV = 4
GQA_GROUP = GQA_HEADS // GQA_KV
N_EXPERTS = 64
N_GROUPS = 8
GROUP_SIZE = N_EXPERTS // N_GROUPS
TOPK_GROUPS = 4
MOE_TOP_K = 8
D_EXPERT = 512
ROUTED_SCALE = 2.5

NEG = -1e30
LOG2E = 1.4426950408889634

VMEM_LIMIT_CAP = 60 * 1024 * 1024

ROW_TILE = 256
MM_TM = 768
NA_RB = 4
NA_U = NA_RB + NA_KH - 1
EXPERT_TILE = 256
COMBINE_TILE = 128
DMA_LOOP_UNROLL = 128


def _cparams(sem, vmem_bytes):
    return pltpu.CompilerParams(
        dimension_semantics=sem,
        vmem_limit_bytes=int(min(max(vmem_bytes, 16 * 1024 * 1024), VMEM_LIMIT_CAP)),
    )


def _pack_bf16_pairs(x):
    w = x.shape[1] // 2

    def rounded(v):
        b = lax.bitcast_convert_type(v, jnp.uint32)
        return b + (((b >> 16) & jnp.uint32(1)) + jnp.uint32(0x7FFF))

    return (rounded(x[:, :w]) >> 16) | (rounded(x[:, w:]) & jnp.uint32(0xFFFF0000))


def _unpack_bf16_pairs(p):
    lo = lax.bitcast_convert_type(p << 16, F32)
    hi = lax.bitcast_convert_type(p & jnp.uint32(0xFFFF0000), F32)
    return jnp.concatenate([lo, hi], axis=1)


def _dot(a, b):
    return jnp.dot(a, b, preferred_element_type=F32)


def _dot_nt(a, b):
    return lax.dot_general(a, b, (((1,), (1,)), ((), ())), preferred_element_type=F32)


def _mods_kernel(cc_ref, w_ref, b_ref, o_ref, acc_ref, *, nk):
    k = pl.program_id(2)

    @pl.when(k == 0)
    def _():
        acc_ref[...] = jnp.zeros_like(acc_ref)

    a = cc_ref[...]
    a = a * jax.nn.sigmoid(a)
    w = w_ref[...]
    tk, tn = w.shape
    w3 = w.reshape(tk // 8, 8, tn)
    for v in range(2):
        col = a[:, v:v + 1].reshape(tk // 8, 8, 1)
        acc_ref[v] += jnp.sum(w3 * col, axis=0)

    @pl.when(k == nk - 1)
    def _():
        o_ref[...] = jnp.sum(acc_ref[...], axis=1) + b_ref[...]


def _adaln_mods(c, c_ctx, ada_w, ada_b):
    depth, d, n = ada_w.shape
    tk, tn = 1024, 1536
    nk = d // tk
    cc = jnp.stack([c[0], c_ctx], axis=1)
    return pl.pallas_call(
        functools.partial(_mods_kernel, nk=nk),
        grid=(depth, n // tn, nk),
        in_specs=[
            pl.BlockSpec((tk, 2), lambda i, j, k: (k, 0)),
            pl.BlockSpec((None, tk, tn), lambda i, j, k: (i, k, j)),
            pl.BlockSpec((None, 1, tn), lambda i, j, k: (i, 0, j)),
        ],
        out_specs=pl.BlockSpec((None, 2, tn), lambda i, j, k: (i, 0, j)),
        out_shape=jax.ShapeDtypeStruct((depth, 2, n), F32),
        scratch_shapes=[pltpu.VMEM((2, 8, tn), F32)],
        compiler_params=_cparams(("arbitrary", "arbitrary", "arbitrary"), 4 * tk * tn * 4),
        name="adaln_mods",
    )(cc, ada_w, ada_b.reshape(depth, 1, n))


def _router_select(hf, rwt_ref, rb_ref, base_ref):
    logits = lax.dot_general(rwt_ref[...], hf, (((1,), (1,)), ((), ())),
                             precision=lax.Precision.HIGHEST, preferred_element_type=F32)
    scores = jax.nn.sigmoid(logits)
    sel = scores + rb_ref[...]
    tm = sel.shape[1]
    ninf = jnp.float32(-jnp.inf)
    sub = lax.broadcasted_iota(jnp.int32, (GROUP_SIZE, tm), 0)
    gi = lax.broadcasted_iota(jnp.int32, (N_GROUPS, tm), 0)
    gs = jnp.zeros((N_GROUPS, tm), F32)
    for g in range(N_GROUPS):
        blk = sel[g * GROUP_SIZE:(g + 1) * GROUP_SIZE, :]
        m1 = jnp.max(blk, axis=0, keepdims=True)
        first = jnp.min(jnp.where(blk == m1, sub, GROUP_SIZE), axis=0, keepdims=True)
        m2 = jnp.max(jnp.where(sub == first, ninf, blk), axis=0, keepdims=True)
        gs = jnp.where(gi == g, m1 + m2, gs)
    gsel = jnp.zeros((N_GROUPS, tm), F32)
    cur = gs
    for _ in range(TOPK_GROUPS):
        mx = jnp.max(cur, axis=0, keepdims=True)
        idx = jnp.min(jnp.where(cur == mx, gi, N_GROUPS), axis=0, keepdims=True)
        pick = gi == idx
        gsel = jnp.where(pick, 1.0, gsel)
        cur = jnp.where(pick, ninf, cur)
    ei = lax.broadcasted_iota(jnp.int32, (N_EXPERTS, tm), 0)
    eg = ei // GROUP_SIZE
    emask = jnp.zeros((N_EXPERTS, tm), F32)
    for g in range(N_GROUPS):
        emask = jnp.where(eg == g, gsel[g:g + 1, :], emask)
    cur = jnp.where(emask > 0.0, sel, ninf)
    idxs, ws, picks = [], [], []
    for _ in range(MOE_TOP_K):
        mx = jnp.max(cur, axis=0, keepdims=True)
        idx = jnp.min(jnp.where(cur == mx, ei, N_EXPERTS), axis=0, keepdims=True)
        pick = ei == idx
        ws.append(jnp.sum(jnp.where(pick, scores, 0.0), axis=0, keepdims=True))
        idxs.append(idx)
        picks.append(pick)
        cur = jnp.where(pick, ninf, cur)
    total = ws[0]
    for w in ws[1:]:
        total = total + w
    ws = [w / total * ROUTED_SCALE for w in ws]
    onehot = jnp.zeros((N_EXPERTS, tm), F32)
    for pick in picks:
        onehot = jnp.where(pick, 1.0, onehot)
    tri = (lax.broadcasted_iota(jnp.int32, (tm, tm), 0) < lax.broadcasted_iota(jnp.int32, (tm, tm), 1))
    before = _dot(onehot.astype(BF16), jnp.where(tri, 1.0, 0.0).astype(BF16)) + base_ref[...]
    ranks = [jnp.sum(jnp.where(pick, before, 0.0), axis=0, keepdims=True).astype(jnp.int32) for pick in picks]
    base_ref[...] += jnp.sum(onehot, axis=1, keepdims=True)
    return idxs, ws, ranks


def _norm_kernel(*refs, has_resid, modulated, with_router, out_dtype):
    it = iter(refs)
    x_ref = next(it)
    if has_resid:
        o_ref = next(it)
        gate_ref = next(it)
    g_ref = next(it)
    if modulated:
        shift_ref = next(it)
        scale_ref = next(it)
    if with_router:
        rwt_ref = next(it)
        rb_ref = next(it)
    if has_resid:
        xn_ref = next(it)
    h_ref = next(it)
    if with_router:
        hp_ref = next(it)
        eidx_ref = next(it)
        ew_ref = next(it)
        rank_ref = next(it)
        cnt_ref = next(it)
        base_ref = next(it)

        @pl.when(pl.program_id(0) == 0)
        def _():
            base_ref[...] = jnp.zeros_like(base_ref)

    x = x_ref[...]
    if has_resid:
        x = x + gate_ref[...] * o_ref[...].astype(F32)
        xn_ref[...] = x
    y = x * lax.rsqrt(jnp.mean(x * x, axis=-1, keepdims=True) + EPS)
    y = y * g_ref[...]
    if modulated:
        y = y * (1.0 + scale_ref[...]) + shift_ref[...]
    h_ref[...] = y.astype(out_dtype)
    if with_router:
        hp_ref[...] = _pack_bf16_pairs(y)
        idxs, ws, ranks = _router_select(y, rwt_ref, rb_ref, base_ref)
        for k in range(MOE_TOP_K):
            eidx_ref[k:k + 1, :] = idxs[k]
            ew_ref[k:k + 1, :] = ws[k]
            rank_ref[k:k + 1, :] = ranks[k]
        cnt_ref[...] = base_ref[...].astype(jnp.int32)


def _norm_call(x, n_rows, gain, *, resid=None, mod=None, router=None, out_dtype=BF16):
    d = x.shape[1]
    tm = ROW_TILE
    nblk = n_rows // tm
    lat_blocks = SEQ // tm

    def cls_map(i):
        return (jnp.where(i >= lat_blocks, 1, 0), 0, 0)

    row = pl.BlockSpec((tm, d), lambda i: (i, 0))
    vec = pl.BlockSpec((None, 1, d), cls_map)
    args, in_specs = [x], [row]
    if resid is not None:
        o, gate = resid
        args += [o, gate.reshape(2, 1, d)]
        in_specs += [row, vec]
    args.append(gain.reshape(1, d))
    in_specs.append(pl.BlockSpec((1, d), lambda i: (0, 0)))
    if mod is not None:
        shift, scale = mod
        args += [shift.reshape(2, 1, d), scale.reshape(2, 1, d)]
        in_specs += [vec, vec]
    if router is not None:
        rw, rb = router
        args += [rw.T, rb.reshape(N_EXPERTS, 1)]
        in_specs += [pl.BlockSpec((N_EXPERTS, d), lambda i: (0, 0)),
                     pl.BlockSpec((N_EXPERTS, 1), lambda i: (0, 0))]
    out_shape, out_specs = [], []
    if resid is not None:
        out_shape.append(jax.ShapeDtypeStruct((n_rows, d), F32))
        out_specs.append(row)
    out_shape.append(jax.ShapeDtypeStruct((n_rows, d), out_dtype))
    out_specs.append(row)
    if router is not None:
        out_shape += [jax.ShapeDtypeStruct((n_rows, d // 2), jnp.uint32),
                      jax.ShapeDtypeStruct((MOE_TOP_K, n_rows), jnp.int32),
                      jax.ShapeDtypeStruct((MOE_TOP_K, n_rows), F32),
                      jax.ShapeDtypeStruct((MOE_TOP_K, n_rows), jnp.int32),
                      jax.ShapeDtypeStruct((N_EXPERTS, 1), jnp.int32)]
        topk = pl.BlockSpec((MOE_TOP_K, tm), lambda i: (0, i))
        out_specs += [pl.BlockSpec((tm, d // 2), lambda i: (i, 0)), topk, topk, topk,
                      pl.BlockSpec((N_EXPERTS, 1), lambda i: (0, 0))]
    return pl.pallas_call(
        functools.partial(_norm_kernel, has_resid=resid is not None, modulated=mod is not None,
                          with_router=router is not None, out_dtype=out_dtype),
        grid=(nblk,),
        in_specs=in_specs,
        out_specs=out_specs,
        out_shape=out_shape,
        scratch_shapes=[pltpu.VMEM((N_EXPERTS, 1), F32)] if router is not None else [],
        compiler_params=_cparams(("arbitrary",), 16 * tm * d * 4),
        name="norm_router" if router is not None else "norm",
    )(*args)


def _mm_kernel(a_ref, w_ref, o_ref, wb_ref):
    @pl.when(pl.program_id(1) == 0)
    def _():
        wb_ref[...] = w_ref[...].astype(BF16)

    o_ref[...] = _dot(a_ref[...], wb_ref[...]).astype(o_ref.dtype)


def _matmul(a, w, m_rows, out_dtype, tm, tn):
    k, n = w.shape
    assert m_rows % tm == 0 and n % tn == 0 and a.shape[1] == k
    osz = jnp.dtype(out_dtype).itemsize
    vmem = 2 * k * tn * 4 + k * tn * 2 + 2 * tm * k * 2 + 2 * tm * tn * osz + tm * tn * 4
    return pl.pallas_call(
        _mm_kernel,
        grid=(n // tn, m_rows // tm),
        in_specs=[pl.BlockSpec((tm, k), lambda j, i: (i, 0)),
                  pl.BlockSpec((k, tn), lambda j, i: (0, j))],
        out_specs=pl.BlockSpec((tm, tn), lambda j, i: (i, j)),
        out_shape=jax.ShapeDtypeStruct((m_rows, n), out_dtype),
        scratch_shapes=[pltpu.VMEM((k, tn), BF16)],
        compiler_params=_cparams(("arbitrary", "arbitrary"), vmem + (4 << 20)),
        name="matmul",
    )(a, w)


def _pair_swap_matrix(w):
    r = lax.broadcasted_iota(jnp.int32, (w, w), 0)
    c = lax.broadcasted_iota(jnp.int32, (w, w), 1)
    return jnp.where(r == jnp.bitwise_xor(c, 1), 1.0, 0.0).astype(BF16)


def _rope(x, cos_e, sin_s, swap):
    swapped = _dot(x.astype(BF16), swap)
    return x * cos_e + swapped * sin_s


def _rms(x, g):
    return x * lax.rsqrt(jnp.mean(x * x, axis=-1, keepdims=True) + EPS) * g


def _mla_prep1_kernel(lat_ref, qg_ref, kvg_ref, cos_ref, sin_ref, nq_ref, nkv_ref, kr_ref):
    lat = lat_ref[...]
    nq_ref[...] = _rms(lat[:, :MLA_Q_RANK], qg_ref[...]).astype(BF16)
    nkv_ref[...] = _rms(lat[:, MLA_Q_RANK:MLA_Q_RANK + MLA_KV_RANK], kvg_ref[...]).astype(BF16)
    kr = lat[:, MLA_Q_RANK + MLA_KV_RANK:]
    kr_ref[...] = _rope(kr, cos_ref[...], sin_ref[...], _pair_swap_matrix(kr.shape[1])).astype(BF16)


def _mla_prep1(lat, q_norm, kv_norm, cos_t, sin_t):
    n, wdt = lat.shape
    tm = ROW_TILE
    row = lambda w: pl.BlockSpec((tm, w), lambda i: (i, 0))
    one = lambda w: pl.BlockSpec((1, w), lambda i: (0, 0))
    return pl.pallas_call(
        _mla_prep1_kernel,
        grid=(n // tm,),
        in_specs=[row(wdt), one(MLA_Q_RANK), one(MLA_KV_RANK), row(128), row(128)],
        out_specs=[row(MLA_Q_RANK), row(MLA_KV_RANK), row(128)],
        out_shape=[jax.ShapeDtypeStruct((n, MLA_Q_RANK), BF16),
                   jax.ShapeDtypeStruct((n, MLA_KV_RANK), BF16),
                   jax.ShapeDtypeStruct((n, 128), BF16)],
        compiler_params=_cparams(("arbitrary",), 8 << 20),
        name="mla_prep1",
    )(lat, q_norm.reshape(1, -1), kv_norm.reshape(1, -1), cos_t, sin_t)


def _mla_prep2_kernel(q_ref, kn_ref, kr_ref, cos_ref, sin_ref, qo_ref, ko_ref, *, scale):
    cos_e = cos_ref[...]
    sin_s = sin_ref[...]
    kr = kr_ref[...]
    swap = _pair_swap_matrix(MLA_QK_PAD - MLA_NOPE)
    for h in range(MLA_HEADS):
        lo = h * MLA_QK_PAD
        qo_ref[:, lo:lo + MLA_NOPE] = (q_ref[:, lo:lo + MLA_NOPE].astype(F32) * scale).astype(BF16)
        qr = q_ref[:, lo + MLA_NOPE:lo + MLA_QK_PAD].astype(F32)
        qo_ref[:, lo + MLA_NOPE:lo + MLA_QK_PAD] = (_rope(qr, cos_e, sin_s, swap) * scale).astype(BF16)
        ko_ref[:, lo:lo + MLA_NOPE] = kn_ref[:, h * MLA_NOPE:(h + 1) * MLA_NOPE]
        ko_ref[:, lo + MLA_NOPE:lo + MLA_QK_PAD] = kr


def _mla_prep2(q_raw, kv_all, kr, cos_t, sin_t):
    n = q_raw.shape[0]
    tm = ROW_TILE
    wq = MLA_HEADS * MLA_QK_PAD
    row = lambda w: pl.BlockSpec((tm, w), lambda i: (i, 0))
    return pl.pallas_call(
        functools.partial(_mla_prep2_kernel, scale=(MLA_NOPE + MLA_ROPE) ** -0.5 * LOG2E),
        grid=(n // tm,),
        in_specs=[row(wq), row(MLA_HEADS * MLA_NOPE), row(128), row(128), row(128)],
        out_specs=[row(wq), row(wq)],
        out_shape=[jax.ShapeDtypeStruct((n, wq), BF16), jax.ShapeDtypeStruct((n, wq), BF16)],
        compiler_params=_cparams(("arbitrary",), 16 << 20),
        name="mla_prep2",
    )(q_raw, kv_all, kr, cos_t, sin_t)


def _gqa_prep_kernel(qkv_ref, qg_ref, kg_ref, cos_ref, sin_ref, qo_ref, ko_ref, *, scale):
    cos_e = cos_ref[...]
    sin_s = sin_ref[...]
    swap = _pair_swap_matrix(HEAD_DIM)
    for h in range(GQA_HEADS):
        xh = qkv_ref[:, h * HEAD_DIM:(h + 1) * HEAD_DIM].astype(F32)
        qo_ref[:, h * HEAD_DIM:(h + 1) * HEAD_DIM] = (
            _rope(_rms(xh, qg_ref[...]), cos_e, sin_s, swap) * scale).astype(BF16)
    for h in range(GQA_KV):
        lo = (GQA_HEADS + h) * HEAD_DIM
        xh = qkv_ref[:, lo:lo + HEAD_DIM].astype(F32)
        ko_ref[:, h * HEAD_DIM:(h + 1) * HEAD_DIM] = _rope(_rms(xh, kg_ref[...]), cos_e, sin_s, swap).astype(BF16)


def _gqa_prep(qkv, q_norm, k_norm, cos_t, sin_t):
    n = qkv.shape[0]
    tm = ROW_TILE
    wqk = (GQA_HEADS + GQA_KV) * HEAD_DIM
    row = lambda w: pl.BlockSpec((tm, w), lambda i: (i, 0))
    one = pl.BlockSpec((1, HEAD_DIM), lambda i: (0, 0))
    return pl.pallas_call(
        functools.partial(_gqa_prep_kernel, scale=HEAD_DIM ** -0.5 * LOG2E),
        grid=(n // tm,),
        in_specs=[row(wqk), one, one, row(HEAD_DIM), row(HEAD_DIM)],
        out_specs=[row(GQA_HEADS * HEAD_DIM), row(GQA_KV * HEAD_DIM)],
        out_shape=[jax.ShapeDtypeStruct((n, GQA_HEADS * HEAD_DIM), BF16),
                   jax.ShapeDtypeStruct((n, GQA_KV * HEAD_DIM), BF16)],
        compiler_params=_cparams(("arbitrary",), 16 << 20),
        name="gqa_prep",
    )(qkv, q_norm.reshape(1, -1), k_norm.reshape(1, -1), cos_t, sin_t)


def _flash_kernel(q_ref, k_ref, v_ref, o_ref, sa_ref, sb_ref, *, group, dk, dv, tk, n_lat, q_scale):
    tq = q_ref.shape[0]
    q = jnp.concatenate([q_ref[:, g * dk:(g + 1) * dk] for g in range(group)], axis=0) if group > 1 else q_ref[...]
    if q_scale is not None:
        q = (q.astype(F32) * q_scale).astype(BF16)

    s = _dot_nt(q, k_ref[SEQ:SEQ + CTX_LEN, :])
    m = jnp.max(s, axis=-1, keepdims=True)
    p = jnp.exp2(s - m)
    l = jnp.sum(p, axis=-1, keepdims=True)
    acc = _dot(p.astype(BF16), v_ref[SEQ:SEQ + CTX_LEN, :])

    if n_lat:
        def rows(ref, j):
            return ref[pl.ds(pl.multiple_of(j * tk, tk), tk), :]

        def update(s, vb, m, l, acc):
            m_new = jnp.maximum(m, jnp.max(s, axis=-1, keepdims=True))
            alpha = jnp.exp2(m - m_new)
            p = jnp.exp2(s - m_new)
            l = alpha * l + jnp.sum(p, axis=-1, keepdims=True)
            return m_new, l, alpha * acc + _dot(p.astype(BF16), vb)

        def pair(jj, carry, last):
            j0 = 2 * jj
            sb_ref[...] = _dot_nt(q, rows(k_ref, j0 + 1))
            carry = update(sa_ref[...], rows(v_ref, j0), *carry)
            if not last:
                sa_ref[...] = _dot_nt(q, rows(k_ref, j0 + 2))
            return update(sb_ref[...], rows(v_ref, j0 + 1), *carry)

        sa_ref[...] = _dot_nt(q, rows(k_ref, 0))
        carry = lax.fori_loop(0, n_lat // 2 - 1, lambda jj, c: pair(jj, c, False), (m, l, acc))
        m, l, acc = pair(n_lat // 2 - 1, carry, True)

    o = (acc / l).astype(o_ref.dtype)
    for g in range(group):
        o_ref[:, g * dv:(g + 1) * dv] = o[g * tq:(g + 1) * tq]


def _flash(q_arr, k_arr, v_arr, *, n_kv_heads, group, dk, dv, q_col0, k_col0, v_col0, ctx_queries, tq, tk,
           q_scale=None):
    n_tok = k_arr.shape[0]
    if ctx_queries:
        n_q, q_blk0, n_lat = CTX_LEN, SEQ // tq, 0
    else:
        n_q, q_blk0, n_lat = SEQ, 0, SEQ // tk
        assert n_lat % 2 == 0
    m = group * tq
    s_shape = (m, tk) if n_lat else (8, 128)
    kv_mode = pl.Buffered(1) if m * tk * 4 >= (8 << 20) else None
    vmem = (2 * n_tok * (dk + dv) * 2 + 4 * tq * group * (dk + dv) * 2
            + 5 * m * max(tk, CTX_LEN) * 4 + 4 * m * dv * 4)
    return pl.pallas_call(
        functools.partial(_flash_kernel, group=group, dk=dk, dv=dv, tk=tk, n_lat=n_lat, q_scale=q_scale),
        grid=(n_kv_heads, n_q // tq),
        in_specs=[pl.BlockSpec((tq, group * dk), lambda h, i: (q_blk0 + i, q_col0 + h)),
                  pl.BlockSpec((n_tok, dk), lambda h, i: (0, k_col0 + h), pipeline_mode=kv_mode),
                  pl.BlockSpec((n_tok, dv), lambda h, i: (0, v_col0 + h), pipeline_mode=kv_mode)],
        out_specs=pl.BlockSpec((tq, group * dv), lambda h, i: (i, h)),
        out_shape=jax.ShapeDtypeStruct((n_q, n_kv_heads * group * dv), BF16),
        scratch_shapes=[pltpu.VMEM(s_shape, F32), pltpu.VMEM(s_shape, F32)],
        compiler_params=_cparams(("arbitrary", "arbitrary"), vmem + (8 << 20)),
        name="flash_ctx" if ctx_queries else "flash",
    )(q_arr, k_arr, v_arr)


def _na_start_row(b):
    return jnp.clip(NA_RB * b - NA_KH // 2, 0, GRID_H - NA_U)


def _na_kernel(q_ref, k_ref, v_ref, bias_ref, o_ref, *, scale):
    b = pl.program_id(1)
    start = pl.multiple_of(_na_start_row(b) * GRID_W, GRID_W)
    nwin = NA_U * GRID_W
    q = (q_ref[...].astype(F32) * scale).astype(BF16)
    s_loc = _dot_nt(q, k_ref[pl.ds(start, nwin), :]) + bias_ref[...]
    s_ctx = _dot_nt(q, k_ref[SEQ:SEQ + CTX_LEN, :])
    m = jnp.maximum(jnp.max(s_loc, axis=-1, keepdims=True), jnp.max(s_ctx, axis=-1, keepdims=True))
    p_loc = jnp.exp(s_loc - m)
    p_ctx = jnp.exp(s_ctx - m)
    l = jnp.sum(p_loc, axis=-1, keepdims=True) + jnp.sum(p_ctx, axis=-1, keepdims=True)
    o = _dot(p_loc.astype(BF16), v_ref[pl.ds(start, nwin), :]) + _dot(p_ctx.astype(BF16), v_ref[SEQ:SEQ + CTX_LEN, :])
    o_ref[...] = (o / l).astype(o_ref.dtype)


def _na_bias_tables(rpb):
    h = rpb.shape[0]
    w = np.arange(GRID_W)
    c0 = np.clip(w - NA_KW // 2, 0, GRID_W - NA_KW)
    colmask = (w[None, :] >= c0[:, None]) & (w[None, :] < c0[:, None] + NA_KW)
    colidx = np.clip(w[None, :] - w[:, None] + NA_KW - 1, 0, 2 * NA_KW - 2)
    slabs = jnp.where(colmask[None, None], rpb[:, :, colidx], NEG)
    slabs = jnp.concatenate([slabs, jnp.full((h, 1, GRID_W, GRID_W), NEG, rpb.dtype)], axis=1)
    masked = 2 * NA_KH - 1
    nb = GRID_H // NA_RB
    tabs = []
    for b in (0, 1, nb - 1):
        start = int(np.clip(NA_RB * b - NA_KH // 2, 0, GRID_H - NA_U))
        a = np.full((NA_RB, NA_U), masked, np.int32)
        for j in range(NA_RB):
            r = NA_RB * b + j
            r0 = int(np.clip(r - NA_KH // 2, 0, GRID_H - NA_KH))
            for u in range(NA_U):
                kr = start + u
                if r0 <= kr < r0 + NA_KH:
                    a[j, u] = kr - r + NA_KH - 1
        t = slabs[:, a]
        tabs.append(t.transpose(0, 1, 3, 2, 4).reshape(h, NA_RB * GRID_W, NA_U * GRID_W))
    return jnp.stack(tabs, axis=1)


def _na_attention(qkv, bias_tabs):
    n_tok = qkv.shape[0]
    nb = GRID_H // NA_RB
    tq = NA_RB * GRID_W
    nwin = NA_U * GRID_W

    def cls(b):
        return jnp.where(b == 0, 0, jnp.where(b == nb - 1, 2, 1))

    vmem = 4 * n_tok * HEAD_DIM * 2 + 2 * tq * nwin * 4 + 4 * tq * (nwin + CTX_LEN) * 4
    return pl.pallas_call(
        functools.partial(_na_kernel, scale=HEAD_DIM ** -0.5),
        grid=(NA_HEADS, nb),
        in_specs=[pl.BlockSpec((tq, HEAD_DIM), lambda h, b: (b, h)),
                  pl.BlockSpec((n_tok, HEAD_DIM), lambda h, b: (0, NA_HEADS + h)),
                  pl.BlockSpec((n_tok, HEAD_DIM), lambda h, b: (0, 2 * NA_HEADS + h)),
                  pl.BlockSpec((None, None, tq, nwin), lambda h, b: (h, cls(b), 0, 0))],
        out_specs=pl.BlockSpec((tq, HEAD_DIM), lambda h, b: (b, h)),
        out_shape=jax.ShapeDtypeStruct((SEQ, NA_HEADS * HEAD_DIM), BF16),
        compiler_params=_cparams(("arbitrary", "arbitrary"), vmem + (8 << 20)),
        name="na_attention",
    )(qkv, qkv, qkv, bias_tabs)


def _dispatch_kernel(tail_ref, nu_ref, slot_ref, h_ref, xg_hbm, zbuf, sem, zsem, *, tb, n_blocks):
    te = EXPERT_TILE

    @pl.when(pl.program_id(0) == 0)
    def _():
        zbuf[...] = jnp.zeros_like(zbuf)

        def zero_tile(row0, wait):
            cp = pltpu.make_async_copy(zbuf, xg_hbm.at[pl.ds(pl.multiple_of(row0, te), te)], zsem)
            if wait:
                cp.wait()
            else:
                cp.start()

        def zero_tail(e, wait):
            @pl.when(tail_ref[e] >= 0)
            def _():
                zero_tile(tail_ref[e], wait)

        for wait in (False, True):
            lax.fori_loop(0, N_EXPERTS, lambda e, c: (zero_tail(e, wait), c)[1], 0)
            lax.fori_loop(nu_ref[0], n_blocks, lambda b, c: (zero_tile(b * te, wait), c)[1], 0)

    def body(t, carry):
        for k in range(MOE_TOP_K):
            pltpu.make_async_copy(h_ref.at[pl.ds(t, 1)], xg_hbm.at[pl.ds(slot_ref[k, t], 1)], sem).start(
                priority=k % 2)
        return carry

    lax.fori_loop(0, tb, body, 0, unroll=DMA_LOOP_UNROLL)
    for k in range(MOE_TOP_K):
        pltpu.make_async_copy(h_ref, xg_hbm.at[pl.ds(0, tb)], sem).wait()


def _dispatch(h32, slot_t, tail_start, n_used, n_rows, n_slots):
    d = h32.shape[1]
    tb = ROW_TILE
    grid_spec = pltpu.PrefetchScalarGridSpec(
        num_scalar_prefetch=2,
        grid=(n_rows // tb,),
        in_specs=[pl.BlockSpec((MOE_TOP_K, tb), lambda i, tl, nu: (0, i), memory_space=pltpu.SMEM),
                  pl.BlockSpec((tb, d), lambda i, tl, nu: (i, 0))],
        out_specs=pl.BlockSpec(memory_space=pl.ANY),
        scratch_shapes=[pltpu.VMEM((EXPERT_TILE, d), h32.dtype), pltpu.SemaphoreType.DMA(()),
                        pltpu.SemaphoreType.DMA(())],
    )
    return pl.pallas_call(
        functools.partial(_dispatch_kernel, tb=tb, n_blocks=n_slots // EXPERT_TILE),
        grid_spec=grid_spec,
        out_shape=jax.ShapeDtypeStruct((n_slots, d), h32.dtype),
        compiler_params=_cparams(("arbitrary",), 16 << 20),
        name="moe_dispatch",
    )(tail_start, n_used, slot_t, h32)


FFN_LEAD = 2


def _ffn_kernel(be_ref, ord_ref, nu_ref, x_ref, wg_ref, wu_ref, wd_ref, o_ref, wgb, wub, wdb, *, packed, nb):
    b = pl.program_id(0)
    c = b - FFN_LEAD

    def at(ref, i):
        return ref[jnp.clip(i, 0, nb - 1)]

    def starts_expert(i):
        return jnp.logical_or(b == 0, at(be_ref, i) != at(be_ref, i - 1))

    @pl.when(starts_expert(c))
    def _():
        wdb[at(ord_ref, c) % 2] = wd_ref[...].astype(BF16)

    @pl.when(jnp.logical_and(c >= 0, c < nu_ref[0]))
    def _():
        slot = at(ord_ref, c) % 2
        x = _unpack_bf16_pairs(x_ref[...]).astype(BF16) if packed else x_ref[...]
        g = _dot(x, wgb[slot])
        u = _dot(x, wub[slot])
        a = (g * jax.nn.sigmoid(g) * u).astype(BF16)
        y = _dot(a, wdb[slot])
        o_ref[...] = _pack_bf16_pairs(y) if packed else y.astype(o_ref.dtype)

    @pl.when(c >= nu_ref[0])
    def _():
        o_ref[...] = jnp.zeros_like(o_ref)

    @pl.when(starts_expert(b))
    def _():
        wgb[at(ord_ref, b) % 2] = wg_ref[...].astype(BF16)

    @pl.when(starts_expert(b - 1))
    def _():
        wub[at(ord_ref, b - 1) % 2] = wu_ref[...].astype(BF16)


def _expert_ffn(x_rows, block_expert, n_used, wg, wu, wd, layer, tb, packed):
    p, dx = x_rows.shape
    d, de = wg.shape[-2:]
    nb = p // tb
    out_dtype = x_rows.dtype
    vmem = 2 * 3 * d * de * 4 + 2 * 3 * d * de * 2 + 4 * tb * d * 2 + 4 * tb * de * 4 + 3 * tb * d * 4
    idx = jnp.arange(nb, dtype=jnp.int32)
    be = jnp.where(idx < n_used[0], block_expert, block_expert[jnp.maximum(n_used[0] - 1, 0)])
    ordinal = jnp.cumsum(jnp.concatenate([jnp.zeros((1,), jnp.int32),
                                          (be[1:] != be[:-1]).astype(jnp.int32)])).astype(jnp.int32)

    def blk(i):
        return jnp.clip(i, 0, nb - 1)

    def wspec(shape, lag):
        return pl.BlockSpec((None, None) + shape, lambda b, be, od, nu: (layer, be[blk(b - lag)], 0, 0))

    grid_spec = pltpu.PrefetchScalarGridSpec(
        num_scalar_prefetch=3,
        grid=(nb + FFN_LEAD,),
        in_specs=[pl.BlockSpec((tb, dx), lambda b, be, od, nu: (jnp.minimum(blk(b - FFN_LEAD), nu[0] - 1), 0)),
                  wspec((d, de), 0), wspec((d, de), 1), wspec((de, d), 2)],
        out_specs=pl.BlockSpec((tb, dx), lambda b, be, od, nu: (blk(b - FFN_LEAD), 0)),
        scratch_shapes=[pltpu.VMEM((2, d, de), BF16), pltpu.VMEM((2, d, de), BF16), pltpu.VMEM((2, de, d), BF16)],
    )
    return pl.pallas_call(
        functools.partial(_ffn_kernel, packed=packed, nb=nb),
        grid_spec=grid_spec,
        out_shape=jax.ShapeDtypeStruct((p, dx), out_dtype),
        compiler_params=_cparams(("arbitrary",), vmem + (4 << 20)),
        name="expert_ffn",
    )(be, ordinal, n_used, x_rows, wg, wu, wd)


def _combine_kernel(*refs, tc, nblk, modulated, emit_x):
    it = iter(refs)
    s_cur, s_nxt, y_hbm, ew_ref, ysh_ref, x_ref, gate_ref, g_ref = [next(it) for _ in range(8)]
    if modulated:
        shift_ref, scale_ref = next(it), next(it)
    if emit_x:
        o_ref = next(it)
    h_ref, buf, sem = next(it), next(it), next(it)
    i = pl.program_id(0)

    def gather(slot_ref, bslot):
        def body(t, carry):
            for k in range(MOE_TOP_K):
                pltpu.make_async_copy(y_hbm.at[pl.ds(slot_ref[k, t], 1)],
                                      buf.at[bslot, k, pl.ds(t, 1)], sem.at[bslot]).start(priority=k % 2)
            return carry

        lax.fori_loop(0, tc, body, 0, unroll=DMA_LOOP_UNROLL)

    @pl.when(i == 0)
    def _():
        gather(s_cur, 0)

    cur = i % 2

    @pl.when(i + 1 < nblk)
    def _():
        @pl.when(cur == 0)
        def _():
            gather(s_nxt, 1)

        @pl.when(cur == 1)
        def _():
            gather(s_nxt, 0)

    def finish(bslot):
        for k in range(MOE_TOP_K):
            pltpu.make_async_copy(y_hbm.at[pl.ds(0, tc)], buf.at[bslot, k], sem.at[bslot]).wait()
        ew = ew_ref[...]
        acc = ysh_ref[...].astype(F32)
        for k in range(MOE_TOP_K):
            acc = acc + ew[:, k:k + 1] * _unpack_bf16_pairs(buf[bslot, k])
        xn = x_ref[...] + gate_ref[...] * acc
        if emit_x:
            o_ref[...] = xn
        hn = _rms(xn, g_ref[...])
        if modulated:
            hn = hn * (1.0 + scale_ref[...]) + shift_ref[...]
        h_ref[...] = hn.astype(h_ref.dtype)

    @pl.when(cur == 0)
    def _():
        finish(0)

    @pl.when(cur == 1)
    def _():
        finish(1)


def _combine(y, slot_t, ew, ysh, x, gate, n_rows, gain, mod):
    d = x.shape[1]
    tc = COMBINE_TILE
    nblk = n_rows // tc
    lat_blocks = SEQ // tc
    row = pl.BlockSpec((tc, d), lambda i: (i, 0))
    vec = pl.BlockSpec((None, 1, d), lambda i: (jnp.where(i >= lat_blocks, 1, 0), 0, 0))
    vmem = 2 * MOE_TOP_K * tc * d * 2 + 12 * tc * d * 4
    args = [slot_t, slot_t, y, ew, ysh, x, gate.reshape(2, 1, d), gain.reshape(1, d)]
    in_specs = [pl.BlockSpec((MOE_TOP_K, tc), lambda i: (0, i), memory_space=pltpu.SMEM),
                pl.BlockSpec((MOE_TOP_K, tc), lambda i: (0, jnp.minimum(i + 1, nblk - 1)),
                             memory_space=pltpu.SMEM),
                pl.BlockSpec(memory_space=pl.ANY),
                pl.BlockSpec((tc, MOE_TOP_K), lambda i: (i, 0)),
                row, row, vec, pl.BlockSpec((1, d), lambda i: (0, 0))]
    if mod is not None:
        args += [mod[0].reshape(2, 1, d), mod[1].reshape(2, 1, d)]
        in_specs += [vec, vec]
        out_shape = [jax.ShapeDtypeStruct((n_rows, d), F32), jax.ShapeDtypeStruct((n_rows, d), BF16)]
    else:
        out_shape = [jax.ShapeDtypeStruct((n_rows, d), F32)]
    return pl.pallas_call(
        functools.partial(_combine_kernel, tc=tc, nblk=nblk, modulated=mod is not None, emit_x=mod is not None),
        grid=(nblk,),
        in_specs=in_specs,
        out_specs=[row] * len(out_shape),
        out_shape=out_shape,
        scratch_shapes=[pltpu.VMEM((2, MOE_TOP_K, tc, d // 2), jnp.uint32), pltpu.SemaphoreType.DMA((2,))],
        compiler_params=_cparams(("arbitrary",), vmem + (4 << 20)),
        name="moe_combine",
    )(*args)


def _routing_tables(counts, n_rows):
    tb = EXPERT_TILE
    n_blocks = -(-(n_rows * MOE_TOP_K) // tb) + N_EXPERTS
    padded = (counts + tb - 1) // tb * tb
    pad_end = jnp.cumsum(padded)
    pad_start = (pad_end - padded).astype(jnp.int32)
    tail_start = jnp.where(padded > 0, pad_end - tb, -1).astype(jnp.int32)
    blk_start = jnp.arange(n_blocks, dtype=jnp.int32) * tb
    block_expert = jnp.sum(blk_start[:, None] >= pad_end[None, :], axis=1)
    block_expert = jnp.minimum(block_expert, N_EXPERTS - 1).astype(jnp.int32)
    n_used = (pad_end[-1] // tb).astype(jnp.int32).reshape(1)
    return pad_start, tail_start, block_expert, n_used, n_blocks * tb


def _moe(layer, x_new, h, h32, eidx_t, ew_t, rank_t, counts, gate, n_rows, wg, wu, wd, sg, su, sd, next_gain, next_mod):
    pad_start, tail_start, block_expert, n_used, n_slots = _routing_tables(counts[:, 0], n_rows)
    e_iota = jnp.arange(N_EXPERTS, dtype=jnp.int32)
    slot_t = jnp.sum(jnp.where(eidx_t[:, :, None] == e_iota, pad_start, 0), axis=-1) + rank_t
    xg = _dispatch(h32, slot_t, tail_start, n_used, n_rows, n_slots)
    y = _expert_ffn(xg, block_expert, n_used, wg, wu, wd, layer, EXPERT_TILE, True)
    tm = MM_TM if n_rows % MM_TM == 0 else 512
    ysh = _expert_ffn(h, jnp.zeros((n_rows // tm,), jnp.int32), jnp.full((1,), n_rows // tm, jnp.int32),
                      sg[:, None], su[:, None], sd[:, None], layer, tm, False)
    return _combine(y, slot_t, ew_t.T, ysh, x_new, gate, n_rows, next_gain, next_mod)


def _rope_tables(d, width):
    t = jnp.arange(SEQ)
    row = (t // GRID_W).astype(F32)
    col = (t % GRID_W).astype(F32)
    n = d // 4
    inv = ROPE_THETA ** (-jnp.arange(n, dtype=F32) / n)
    ang = jnp.concatenate([row[:, None] * inv, col[:, None] * inv], axis=-1)
    ang = jnp.concatenate([ang, jnp.zeros((CTX_LEN, d // 2), F32)], axis=0)
    cos = jnp.repeat(jnp.cos(ang), 2, axis=1)
    sin = jnp.repeat(jnp.sin(ang), 2, axis=1) * jnp.tile(jnp.array([-1.0, 1.0], F32), d // 2)
    if width > d:
        cos = jnp.concatenate([cos, jnp.ones((N_TOK, width - d), F32)], axis=1)
        sin = jnp.concatenate([sin, jnp.zeros((N_TOK, width - d), F32)], axis=1)
    return cos, sin


def _mixer_even(h, w_in, rpb, q_norm, w_uq, kv_norm, w_ukv, w_out):
    n_na = 3 * NA_HEADS * HEAD_DIM
    qkv = _matmul(h, w_in[:, :n_na], N_TOK, BF16, MM_TM, 1024)
    w_tail = jnp.pad(w_in[:, n_na:], ((0, 0), (0, 64)))
    lat = _matmul(h, w_tail, N_TOK, F32, MM_TM, 384)
    cos_t, sin_t = _rope_tables(MLA_ROPE, 128)
    nq, nkv, kr = _mla_prep1(lat, q_norm, kv_norm, cos_t, sin_t)
    wq = w_uq.reshape(MLA_Q_RANK, MLA_HEADS, MLA_NOPE + MLA_ROPE)
    wq = jnp.pad(wq, ((0, 0), (0, 0), (0, MLA_QK_PAD - MLA_NOPE - MLA_ROPE))).reshape(MLA_Q_RANK, -1)
    wkv = w_ukv.reshape(MLA_KV_RANK, MLA_HEADS, MLA_NOPE + MLA_V)
    wkv = jnp.concatenate([wkv[:, :, :MLA_NOPE].reshape(MLA_KV_RANK, -1),
                           wkv[:, :, MLA_NOPE:].reshape(MLA_KV_RANK, -1)], axis=1)
    q_raw = _matmul(nq, wq, N_TOK, F32, MM_TM, 1024)
    kv_all = _matmul(nkv, wkv, N_TOK, BF16, MM_TM, 1024)
    q_mla, k_cat = _mla_prep2(q_raw, kv_all, kr, cos_t, sin_t)

    o_na = _na_attention(qkv, _na_bias_tables(rpb))
    mla = dict(n_kv_heads=MLA_HEADS, group=1, dk=MLA_QK_PAD, dv=MLA_V, q_col0=0, k_col0=0, v_col0=MLA_HEADS)
    o_mla = _flash(q_mla, k_cat, kv_all, ctx_queries=False, tq=1024, tk=2048, **mla)
    o_mla_c = _flash(q_mla, k_cat, kv_all, ctx_queries=True, tq=256, tk=512, **mla)
    na = dict(n_kv_heads=NA_HEADS, group=1, dk=HEAD_DIM, dv=HEAD_DIM, q_col0=0, k_col0=NA_HEADS, v_col0=2 * NA_HEADS)
    o_na_c = _flash(qkv, qkv, qkv, ctx_queries=True, tq=256, tk=512, q_scale=HEAD_DIM ** -0.5 * LOG2E, **na)
    o = jnp.concatenate([jnp.concatenate([o_na, o_mla], axis=1),
                         jnp.concatenate([o_na_c, o_mla_c], axis=1)], axis=0)
    return _matmul(o, w_out, N_TOK, F32, MM_TM, 1024)


def _mixer_odd_last(h, w_qkv, q_norm, k_norm, w_o):
    qkv = _matmul(h, w_qkv, N_TOK, BF16, MM_TM, 1024)
    cos_t, sin_t = _rope_tables(HEAD_DIM, HEAD_DIM)
    q_g, k_g = _gqa_prep(qkv, q_norm, k_norm, cos_t, sin_t)
    o = _flash(q_g, k_g, qkv, n_kv_heads=GQA_KV, group=GQA_GROUP, dk=HEAD_DIM, dv=HEAD_DIM,
               q_col0=0, k_col0=0, v_col0=GQA_HEADS + GQA_KV, ctx_queries=False, tq=256, tk=2048)
    return _matmul(o, w_o, SEQ, F32, 512, 1024)


def kernel(x, c, ctx, c_ctx, ada_w, ada_b, mix_norm, ffn_norm, ev_w_in, ev_rpb, ev_q_norm, ev_w_uq, ev_kv_norm, ev_w_ukv, ev_w_out, od_w_qkv, od_q_norm, od_k_norm, od_w_o, router_w, router_b, exp_w_gate, exp_w_up, exp_w_down, sh_w_gate, sh_w_up, sh_w_down, final_norm):
    assert DEPTH == 2 and x.shape == (1, SEQ, D_MODEL)
    mods = _adaln_mods(c, c_ctx, ada_w, ada_b)
    xt = jnp.concatenate([x[0], ctx[0]], axis=0)

    def parts(i):
        return [mods[i, :, k * D_MODEL:(k + 1) * D_MODEL] for k in range(6)]

    sh1, sc1, g1, sh2, sc2, g2 = parts(0)
    (h,) = _norm_call(xt, N_TOK, mix_norm[0], mod=(sh1, sc1))
    o = _mixer_even(h, ev_w_in[0], ev_rpb[0], ev_q_norm[0], ev_w_uq[0], ev_kv_norm[0], ev_w_ukv[0], ev_w_out[0])
    xt, h, h32, eidx_t, ew_t, rank_t, counts = _norm_call(xt, N_TOK, ffn_norm[0], resid=(o, g1), mod=(sh2, sc2),
                                                          router=(router_w[0], router_b[0]))
    n1_sh1, n1_sc1, n1_g1, n1_sh2, n1_sc2, n1_g2 = parts(1)
    xt, h = _moe(0, xt, h, h32, eidx_t, ew_t, rank_t, counts, g2, N_TOK, exp_w_gate, exp_w_up, exp_w_down,
                 sh_w_gate, sh_w_up, sh_w_down, mix_norm[1], (n1_sh1, n1_sc1))

    o = _mixer_odd_last(h, od_w_qkv[0], od_q_norm[0], od_k_norm[0], od_w_o[0])
    xl, h, h32, eidx_t, ew_t, rank_t, counts = _norm_call(xt, SEQ, ffn_norm[1], resid=(o, n1_g1),
                                                          mod=(n1_sh2, n1_sc2),
                                                          router=(router_w[1], router_b[1]))
    (out,) = _moe(1, xl, h, h32, eidx_t, ew_t, rank_t, counts, n1_g2, SEQ, exp_w_gate, exp_w_up, exp_w_down,
                  sh_w_gate, sh_w_up, sh_w_down, final_norm, None)
    return out[None]
```
